```python
import math
import jax, jax.numpy as jnp
from jax import lax
import numpy as np

D_MODEL = 2048
BATCH = 1
SEQ = 8192
DEPTH = 4

HEAD_DIM = 128
A_HEADS = 8
A_WIDTH = A_HEADS * HEAD_DIM
DILATED_BRANCHES = ((128, 1), (512, 4), (2048, 16))
A_QBLOCK = 128
B_WIDTH = D_MODEL - A_WIDTH
B_GROUP = 16
B_GROUPS = B_WIDTH // B_GROUP
B_STATE = 64
DT_MIN = 1e-3
DT_MAX = 1e-1
C_HEADS = D_MODEL // HEAD_DIM
C_WIDTH = C_HEADS * HEAD_DIM
GRID_W = 64
NA_ROWS_MAX = 8
NA_COLS = 16
T5_BUCKETS = 32
T5_MAX_DISTANCE = 1024
D_FF = 4 * D_MODEL
RMS_EPS = 1e-6
NEG_INF = -1e30
N_EVEN = (DEPTH + 1) // 2
N_ODD = DEPTH // 2

kernel_name = "hybrid_dilated_s5_neighbourhood_encoder"


def rmsnorm(x, g):
    x32 = x.astype(jnp.float32)
    y = x32 * lax.rsqrt(jnp.mean(x32 * x32, axis=-1, keepdims=True) + RMS_EPS)
    return (y * g.astype(jnp.float32)).astype(x.dtype)


def t5_bucket(rel):
    half = T5_BUCKETS // 2
    max_exact = half // 2
    n = jnp.abs(rel)
    nf = jnp.maximum(n, 1).astype(jnp.float32)
    large = max_exact + (jnp.log(nf / max_exact) / math.log(T5_MAX_DISTANCE / max_exact)
                         * (half - max_exact)).astype(jnp.int32)
    large = jnp.minimum(large, half - 1)
    return jnp.where(rel > 0, half, 0) + jnp.where(n < max_exact, n, large)


def dilated_branch(q, k, v, t5_table, window, dilation):
    b, h, s, hd = q.shape
    half = window // (2 * dilation)
    L = s // dilation
    nb = -(-L // A_QBLOCK)
    lp = nb * A_QBLOCK
    kb_len = A_QBLOCK + 2 * half
    scale = 1.0 / math.sqrt(hd)

    def strided(t):
        return t.reshape(b, h, L, dilation, hd).transpose(0, 1, 3, 2, 4)

    qs = jnp.pad(strided(q), ((0, 0), (0, 0), (0, 0), (0, lp - L), (0, 0)))
    qs = qs.reshape(b, h, dilation, nb, A_QBLOCK, hd)
    pad_k = ((0, 0), (0, 0), (0, 0), (half, lp - L + half), (0, 0))
    key_idx = jnp.arange(nb)[:, None] * A_QBLOCK + jnp.arange(kb_len)[None, :]
    ks = jnp.pad(strided(k), pad_k)[:, :, :, key_idx]
    vs = jnp.pad(strided(v), pad_k)[:, :, :, key_idx]

    off = jnp.arange(kb_len)[None, :] - half - jnp.arange(A_QBLOCK)[:, None]
    bias = jnp.transpose(t5_table[t5_bucket(off * dilation)], (2, 0, 1)).astype(jnp.float32)
    key_pos = key_idx - half
    valid = (jnp.abs(off) <= half)[None] & ((key_pos >= 0) & (key_pos < L))[:, None, :]

    sc = jnp.einsum('bhrnqc,bhrnkc->bhrnqk', qs, ks) * scale + bias[None, :, None, None]
    sc = jnp.where(valid, sc, NEG_INF)
    m = sc.max(axis=-1)
    p = jnp.exp(sc - m[..., None])
    l = p.sum(axis=-1)
    num = jnp.einsum('bhrnqk,bhrnkc->bhrnqc', p, vs)

    def unstride(t):
        t = t.reshape(b, h, dilation, lp, *t.shape[5:])[:, :, :, :L]
        t = jnp.moveaxis(t, 2, 3)
        return t.reshape(b, h, s, *t.shape[4:])

    return unstride(m), unstride(l), unstride(num)


def dilated_attention(q, k, v, t5_table):
    outs = [dilated_branch(q, k, v, t5_table, w, d) for (w, d) in DILATED_BRANCHES]
    m = jnp.stack([o[0] for o in outs])
    l = jnp.stack([o[1] for o in outs])
    num = jnp.stack([o[2] for o in outs])
    wgt = jnp.exp(m - m.max(axis=0, keepdims=True))
    return (wgt[..., None] * num).sum(0) / (wgt * l).sum(0)[..., None]


def s5_direction(u, lam_re, lam_im, log_step, b_re, b_im, c_re, c_im, reverse):
    f32 = jnp.float32
    step = jnp.exp(log_step.astype(f32))[:, None]
    lr = jnp.minimum(lam_re.astype(f32), -1e-4)
    li = lam_im.astype(f32)
    mag = jnp.exp(lr * step)
    ab_re = mag * jnp.cos(li * step)
    ab_im = mag * jnp.sin(li * step)
    den = lr * lr + li * li
    zr = ((ab_re - 1.0) * lr + ab_im * li) / den
    zi = (ab_im * lr - (ab_re - 1.0) * li) / den
    br = b_re.astype(f32)
    bi = b_im.astype(f32)
    bb_re = zr[..., None] * br - zi[..., None] * bi
    bb_im = zr[..., None] * bi + zi[..., None] * br
    bu_re = jnp.einsum('bsgc,gpc->bsgp', u, bb_re)
    bu_im = jnp.einsum('bsgc,gpc->bsgp', u, bb_im)
    a_re = jnp.broadcast_to(ab_re, bu_re.shape)
    a_im = jnp.broadcast_to(ab_im, bu_im.shape)

    def combine(e1, e2):
        a1r, a1i, b1r, b1i = e1
        a2r, a2i, b2r, b2i = e2
        return (a2r * a1r - a2i * a1i,
                a2r * a1i + a2i * a1r,
                a2r * b1r - a2i * b1i + b2r,
                a2r * b1i + a2i * b1r + b2i)

    _, _, xr, xi = lax.associative_scan(combine, (a_re, a_im, bu_re, bu_im), reverse=reverse, axis=1)
    return (jnp.einsum('bsgp,gcp->bsgc', xr, c_re.astype(f32))
            - jnp.einsum('bsgp,gcp->bsgc', xi, c_im.astype(f32)))


def neighbourhood_attention(q, k, v, rpb):
    b, h, s, hd = q.shape
    rows = s // GRID_W
    kr = min(NA_ROWS_MAX, rows)
    scale = 1.0 / math.sqrt(hd)
    r = jnp.arange(rows)
    c = jnp.arange(GRID_W)
    row_start = jnp.clip(r - kr // 2, 0, rows - kr)
    row_idx = row_start[:, None] + jnp.arange(kr)[None, :]
    col_start = jnp.clip(c - NA_COLS // 2, 0, GRID_W - NA_COLS)
    col_ok = (c[None, :] >= col_start[:, None]) & (c[None, :] < col_start[:, None] + NA_COLS)
    col_off = c[None, :] - c[:, None]
    row_off = row_idx - r[:, None]

    qg = q.reshape(b, h, rows, GRID_W, hd)
    kg = k.reshape(b, h, rows, GRID_W, hd)[:, :, row_idx]
    vg = v.reshape(b, h, rows, GRID_W, hd)[:, :, row_idx]
    bias = rpb[:, (row_off + NA_ROWS_MAX - 1)[:, None, :, None],
               jnp.clip(col_off + NA_COLS - 1, 0, 2 * NA_COLS - 2)[None, :, None, :]]
    sc = jnp.einsum('bhrqc,bhrjkc->bhrqjk', qg, kg) * scale + bias[None].astype(jnp.float32)
    sc = jnp.where(col_ok[:, None, :], sc, NEG_INF)
    p = jax.nn.softmax(sc.reshape(b, h, rows, GRID_W, kr * GRID_W), axis=-1).reshape(sc.shape)
    o = jnp.einsum('bhrqjk,bhrjkc->bhrqc', p, vg)
    return o.reshape(b, h, s, hd)


def mixer_ab(xn, w_in, w_out, t5_table, lam_re, lam_im, log_step, b_re, b_im, c_re, c_im, d_skip, w_glu):
    b, s, _ = xn.shape
    f32 = jnp.float32
    proj = xn @ w_in
    q, k, v, u = jnp.split(proj, [A_WIDTH, 2 * A_WIDTH, 3 * A_WIDTH], axis=-1)

    def heads(t):
        return t.astype(f32).reshape(b, s, A_HEADS, HEAD_DIM).transpose(0, 2, 1, 3)

    o_a = dilated_attention(heads(q), heads(k), heads(v), t5_table)
    o_a = o_a.transpose(0, 2, 1, 3).reshape(b, s, A_WIDTH)

    ug = u.astype(f32).reshape(b, s, B_GROUPS, B_GROUP)
    y = (s5_direction(ug, lam_re[0], lam_im[0], log_step[0], b_re, b_im, c_re[0], c_im[0], False)
         + s5_direction(ug, lam_re[1], lam_im[1], log_step[1], b_re, b_im, c_re[1], c_im[1], True)
         + d_skip.astype(f32).reshape(B_GROUPS, B_GROUP) * ug)
    y = jax.nn.gelu(y.reshape(b, s, B_WIDTH))
    o_b = y * jax.nn.sigmoid((y.astype(xn.dtype) @ w_glu).astype(f32))
    merged = jnp.concatenate([o_a, o_b], axis=-1).astype(xn.dtype)
    return merged @ w_out


def mixer_c(xn, w_qkv, w_out, rpb):
    b, s, _ = xn.shape
    q, k, v = jnp.split(xn @ w_qkv, 3, axis=-1)

    def heads(t):
        return t.astype(jnp.float32).reshape(b, s, C_HEADS, HEAD_DIM).transpose(0, 2, 1, 3)

    o = neighbourhood_attention(heads(q), heads(k), heads(v), rpb)
    o = o.transpose(0, 2, 1, 3).reshape(b, s, C_WIDTH).astype(xn.dtype)
    return o @ w_out


def squared_relu_mlp(xn, w1, w2):
    hdn = jnp.square(jax.nn.relu(xn @ w1))
    return hdn @ w2


def setup_inputs(seed: int = 0) -> dict:
    key = jax.random.key(seed)
    ks = jax.random.split(key, 22)
    f32 = jnp.float32
    nrm = lambda k, shape, sc: (jax.random.normal(k, shape, f32) * sc)
    lam_im_init = jnp.pi * jnp.arange(B_STATE, dtype=f32)
    return {
        "x": nrm(ks[0], (BATCH, SEQ, D_MODEL), 1.0),
        "t5_bias": nrm(ks[1], (T5_BUCKETS, A_HEADS), 0.5),
        "ab_w_in": nrm(ks[2], (N_EVEN, D_MODEL, 3 * A_WIDTH + B_WIDTH), D_MODEL ** -0.5),
        "ab_w_out": nrm(ks[3], (N_EVEN, A_WIDTH + B_WIDTH, D_MODEL), (A_WIDTH + B_WIDTH) ** -0.5),
        "s5_lam_re": -0.5 + nrm(ks[4], (N_EVEN, 2, B_GROUPS, B_STATE), 0.01),
        "s5_lam_im": lam_im_init + nrm(ks[5], (N_EVEN, 2, B_GROUPS, B_STATE), 0.01),
        "s5_log_step": jax.random.uniform(ks[6], (N_EVEN, 2, B_GROUPS), f32,
                                          minval=math.log(DT_MIN), maxval=math.log(DT_MAX)),
        "s5_b_re": nrm(ks[7], (N_EVEN, B_GROUPS, B_STATE, B_GROUP), (2 * B_GROUP) ** -0.5),
        "s5_b_im": nrm(ks[8], (N_EVEN, B_GROUPS, B_STATE, B_GROUP), (2 * B_GROUP) ** -0.5),
        "s5_c_re": nrm(ks[9], (N_EVEN, 2, B_GROUPS, B_GROUP, B_STATE), (2 * B_STATE) ** -0.5),
        "s5_c_im": nrm(ks[10], (N_EVEN, 2, B_GROUPS, B_GROUP, B_STATE), (2 * B_STATE) ** -0.5),
        "s5_d": nrm(ks[11], (N_EVEN, B_WIDTH), 1.0),
        "s5_w_glu": nrm(ks[12], (N_EVEN, B_WIDTH, B_WIDTH), B_WIDTH ** -0.5),
        "c_w_qkv": nrm(ks[13], (N_ODD, D_MODEL, 3 * C_WIDTH), D_MODEL ** -0.5),
        "c_w_out": nrm(ks[14], (N_ODD, C_WIDTH, D_MODEL), C_WIDTH ** -0.5),
        "c_rpb": nrm(ks[15], (N_ODD, C_HEADS, 2 * NA_ROWS_MAX - 1, 2 * NA_COLS - 1), 0.5),
        "norm_mix": 1.0 + nrm(ks[16], (DEPTH, D_MODEL), 0.02),
        "norm_mlp": 1.0 + nrm(ks[17], (DEPTH, D_MODEL), 0.02),
        "mlp_w1": nrm(ks[18], (DEPTH, D_MODEL, D_FF), D_MODEL ** -0.5),
        "mlp_w2": nrm(ks[19], (DEPTH, D_FF, D_MODEL), D_FF ** -0.5),
        "norm_final": 1.0 + nrm(ks[20], (D_MODEL,), 0.02),
    }


def reference(x, t5_bias, ab_w_in, ab_w_out, s5_lam_re, s5_lam_im, s5_log_step, s5_b_re, s5_b_im,
              s5_c_re, s5_c_im, s5_d, s5_w_glu, c_w_qkv, c_w_out, c_rpb, norm_mix, norm_mlp,
              mlp_w1, mlp_w2, norm_final):
    for i in range(DEPTH):
        j = i // 2
        hn = rmsnorm(x, norm_mix[i])
        if i % 2 == 0:
            mix = mixer_ab(hn, ab_w_in[j], ab_w_out[j], t5_bias, s5_lam_re[j], s5_lam_im[j],
                           s5_log_step[j], s5_b_re[j], s5_b_im[j], s5_c_re[j], s5_c_im[j],
                           s5_d[j], s5_w_glu[j])
        else:
            mix = mixer_c(hn, c_w_qkv[j], c_w_out[j], c_rpb[j])
        x = x + mix.astype(x.dtype)
        hn = rmsnorm(x, norm_mlp[i])
        x = x + squared_relu_mlp(hn, mlp_w1[i], mlp_w2[i]).astype(x.dtype)
    return rmsnorm(x, norm_final)
```

```python
import functools
import math

import numpy as np
import jax
import jax.numpy as jnp
from jax import lax
from jax.experimental import pallas as pl
from jax.experimental.pallas import tpu as pltpu

F32 = jnp.float32
BF16 = jnp.bfloat16

HEAD_DIM = 128
A_HEADS = 8
A_WIDTH = A_HEADS * HEAD_DIM
DILATED_BRANCHES = ((128, 1), (512, 4), (2048, 16))
A_QBLOCK = 128
A_HALF = 64
A_KBLOCK = A_QBLOCK + 2 * A_HALF
A_SUPER = 2048
A_HALO = A_HALF * 16
B_GROUP = 16
B_STATE = 64
S5_SLAB = 256
S5_PAIRS = 8
GRID_W = 64
NA_ROWS = 8
NA_COLS = 16
NA_QROWS = 4
NA_KROWS = 12
T5_BUCKETS = 32
T5_MAX_DISTANCE = 1024
RMS_EPS = 1e-6
NEG_INF = -1e30
VMEM_LIMIT = 56 * 1024 * 1024


def _cparams(sem):
    return pltpu.CompilerParams(dimension_semantics=sem, vmem_limit_bytes=VMEM_LIMIT)


def _rms_rows(x, g):
    y = x * lax.rsqrt(jnp.mean(x * x, axis=-1, keepdims=True) + RMS_EPS)
    return y * g


def _norm_matmul_kernel(x_ref, g_ref, w_ref, o_ref, xn_ref):
    @pl.when(pl.program_id(1) == 0)
    def _():
        xn_ref[...] = _rms_rows(x_ref[...], g_ref[...]).astype(BF16)

    o_ref[...] = jnp.dot(xn_ref[...], w_ref[...], preferred_element_type=F32).astype(o_ref.dtype)


def norm_matmul(x, g, w, out_dtype, tm=512, tn=512):
    m, k = x.shape
    n = w.shape[1]
    return pl.pallas_call(
        _norm_matmul_kernel,
        grid=(m // tm, n // tn),
        in_specs=[
            pl.BlockSpec((tm, k), lambda i, j: (i, 0)),
            pl.BlockSpec((1, k), lambda i, j: (0, 0)),
            pl.BlockSpec((k, tn), lambda i, j: (0, j)),
        ],
        out_specs=pl.BlockSpec((tm, tn), lambda i, j: (i, j)),
        out_shape=jax.ShapeDtypeStruct((m, n), out_dtype),
        scratch_shapes=[pltpu.VMEM((tm, k), BF16)],
        compiler_params=_cparams(("arbitrary", "arbitrary")),
    )(x, g.reshape(1, k), w)


def _matmul_residual_kernel(a_ref, w_ref, r_ref, o_ref):
    o_ref[...] = r_ref[...] + jnp.dot(a_ref[...], w_ref[...], preferred_element_type=F32)


def matmul_residual(a, w, res, tm=512):
    m, k = a.shape
    n = w.shape[1]
    return pl.pallas_call(
        _matmul_residual_kernel,
        grid=(m // tm,),
        in_specs=[
            pl.BlockSpec((tm, k), lambda i: (i, 0)),
            pl.BlockSpec((k, n), lambda i: (0, 0)),
            pl.BlockSpec((tm, n), lambda i: (i, 0)),
        ],
        out_specs=pl.BlockSpec((tm, n), lambda i: (i, 0)),
        out_shape=jax.ShapeDtypeStruct((m, n), F32),
        compiler_params=_cparams(("arbitrary",)),
    )(a, w, res)


def _mlp_kernel(x_ref, g_ref, w1_ref, w2_ref, o_ref, xn_ref):
    j = pl.program_id(1)

    @pl.when(j == 0)
    def _():
        xn_ref[...] = _rms_rows(x_ref[...], g_ref[...]).astype(BF16)

    h = jnp.dot(xn_ref[...], w1_ref[...], preferred_element_type=F32)
    h = jnp.square(jnp.maximum(h, 0.0)).astype(BF16)
    contrib = jnp.dot(h, w2_ref[...], preferred_element_type=F32)

    @pl.when(j == 0)
    def _():
        o_ref[...] = x_ref[...] + contrib

    @pl.when(j > 0)
    def _():
        o_ref[...] += contrib


def mlp(x, g, w1, w2, tm=512, tf=512):
    m, d = x.shape
    dff = w1.shape[1]
    return pl.pallas_call(
        _mlp_kernel,
        grid=(m // tm, dff // tf),
        in_specs=[
            pl.BlockSpec((tm, d), lambda i, j: (i, 0)),
            pl.BlockSpec((1, d), lambda i, j: (0, 0)),
            pl.BlockSpec((d, tf), lambda i, j: (0, j)),
            pl.BlockSpec((tf, d), lambda i, j: (j, 0)),
        ],
        out_specs=pl.BlockSpec((tm, d), lambda i, j: (i, 0)),
        out_shape=jax.ShapeDtypeStruct((m, d), F32),
        scratch_shapes=[pltpu.VMEM((tm, d), BF16)],
        compiler_params=_cparams(("arbitrary", "arbitrary")),
    )(x, g.reshape(1, d), w1, w2)


def _rmsnorm_kernel(x_ref, g_ref, o_ref):
    o_ref[...] = _rms_rows(x_ref[...], g_ref[...])


def rmsnorm(x, g, tm=512):
    m, d = x.shape
    return pl.pallas_call(
        _rmsnorm_kernel,
        grid=(m // tm,),
        in_specs=[pl.BlockSpec((tm, d), lambda i: (i, 0)), pl.BlockSpec((1, d), lambda i: (0, 0))],
        out_specs=pl.BlockSpec((tm, d), lambda i: (i, 0)),
        out_shape=jax.ShapeDtypeStruct((m, d), F32),
        compiler_params=_cparams(("arbitrary",)),
    )(x, g.reshape(1, d))


def _t5_bucket(rel):
    half = T5_BUCKETS // 2
    max_exact = half // 2
    n = jnp.abs(rel)
    nf = jnp.maximum(n, 1).astype(F32)
    large = max_exact + (jnp.log(nf / max_exact) / math.log(T5_MAX_DISTANCE / max_exact)
                         * (half - max_exact)).astype(jnp.int32)
    large = jnp.minimum(large, half - 1)
    return jnp.where(rel > 0, half, 0) + jnp.where(n < max_exact, n, large)


def _t5_bias_tables(t5_table):
    tabs = []
    for _, dil in DILATED_BRANCHES:
        off = jnp.arange(A_KBLOCK)[None, :] - A_HALF - jnp.arange(A_QBLOCK)[:, None]
        b = jnp.transpose(t5_table[_t5_bucket(off * dil)], (2, 0, 1)).astype(F32)
        tabs.append(jnp.where((jnp.abs(off) <= A_HALF)[None], b, NEG_INF))
    return jnp.stack(tabs)


def _dilated_kernel(q_ref, k_ref, v_ref, bias_ref, o_ref, kp_ref, vp_ref, m_ref, l_ref, n_ref, *, seq):
    t = pl.program_id(1)
    scale = 1.0 / math.sqrt(HEAD_DIM)

    @pl.when(t == 0)
    def _():
        zeros = jnp.zeros((A_HALO, HEAD_DIM), F32)
        kp_ref[pl.ds(0, A_HALO), :] = zeros
        vp_ref[pl.ds(0, A_HALO), :] = zeros
        kp_ref[pl.ds(A_HALO + seq, A_HALO), :] = zeros
        vp_ref[pl.ds(A_HALO + seq, A_HALO), :] = zeros
        kp_ref[pl.ds(A_HALO, seq), :] = k_ref[...]
        vp_ref[pl.ds(A_HALO, seq), :] = v_ref[...]

    t0 = t * A_SUPER
    kk = lax.broadcasted_iota(jnp.int32, (A_QBLOCK, A_KBLOCK), 1)
    for b, (_, dil) in enumerate(DILATED_BRANCHES):
        sub_len = seq // dil
        blocks_per_residue = A_SUPER // dil // A_QBLOCK

        def body(idx, carry, b=b, dil=dil, sub_len=sub_len, blocks_per_residue=blocks_per_residue):
            r = idx // blocks_per_residue
            n = idx % blocks_per_residue
            qs = r + n * (A_QBLOCK * dil)
            if dil == 1:
                q_idx = pl.ds(qs, A_QBLOCK)
                k_idx = pl.ds(A_HALO + t0 + qs - A_HALF, A_KBLOCK)
            else:
                q_idx = pl.ds(qs, A_QBLOCK, stride=dil)
                k_idx = pl.ds(A_HALO + t0 + qs - A_HALF * dil, A_KBLOCK, stride=dil)
            qb = (q_ref[q_idx, :] * scale).astype(BF16)
            kb = kp_ref[k_idx, :].astype(BF16)
            vb = vp_ref[k_idx, :].astype(BF16)
            s = lax.dot_general(qb, kb, (((1,), (1,)), ((), ())), preferred_element_type=F32)
            s = s + bias_ref[b, 0]
            key_l = (t0 // dil + n * A_QBLOCK - A_HALF) + kk
            s = jnp.where((key_l >= 0) & (key_l < sub_len), s, NEG_INF)
            m = jnp.max(s, axis=1, keepdims=True)
            p = jnp.exp(s - m)
            l = jnp.sum(p, axis=1, keepdims=True)
            num = jnp.dot(p.astype(BF16), vb, preferred_element_type=F32)
            m_ref[b, q_idx, :] = jnp.broadcast_to(m, (A_QBLOCK, HEAD_DIM))
            l_ref[b, q_idx, :] = jnp.broadcast_to(l, (A_QBLOCK, HEAD_DIM))
            n_ref[b, q_idx, :] = num
            return carry

        lax.fori_loop(0, A_SUPER // A_QBLOCK, body, 0)

    rows = 256
    for c in range(A_SUPER // rows):
        sl = pl.ds(c * rows, rows)
        m0, m1, m2 = m_ref[0, sl, :], m_ref[1, sl, :], m_ref[2, sl, :]
        mx = jnp.maximum(jnp.maximum(m0, m1), m2)
        w0, w1, w2 = jnp.exp(m0 - mx), jnp.exp(m1 - mx), jnp.exp(m2 - mx)
        num = w0 * n_ref[0, sl, :] + w1 * n_ref[1, sl, :] + w2 * n_ref[2, sl, :]
        den = w0 * l_ref[0, sl, :] + w1 * l_ref[1, sl, :] + w2 * l_ref[2, sl, :]
        o_ref[sl, :] = (num / den).astype(o_ref.dtype)


def dilated_attention(proj, bias_tabs, seq):
    assert seq % A_SUPER == 0
    nb = len(DILATED_BRANCHES)
    return pl.pallas_call(
        functools.partial(_dilated_kernel, seq=seq),
        grid=(A_HEADS, seq // A_SUPER),
        in_specs=[
            pl.BlockSpec((A_SUPER, HEAD_DIM), lambda h, t: (t, h)),
            pl.BlockSpec((seq, HEAD_DIM), lambda h, t: (0, A_HEADS + h)),
            pl.BlockSpec((seq, HEAD_DIM), lambda h, t: (0, 2 * A_HEADS + h)),
            pl.BlockSpec((nb, 1, A_QBLOCK, A_KBLOCK), lambda h, t: (0, h, 0, 0)),
        ],
        out_specs=pl.BlockSpec((A_SUPER, HEAD_DIM), lambda h, t: (t, h)),
        out_shape=jax.ShapeDtypeStruct((seq, A_WIDTH), BF16),
        scratch_shapes=[
            pltpu.VMEM((seq + 2 * A_HALO, HEAD_DIM), F32),
            pltpu.VMEM((seq + 2 * A_HALO, HEAD_DIM), F32),
            pltpu.VMEM((nb, A_SUPER, HEAD_DIM), F32),
            pltpu.VMEM((nb, A_SUPER, HEAD_DIM), F32),
            pltpu.VMEM((nb, A_SUPER, HEAD_DIM), F32),
        ],
        compiler_params=_cparams(("arbitrary", "arbitrary")),
    )(proj, proj, proj, bias_tabs)


def _s5_discretise(lam_re, lam_im, log_step, b_re, b_im, c_re, c_im):
    g, p = lam_re.shape[1:]
    slabs = g // (2 * S5_PAIRS)
    step = jnp.exp(log_step.astype(F32))[..., None]
    lr = jnp.minimum(lam_re.astype(F32), -1e-4)
    li = lam_im.astype(F32)
    mag = jnp.exp(lr * step)
    ab_re = mag * jnp.cos(li * step)
    ab_im = mag * jnp.sin(li * step)
    den = lr * lr + li * li
    zr = ((ab_re - 1.0) * lr + ab_im * li) / den
    zi = (ab_im * lr - (ab_re - 1.0) * li) / den
    br = b_re.astype(F32)[None]
    bi = b_im.astype(F32)[None]
    bb_re = zr[..., None] * br - zi[..., None] * bi
    bb_im = zr[..., None] * bi + zi[..., None] * br

    a_re = ab_re.reshape(2, slabs, S5_PAIRS, 2 * p)
    a_im = ab_im.reshape(2, slabs, S5_PAIRS, 2 * p)

    eye2 = jnp.eye(2, dtype=F32)
    eyej = jnp.eye(S5_PAIRS, dtype=F32)

    def pack_b(bb):
        bb = bb.reshape(2, slabs, S5_PAIRS, 2, p, B_GROUP)
        out = jnp.einsum('dsjepc,ef,jk->dskjecfp', bb, eye2, eyej)
        return out.reshape(2, slabs, S5_PAIRS, S5_SLAB, 2 * p)

    b_mat = jnp.concatenate([pack_b(bb_re), pack_b(bb_im)], axis=-1)

    def pack_c(cc):
        cc = cc.astype(F32).reshape(2, slabs, S5_PAIRS, 2, B_GROUP, p)
        out = jnp.einsum('dsjecp,ef,jk->dskfpjec', cc, eye2, eyej)
        return out.reshape(2, slabs, S5_PAIRS, 2 * p, S5_SLAB)

    c_mat = jnp.concatenate([pack_c(c_re), -pack_c(c_im)], axis=-2)
    return a_re, a_im, b_mat.astype(BF16), c_mat.astype(BF16)


def _s5_kernel(u_ref, b_ref, c_ref, are_ref, aim_ref, dsk_ref, y_ref,
               xr_ref, xi_ref, bur_ref, bui_ref, *, tm, nchunks):
    d = pl.program_id(1)
    c = pl.program_id(2)
    nstate = 2 * B_STATE

    @pl.when(c == 0)
    def _():
        xr_ref[...] = jnp.zeros_like(xr_ref)
        xi_ref[...] = jnp.zeros_like(xi_ref)

    chunk = c + d * (nchunks - 1 - 2 * c)
    row0 = pl.multiple_of(chunk * tm, tm)
    u = u_ref[...]
    ub = u.astype(BF16)
    for j in range(S5_PAIRS):
        bu = jnp.dot(ub, b_ref[0, 0, j], preferred_element_type=F32)
        bur_ref[pl.ds(j, tm, stride=S5_PAIRS), :] = bu[:, :nstate]
        bui_ref[pl.ds(j, tm, stride=S5_PAIRS), :] = bu[:, nstate:]

    ar = are_ref[0, 0]
    ai = aim_ref[0, 0]

    def step(i, carry):
        xr, xi = carry
        t = i + d * (tm - 1 - 2 * i)
        rows = pl.ds(pl.multiple_of(t * S5_PAIRS, S5_PAIRS), S5_PAIRS)
        nxr = ar * xr - ai * xi + bur_ref[rows, :]
        nxi = ar * xi + ai * xr + bui_ref[rows, :]
        bur_ref[rows, :] = nxr
        bui_ref[rows, :] = nxi
        return nxr, nxi

    xr, xi = lax.fori_loop(0, tm, step, (xr_ref[...], xi_ref[...]), unroll=8)
    xr_ref[...] = xr
    xi_ref[...] = xi

    acc = jnp.zeros((tm, S5_SLAB), F32)
    for j in range(S5_PAIRS):
        xj = jnp.concatenate([bur_ref[pl.ds(j, tm, stride=S5_PAIRS), :],
                              bui_ref[pl.ds(j, tm, stride=S5_PAIRS), :]], axis=1).astype(BF16)
        acc = acc + jnp.dot(xj, c_ref[0, 0, j], preferred_element_type=F32)

    @pl.when(d == 0)
    def _():
        y_ref[pl.ds(row0, tm), :] = acc + dsk_ref[...] * u

    @pl.when(d == 1)
    def _():
        y_ref[pl.ds(row0, tm), :] += acc


def s5_scan(proj, u_col0, a_re, a_im, b_mat, c_mat, d_skip, seq, tm=512):
    slabs = a_re.shape[1]
    width = slabs * S5_SLAB
    nchunks = seq // tm
    ucol = u_col0 // S5_SLAB

    def chunk_of(d, c):
        return c + d * (nchunks - 1 - 2 * c)

    return pl.pallas_call(
        functools.partial(_s5_kernel, tm=tm, nchunks=nchunks),
        grid=(slabs, 2, nchunks),
        in_specs=[
            pl.BlockSpec((tm, S5_SLAB), lambda s, d, c: (chunk_of(d, c), ucol + s)),
            pl.BlockSpec((1, 1, S5_PAIRS, S5_SLAB, S5_SLAB), lambda s, d, c: (d, s, 0, 0, 0)),
            pl.BlockSpec((1, 1, S5_PAIRS, S5_SLAB, S5_SLAB), lambda s, d, c: (d, s, 0, 0, 0)),
            pl.BlockSpec((1, 1, S5_PAIRS, 2 * B_STATE), lambda s, d, c: (d, s, 0, 0)),
            pl.BlockSpec((1, 1, S5_PAIRS, 2 * B_STATE), lambda s, d, c: (d, s, 0, 0)),
            pl.BlockSpec((1, S5_SLAB), lambda s, d, c: (0, s)),
        ],
        out_specs=pl.BlockSpec((seq, S5_SLAB), lambda s, d, c: (0, s)),
        out_shape=jax.ShapeDtypeStruct((seq, width), F32),
        scratch_shapes=[
            pltpu.VMEM((S5_PAIRS, 2 * B_STATE), F32),
            pltpu.VMEM((S5_PAIRS, 2 * B_STATE), F32),
            pltpu.VMEM((tm * S5_PAIRS, 2 * B_STATE), F32),
            pltpu.VMEM((tm * S5_PAIRS, 2 * B_STATE), F32),
        ],
        compiler_params=_cparams(("arbitrary", "arbitrary", "arbitrary")),
    )(proj, b_mat, c_mat, a_re, a_im, d_skip.reshape(1, width))


def _glu_kernel(y_ref, w_ref, o_ref):
    y = jax.nn.gelu(y_ref[...])
    z = jnp.dot(y.astype(BF16), w_ref[...], preferred_element_type=F32)
    o_ref[...] = (y * jax.nn.sigmoid(z)).astype(o_ref.dtype)


def glu(y, w, tm=512):
    m, k = y.shape
    return pl.pallas_call(
        _glu_kernel,
        grid=(m // tm,),
        in_specs=[pl.BlockSpec((tm, k), lambda i: (i, 0)), pl.BlockSpec((k, k), lambda i: (0, 0))],
        out_specs=pl.BlockSpec((tm, k), lambda i: (i, 0)),
        out_shape=jax.ShapeDtypeStruct((m, k), BF16),
        compiler_params=_cparams(("arbitrary",)),
    )(y, w)


def _na_bias_tables(rpb):
    heads = rpb.shape[0]
    c = np.arange(GRID_W)
    col_idx = np.clip(c[None, :] - c[:, None] + NA_COLS - 1, 0, 2 * NA_COLS - 2)
    r2 = rpb.astype(F32)[:, :, col_idx]
    r2 = jnp.pad(r2, ((0, 0), (4, 4), (0, 0), (0, 0)))
    ext = NA_KROWS + 8
    parts = [r2[:, 3 - qr:3 - qr + ext] for qr in range(NA_QROWS)]
    t = jnp.stack(parts, axis=1)
    t = t.transpose(0, 1, 3, 2, 4).reshape(heads, NA_QROWS * GRID_W, ext * GRID_W // 256, 256)
    return t.transpose(0, 2, 1, 3)


def _na_mask_tables():
    qr = np.arange(NA_QROWS)[:, None, None, None]
    cq = np.arange(GRID_W)[None, :, None, None]
    kr = np.arange(NA_KROWS)[None, None, :, None]
    ck = np.arange(GRID_W)[None, None, None, :]
    cs = np.clip(cq - NA_COLS // 2, 0, GRID_W - NA_COLS)
    col_ok = (ck >= cs) & (ck < cs + NA_COLS)
    row_ok = [(kr < NA_ROWS) & (qr >= 0),
              (kr >= qr) & (kr < qr + NA_ROWS),
              (kr >= NA_KROWS - NA_ROWS) & (qr >= 0)]
    out = [np.where(r & col_ok, 0.0, NEG_INF).reshape(NA_QROWS * GRID_W, NA_KROWS * GRID_W) for r in row_ok]
    return np.stack(out).astype(np.float32)


def _natten_kernel(q_ref, k_ref, v_ref, bias_ref, mask_ref, o_ref, *, groups, rows):
    g = pl.program_id(1)
    scale = 1.0 / math.sqrt(HEAD_DIM)
    nk = NA_KROWS * GRID_W
    key_row0 = jnp.clip(NA_QROWS * g - NA_ROWS // 2, 0, rows - NA_KROWS)
    tok0 = pl.multiple_of(key_row0 * GRID_W, GRID_W)
    variant = jnp.where(g == 0, 0, jnp.where(g == groups - 1, 2, 1))
    bias_blk0 = 2 - variant
    kb = k_ref[pl.ds(tok0, nk), :]
    vb = v_ref[pl.ds(tok0, nk), :]
    s = lax.dot_general(q_ref[...], kb, (((1,), (1,)), ((), ())), preferred_element_type=F32) * scale
    bias = jnp.concatenate([bias_ref[0, bias_blk0 + i] for i in range(nk // 256)], axis=1)
    s = s + bias + mask_ref[variant]
    m = jnp.max(s, axis=1, keepdims=True)
    p = jnp.exp(s - m)
    l = jnp.sum(p, axis=1, keepdims=True)
    o = jnp.dot(p.astype(BF16), vb, preferred_element_type=F32)
    o_ref[...] = (o / l).astype(o_ref.dtype)


def neighbourhood_attention(qkv, bias_tabs, mask_tabs, seq, heads):
    rows = seq // GRID_W
    assert rows % NA_QROWS == 0 and rows >= NA_KROWS + NA_QROWS
    groups = rows // NA_QROWS
    tq = NA_QROWS * GRID_W
    return pl.pallas_call(
        functools.partial(_natten_kernel, groups=groups, rows=rows),
        grid=(heads, groups),
        in_specs=[
            pl.BlockSpec((tq, HEAD_DIM), lambda h, g: (g, h)),
            pl.BlockSpec((seq, HEAD_DIM), lambda h, g: (0, heads + h)),
            pl.BlockSpec((seq, HEAD_DIM), lambda h, g: (0, 2 * heads + h)),
            pl.BlockSpec((1,) + bias_tabs.shape[1:], lambda h, g: (h, 0, 0, 0)),
            pl.BlockSpec(mask_tabs.shape, lambda h, g: (0, 0, 0)),
        ],
        out_specs=pl.BlockSpec((tq, HEAD_DIM), lambda h, g: (g, h)),
        out_shape=jax.ShapeDtypeStruct((seq, heads * HEAD_DIM), BF16),
        compiler_params=_cparams(("arbitrary", "arbitrary")),
    )(qkv, qkv, qkv, bias_tabs, mask_tabs)


def kernel(x, t5_bias, ab_w_in, ab_w_out, s5_lam_re, s5_lam_im, s5_log_step, s5_b_re, s5_b_im, s5_c_re, s5_c_im, s5_d, s5_w_glu, c_w_qkv, c_w_out, c_rpb, norm_mix, norm_mlp, mlp_w1, mlp_w2, norm_final):
    batch, seq, d_model = x.shape
    depth = norm_mix.shape[0]
    c_heads = c_rpb.shape[1]
    t5_tabs = _t5_bias_tables(t5_bias)
    na_mask = jnp.asarray(_na_mask_tables())
    outs = []
    for bi in range(batch):
        h = x[bi]
        for i in range(depth):
            j = i // 2
            if i % 2 == 0:
                proj = norm_matmul(h, norm_mix[i], ab_w_in[j].astype(BF16), F32)
                o_a = dilated_attention(proj, t5_tabs, seq)
                a_re, a_im, b_mat, c_mat = _s5_discretise(
                    s5_lam_re[j], s5_lam_im[j], s5_log_step[j], s5_b_re[j], s5_b_im[j], s5_c_re[j], s5_c_im[j])
                y = s5_scan(proj, 3 * A_WIDTH, a_re, a_im, b_mat, c_mat, s5_d[j], seq)
                o_b = glu(y, s5_w_glu[j].astype(BF16))
                merged = jnp.concatenate([o_a, o_b], axis=-1)
                h = matmul_residual(merged, ab_w_out[j].astype(BF16), h)
            else:
                qkv = norm_matmul(h, norm_mix[i], c_w_qkv[j].astype(BF16), BF16)
                o = neighbourhood_attention(qkv, _na_bias_tables(c_rpb[j]), na_mask, seq, c_heads)
                h = matmul_residual(o, c_w_out[j].astype(BF16), h)
            h = mlp(h, norm_mlp[i], mlp_w1[i].astype(BF16), mlp_w2[i].astype(BF16))
        outs.append(rmsnorm(h, norm_final))
    return jnp.stack(outs)
```

```python
import functools
import math

import numpy as np
import jax
import jax.numpy as jnp
from jax import lax
from jax.experimental import pallas as pl
from jax.experimental.pallas import tpu as pltpu

F32 = jnp.float32
BF16 = jnp.bfloat16

HEAD_DIM = 128
A_HEADS = 8
A_WIDTH = A_HEADS * HEAD_DIM
DILATED_BRANCHES = ((128, 1), (512, 4), (2048, 16))
A_QBLOCK = 128
A_HALF = 64
A_KBLOCK = A_QBLOCK + 2 * A_HALF
A_SUPER = 2048
B_GROUP = 16
B_STATE = 64
S5_SLAB = 256
S5_PAIRS = 8
GRID_W = 64
NA_ROWS = 8
NA_COLS = 16
NA_QROWS = 4
NA_KROWS = 12
NA_EXT = NA_KROWS + 8
NA_HEADS_PER_STEP = 2
T5_BUCKETS = 32
T5_MAX_DISTANCE = 1024
RMS_EPS = 1e-6
NEG_INF = -1e30
VMEM_LIMIT = 56 * 1024 * 1024


def _cparams(sem):
    return pltpu.CompilerParams(dimension_semantics=sem, vmem_limit_bytes=VMEM_LIMIT)


def _rms_rows(x, g):
    y = x * lax.rsqrt(jnp.mean(x * x, axis=-1, keepdims=True) + RMS_EPS)
    return y * g


def _norm_matmul_kernel(x_ref, g_ref, w_ref, o_ref, xn_ref, *, scaled_blocks, scale):
    j = pl.program_id(1)

    @pl.when(j == 0)
    def _():
        xn_ref[...] = _rms_rows(x_ref[...], g_ref[...]).astype(BF16)

    acc = jnp.dot(xn_ref[...], w_ref[...], preferred_element_type=F32)
    if scaled_blocks:
        acc = acc * jnp.where(j < scaled_blocks, scale, 1.0)
    o_ref[...] = acc.astype(o_ref.dtype)


def norm_matmul(x, g, w, out_dtype, scaled_cols=0, scale=1.0, tm=512, tn=512):
    m, k = x.shape
    n = w.shape[1]
    assert scaled_cols % tn == 0
    return pl.pallas_call(
        functools.partial(_norm_matmul_kernel, scaled_blocks=scaled_cols // tn, scale=scale),
        grid=(m // tm, n // tn),
        in_specs=[
            pl.BlockSpec((tm, k), lambda i, j: (i, 0)),
            pl.BlockSpec((1, k), lambda i, j: (0, 0)),
            pl.BlockSpec((k, tn), lambda i, j: (0, j)),
        ],
        out_specs=pl.BlockSpec((tm, tn), lambda i, j: (i, j)),
        out_shape=jax.ShapeDtypeStruct((m, n), out_dtype),
        scratch_shapes=[pltpu.VMEM((tm, k), BF16)],
        compiler_params=_cparams(("arbitrary", "arbitrary")),
    )(x, g.reshape(1, k), w)


def _matmul_residual_kernel(*refs, widths):
    a_refs = refs[:len(widths)]
    w_ref, r_ref, o_ref = refs[len(widths):]
    acc = r_ref[...]
    row = 0
    for a_ref, kw in zip(a_refs, widths):
        acc = acc + jnp.dot(a_ref[...], w_ref[pl.ds(row, kw), :], preferred_element_type=F32)
        row += kw
    o_ref[...] = acc


def matmul_residual(a_list, w, res, tm=512):
    m = res.shape[0]
    k, n = w.shape
    widths = tuple(a.shape[1] for a in a_list)
    assert sum(widths) == k
    return pl.pallas_call(
        functools.partial(_matmul_residual_kernel, widths=widths),
        grid=(m // tm,),
        in_specs=[pl.BlockSpec((tm, kw), lambda i: (i, 0)) for kw in widths] + [
            pl.BlockSpec((k, n), lambda i: (0, 0)),
            pl.BlockSpec((tm, n), lambda i: (i, 0)),
        ],
        out_specs=pl.BlockSpec((tm, n), lambda i: (i, 0)),
        out_shape=jax.ShapeDtypeStruct((m, n), F32),
        compiler_params=_cparams(("arbitrary",)),
    )(*a_list, w, res)


def _mlp_kernel(x_ref, g_ref, w1_ref, w2_ref, o_ref, xn_ref):
    j = pl.program_id(1)

    @pl.when(j == 0)
    def _():
        xn_ref[...] = _rms_rows(x_ref[...], g_ref[...]).astype(BF16)

    h = jnp.dot(xn_ref[...], w1_ref[...], preferred_element_type=F32)
    h = jnp.square(jnp.maximum(h, 0.0)).astype(BF16)
    contrib = jnp.dot(h, w2_ref[...], preferred_element_type=F32)

    @pl.when(j == 0)
    def _():
        o_ref[...] = x_ref[...] + contrib

    @pl.when(j > 0)
    def _():
        o_ref[...] += contrib


def mlp(x, g, w1, w2, tm=512, tf=512):
    m, d = x.shape
    dff = w1.shape[1]
    return pl.pallas_call(
        _mlp_kernel,
        grid=(m // tm, dff // tf),
        in_specs=[
            pl.BlockSpec((tm, d), lambda i, j: (i, 0)),
            pl.BlockSpec((1, d), lambda i, j: (0, 0)),
            pl.BlockSpec((d, tf), lambda i, j: (0, j)),
            pl.BlockSpec((tf, d), lambda i, j: (j, 0)),
        ],
        out_specs=pl.BlockSpec((tm, d), lambda i, j: (i, 0)),
        out_shape=jax.ShapeDtypeStruct((m, d), F32),
        scratch_shapes=[pltpu.VMEM((tm, d), BF16)],
        compiler_params=_cparams(("arbitrary", "arbitrary")),
    )(x, g.reshape(1, d), w1, w2)


def _rmsnorm_kernel(x_ref, g_ref, o_ref):
    o_ref[...] = _rms_rows(x_ref[...], g_ref[...])


def rmsnorm(x, g, tm=512):
    m, d = x.shape
    return pl.pallas_call(
        _rmsnorm_kernel,
        grid=(m // tm,),
        in_specs=[pl.BlockSpec((tm, d), lambda i: (i, 0)), pl.BlockSpec((1, d), lambda i: (0, 0))],
        out_specs=pl.BlockSpec((tm, d), lambda i: (i, 0)),
        out_shape=jax.ShapeDtypeStruct((m, d), F32),
        compiler_params=_cparams(("arbitrary",)),
    )(x, g.reshape(1, d))


def _t5_bucket(rel):
    half = T5_BUCKETS // 2
    max_exact = half // 2
    n = jnp.abs(rel)
    nf = jnp.maximum(n, 1).astype(F32)
    large = max_exact + (jnp.log(nf / max_exact) / math.log(T5_MAX_DISTANCE / max_exact)
                         * (half - max_exact)).astype(jnp.int32)
    large = jnp.minimum(large, half - 1)
    return jnp.where(rel > 0, half, 0) + jnp.where(n < max_exact, n, large)


def _t5_bucket_tables():
    tabs = []
    for _, dil in DILATED_BRANCHES:
        off = jnp.arange(A_KBLOCK)[None, :] - A_HALF - jnp.arange(A_QBLOCK)[:, None]
        tabs.append(jnp.where(jnp.abs(off) <= A_HALF, _t5_bucket(off * dil), T5_BUCKETS))
    return jnp.stack(tabs).astype(jnp.int32)


def _dilated_sections(seq):
    bases, sizes, row = [], [], 0
    for _, dil in DILATED_BRANCHES:
        sec = seq // dil + 2 * A_HALF
        bases.append(row)
        sizes.append(sec)
        row += dil * sec
    return bases, sizes, row


def _dilated_kernel(t5_ref, q_ref, k_ref, v_ref, bucket_ref, o_ref,
                    kd_ref, vd_ref, bias_ref, m_ref, l_ref, n_ref, *, seq):
    h = pl.program_id(0)
    t = pl.program_id(1)
    scale = 1.0 / math.sqrt(HEAD_DIM)
    bases, sizes, _ = _dilated_sections(seq)

    @pl.when(t == 0)
    def _():
        for b in range(len(DILATED_BRANCHES)):
            bk = bucket_ref[b]
            acc = jnp.full((A_QBLOCK, A_KBLOCK), NEG_INF, F32)
            for kbkt in range(T5_BUCKETS):
                acc = jnp.where(bk == kbkt, t5_ref[kbkt * A_HEADS + h], acc)
            bias_ref[b] = acc
        zeros = jnp.zeros((A_HALF, HEAD_DIM), BF16)
        for src, dst in ((k_ref, kd_ref), (v_ref, vd_ref)):
            for b, (_, dil) in enumerate(DILATED_BRANCHES):
                sub_len = seq // dil
                for r in range(dil):
                    o = bases[b] + r * sizes[b]
                    dst[pl.ds(o, A_HALF), :] = zeros
                    dst[pl.ds(o + A_HALF + sub_len, A_HALF), :] = zeros
                    rows = src[...] if dil == 1 else src[pl.ds(r, sub_len, stride=dil), :]
                    dst[pl.ds(o + A_HALF, sub_len), :] = rows.astype(BF16)

    kk = lax.broadcasted_iota(jnp.int32, (1, A_KBLOCK), 1)
    for b, (_, dil) in enumerate(DILATED_BRANCHES):
        sub_len = seq // dil
        blocks_per_residue = A_SUPER // dil // A_QBLOCK

        def body(idx, carry, b=b, dil=dil, sub_len=sub_len, blocks_per_residue=blocks_per_residue):
            r = idx // blocks_per_residue
            n = idx % blocks_per_residue
            qs = r + n * (A_QBLOCK * dil)
            q_idx = pl.ds(qs, A_QBLOCK) if dil == 1 else pl.ds(qs, A_QBLOCK, stride=dil)
            blk = t * blocks_per_residue + n
            k_idx = pl.ds(pl.multiple_of(bases[b] + r * sizes[b] + blk * A_QBLOCK, A_QBLOCK), A_KBLOCK)
            qb = (q_ref[q_idx, :] * scale).astype(BF16)
            s = lax.dot_general(qb, kd_ref[k_idx, :], (((1,), (1,)), ((), ())), preferred_element_type=F32)
            key_l = blk * A_QBLOCK - A_HALF + kk
            edge = jnp.where((key_l >= 0) & (key_l < sub_len), 0.0, NEG_INF)
            s = s + bias_ref[b] + edge
            m = jnp.max(s, axis=1, keepdims=True)
            p = jnp.exp(s - m)
            l = jnp.sum(p, axis=1, keepdims=True)
            num = jnp.dot(p.astype(BF16), vd_ref[k_idx, :], preferred_element_type=F32)
            m_ref[b, q_idx, :] = jnp.broadcast_to(m, (A_QBLOCK, HEAD_DIM))
            l_ref[b, q_idx, :] = jnp.broadcast_to(l, (A_QBLOCK, HEAD_DIM))
            n_ref[b, q_idx, :] = num
            return carry

        lax.fori_loop(0, A_SUPER // A_QBLOCK, body, 0, unroll=4)

    rows = 256
    for c in range(A_SUPER // rows):
        sl = pl.ds(c * rows, rows)
        m0, m1, m2 = m_ref[0, sl, :], m_ref[1, sl, :], m_ref[2, sl, :]
        mx = jnp.maximum(jnp.maximum(m0, m1), m2)
        w0, w1, w2 = jnp.exp(m0 - mx), jnp.exp(m1 - mx), jnp.exp(m2 - mx)
        num = w0 * n_ref[0, sl, :] + w1 * n_ref[1, sl, :] + w2 * n_ref[2, sl, :]
        den = w0 * l_ref[0, sl, :] + w1 * l_ref[1, sl, :] + w2 * l_ref[2, sl, :]
        o_ref[sl, :] = (num / den).astype(o_ref.dtype)


def dilated_attention(proj, t5_table, bucket_tabs, seq):
    assert seq % A_SUPER == 0
    nb = len(DILATED_BRANCHES)
    total_rows = _dilated_sections(seq)[2]
    return pl.pallas_call(
        functools.partial(_dilated_kernel, seq=seq),
        grid=(A_HEADS, seq // A_SUPER),
        in_specs=[
            pl.BlockSpec(memory_space=pltpu.SMEM),
            pl.BlockSpec((A_SUPER, HEAD_DIM), lambda h, t: (t, h)),
            pl.BlockSpec((seq, HEAD_DIM), lambda h, t: (0, A_HEADS + h)),
            pl.BlockSpec((seq, HEAD_DIM), lambda h, t: (0, 2 * A_HEADS + h)),
            pl.BlockSpec((nb, A_QBLOCK, A_KBLOCK), lambda h, t: (0, 0, 0)),
        ],
        out_specs=pl.BlockSpec((A_SUPER, HEAD_DIM), lambda h, t: (t, h)),
        out_shape=jax.ShapeDtypeStruct((seq, A_WIDTH), BF16),
        scratch_shapes=[
            pltpu.VMEM((total_rows, HEAD_DIM), BF16),
            pltpu.VMEM((total_rows, HEAD_DIM), BF16),
            pltpu.VMEM((nb, A_QBLOCK, A_KBLOCK), F32),
            pltpu.VMEM((nb, A_SUPER, HEAD_DIM), F32),
            pltpu.VMEM((nb, A_SUPER, HEAD_DIM), F32),
            pltpu.VMEM((nb, A_SUPER, HEAD_DIM), F32),
        ],
        compiler_params=_cparams(("arbitrary", "arbitrary")),
    )(t5_table.astype(F32).reshape(-1), proj, proj, proj, bucket_tabs)


def _s5_discretise(lam_re, lam_im, log_step, b_re, b_im, c_re, c_im):
    g, p = lam_re.shape[1:]
    slabs = g // (2 * S5_PAIRS)
    step = jnp.exp(log_step.astype(F32))[..., None]
    lr = jnp.minimum(lam_re.astype(F32), -1e-4)
    li = lam_im.astype(F32)
    mag = jnp.exp(lr * step)
    ab_re = mag * jnp.cos(li * step)
    ab_im = mag * jnp.sin(li * step)
    den = lr * lr + li * li
    zr = ((ab_re - 1.0) * lr + ab_im * li) / den
    zi = (ab_im * lr - (ab_re - 1.0) * li) / den
    br = b_re.astype(F32)[None]
    bi = b_im.astype(F32)[None]
    bb_re = zr[..., None] * br - zi[..., None] * bi
    bb_im = zr[..., None] * bi + zi[..., None] * br

    a_re = ab_re.reshape(2, slabs, S5_PAIRS, 2 * p)
    a_im = ab_im.reshape(2, slabs, S5_PAIRS, 2 * p)

    eye2 = jnp.eye(2, dtype=F32)
    eyej = jnp.eye(S5_PAIRS, dtype=F32)

    def pack_b(bb):
        bb = bb.reshape(2, slabs, S5_PAIRS, 2, p, B_GROUP)
        out = jnp.einsum('dsjepc,ef,jk->dskjecfp', bb, eye2, eyej)
        return out.reshape(2, slabs, S5_PAIRS, S5_SLAB, 2 * p)

    b_mat = jnp.concatenate([pack_b(bb_re), pack_b(bb_im)], axis=-1)

    def pack_c(cc):
        cc = cc.astype(F32).reshape(2, slabs, S5_PAIRS, 2, B_GROUP, p)
        out = jnp.einsum('dsjecp,ef,jk->dskfpjec', cc, eye2, eyej)
        return out.reshape(2, slabs, S5_PAIRS, 2 * p, S5_SLAB)

    c_mat = jnp.concatenate([pack_c(c_re), -pack_c(c_im)], axis=-2)
    return a_re, a_im, b_mat.astype(BF16), c_mat.astype(BF16)


def _s5_kernel(u_ref, b_ref, c_ref, are_ref, aim_ref, y_ref, xr_ref, xi_ref, bur_ref, bui_ref, *, tm, slabs):
    d = pl.program_id(0)
    c = pl.program_id(1)
    nstate = 2 * B_STATE

    @pl.when(c == 0)
    def _():
        xr_ref[...] = jnp.zeros_like(xr_ref)
        xi_ref[...] = jnp.zeros_like(xi_ref)

    for s in range(slabs):
        ub = u_ref[:, s * S5_SLAB:(s + 1) * S5_SLAB].astype(BF16)
        for j in range(S5_PAIRS):
            bu = jnp.dot(ub, b_ref[0, s, j], preferred_element_type=F32)
            bur_ref[s, pl.ds(j, tm, stride=S5_PAIRS), :] = bu[:, :nstate]
            bui_ref[s, pl.ds(j, tm, stride=S5_PAIRS), :] = bu[:, nstate:]

    ar = [are_ref[0, s] for s in range(slabs)]
    ai = [aim_ref[0, s] for s in range(slabs)]

    def step(i, carry):
        t = i + d * (tm - 1 - 2 * i)
        rows = pl.ds(pl.multiple_of(t * S5_PAIRS, S5_PAIRS), S5_PAIRS)
        new = []
        for s in range(slabs):
            xr, xi = carry[2 * s], carry[2 * s + 1]
            nxr = ar[s] * xr - ai[s] * xi + bur_ref[s, rows, :]
            nxi = ar[s] * xi + ai[s] * xr + bui_ref[s, rows, :]
            bur_ref[s, rows, :] = nxr
            bui_ref[s, rows, :] = nxi
            new += [nxr, nxi]
        return tuple(new)

    init = []
    for s in range(slabs):
        init += [xr_ref[s], xi_ref[s]]
    final = lax.fori_loop(0, tm, step, tuple(init), unroll=4)
    for s in range(slabs):
        xr_ref[s] = final[2 * s]
        xi_ref[s] = final[2 * s + 1]

    for s in range(slabs):
        acc = jnp.zeros((tm, S5_SLAB), F32)
        for j in range(S5_PAIRS):
            xj = jnp.concatenate([bur_ref[s, pl.ds(j, tm, stride=S5_PAIRS), :],
                                  bui_ref[s, pl.ds(j, tm, stride=S5_PAIRS), :]], axis=1).astype(BF16)
            acc = acc + jnp.dot(xj, c_ref[0, s, j], preferred_element_type=F32)
        y_ref[0, :, s * S5_SLAB:(s + 1) * S5_SLAB] = acc


def s5_scan(proj, u_col0, a_re, a_im, b_mat, c_mat, seq, tm=512):
    slabs = a_re.shape[1]
    width = slabs * S5_SLAB
    nchunks = seq // tm
    assert u_col0 % width == 0

    def chunk_of(d, c):
        return c + d * (nchunks - 1 - 2 * c)

    wspec = pl.BlockSpec((1, slabs, S5_PAIRS, S5_SLAB, S5_SLAB), lambda d, c: (d, 0, 0, 0, 0))
    aspec = pl.BlockSpec((1, slabs, S5_PAIRS, 2 * B_STATE), lambda d, c: (d, 0, 0, 0))
    return pl.pallas_call(
        functools.partial(_s5_kernel, tm=tm, slabs=slabs),
        grid=(2, nchunks),
        in_specs=[pl.BlockSpec((tm, width), lambda d, c: (chunk_of(d, c), u_col0 // width)),
                  wspec, wspec, aspec, aspec],
        out_specs=pl.BlockSpec((1, tm, width), lambda d, c: (d, chunk_of(d, c), 0)),
        out_shape=jax.ShapeDtypeStruct((2, seq, width), F32),
        scratch_shapes=[
            pltpu.VMEM((slabs, S5_PAIRS, 2 * B_STATE), F32),
            pltpu.VMEM((slabs, S5_PAIRS, 2 * B_STATE), F32),
            pltpu.VMEM((slabs, tm * S5_PAIRS, 2 * B_STATE), F32),
            pltpu.VMEM((slabs, tm * S5_PAIRS, 2 * B_STATE), F32),
        ],
        compiler_params=_cparams(("arbitrary", "arbitrary")),
    )(proj, b_mat, c_mat, a_re, a_im)


def _glu_kernel(yf_ref, yb_ref, u_ref, dsk_ref, w_ref, o_ref):
    y = jax.nn.gelu(yf_ref[0] + yb_ref[0] + dsk_ref[...] * u_ref[...])
    z = jnp.dot(y.astype(BF16), w_ref[...], preferred_element_type=F32)
    o_ref[...] = (y * jax.nn.sigmoid(z)).astype(o_ref.dtype)


def glu(y2, proj, u_col0, d_skip, w, tm=512):
    _, m, k = y2.shape
    return pl.pallas_call(
        _glu_kernel,
        grid=(m // tm,),
        in_specs=[
            pl.BlockSpec((1, tm, k), lambda i: (0, i, 0)),
            pl.BlockSpec((1, tm, k), lambda i: (1, i, 0)),
            pl.BlockSpec((tm, k), lambda i: (i, u_col0 // k)),
            pl.BlockSpec((1, k), lambda i: (0, 0)),
            pl.BlockSpec((k, k), lambda i: (0, 0)),
        ],
        out_specs=pl.BlockSpec((tm, k), lambda i: (i, 0)),
        out_shape=jax.ShapeDtypeStruct((m, k), BF16),
        compiler_params=_cparams(("arbitrary",)),
    )(y2, y2, proj, d_skip.reshape(1, k), w)


def _na_bias_pairs(rpb):
    c = np.arange(GRID_W)
    col_idx = np.clip(c[None, :] - c[:, None] + NA_COLS - 1, 0, 2 * NA_COLS - 2)
    onehot = (col_idx[None] == np.arange(2 * NA_COLS - 1)[:, None, None]).astype(np.float32)
    r2 = jnp.einsum('hdk,kqc->hdqc', rpb.astype(F32), jnp.asarray(onehot), precision=lax.Precision.HIGHEST)
    r2 = jnp.pad(r2, ((0, 0), (4, 4), (0, 0), (0, 0)))
    return jnp.concatenate([r2[:, :-1], r2[:, 1:]], axis=-1)


def _na_mask_tables():
    qr = np.arange(NA_QROWS)[:, None, None, None]
    cq = np.arange(GRID_W)[None, :, None, None]
    kr = np.arange(NA_KROWS)[None, None, :, None]
    ck = np.arange(GRID_W)[None, None, None, :]
    cs = np.clip(cq - NA_COLS // 2, 0, GRID_W - NA_COLS)
    col_ok = (ck >= cs) & (ck < cs + NA_COLS)
    row_ok = [(kr < NA_ROWS) & (qr >= 0),
              (kr >= qr) & (kr < qr + NA_ROWS),
              (kr >= NA_KROWS - NA_ROWS) & (qr >= 0)]
    out = [np.where(r & col_ok, 0.0, NEG_INF).reshape(NA_QROWS * GRID_W, NA_KROWS * GRID_W) for r in row_ok]
    return np.stack(out).astype(np.float32)


def _natten_kernel(q_ref, k_ref, v_ref, pair_ref, mask_ref, o_ref, bias_ref, *, groups, rows):
    g = pl.program_id(1)
    nk = NA_KROWS * GRID_W

    @pl.when(g == 0)
    def _():
        for hh in range(NA_HEADS_PER_STEP):
            for qr in range(NA_QROWS):
                for e2 in range(NA_EXT // 2):
                    bias_ref[hh, e2 // 2, pl.ds(qr * GRID_W, GRID_W), pl.ds((e2 % 2) * 128, 128)] = (
                        pair_ref[hh, 2 * e2 - qr + 3])

    key_row0 = jnp.clip(NA_QROWS * g - NA_ROWS // 2, 0, rows - NA_KROWS)
    tok0 = pl.multiple_of(key_row0 * GRID_W, GRID_W)
    variant = jnp.where(g == 0, 0, jnp.where(g == groups - 1, 2, 1))
    bias_blk0 = 2 - variant
    mask = mask_ref[variant]
    for hh in range(NA_HEADS_PER_STEP):
        cols = pl.ds(hh * HEAD_DIM, HEAD_DIM)
        s = lax.dot_general(q_ref[:, cols], k_ref[pl.ds(tok0, nk), cols], (((1,), (1,)), ((), ())),
                            preferred_element_type=F32)
        bias = jnp.concatenate([bias_ref[hh, bias_blk0 + i] for i in range(nk // 256)], axis=1)
        s = s + bias + mask
        m = jnp.max(s, axis=1, keepdims=True)
        p = jnp.exp(s - m)
        l = jnp.sum(p, axis=1, keepdims=True)
        o = jnp.dot(p.astype(BF16), v_ref[pl.ds(tok0, nk), cols], preferred_element_type=F32)
        o_ref[:, cols] = (o / l).astype(o_ref.dtype)


def neighbourhood_attention(qkv, bias_pairs, mask_tabs, seq, heads):
    rows = seq // GRID_W
    assert rows % NA_QROWS == 0 and rows >= NA_KROWS + NA_QROWS and heads % NA_HEADS_PER_STEP == 0
    groups = rows // NA_QROWS
    tq = NA_QROWS * GRID_W
    hw = NA_HEADS_PER_STEP * HEAD_DIM
    hsteps = heads // NA_HEADS_PER_STEP
    return pl.pallas_call(
        functools.partial(_natten_kernel, groups=groups, rows=rows),
        grid=(hsteps, groups),
        in_specs=[
            pl.BlockSpec((tq, hw), lambda h, g: (g, h)),
            pl.BlockSpec((seq, hw), lambda h, g: (0, hsteps + h)),
            pl.BlockSpec((seq, hw), lambda h, g: (0, 2 * hsteps + h)),
            pl.BlockSpec((NA_HEADS_PER_STEP,) + bias_pairs.shape[1:], lambda h, g: (h, 0, 0, 0)),
            pl.BlockSpec(mask_tabs.shape, lambda h, g: (0, 0, 0)),
        ],
        out_specs=pl.BlockSpec((tq, hw), lambda h, g: (g, h)),
        out_shape=jax.ShapeDtypeStruct((seq, heads * HEAD_DIM), BF16),
        scratch_shapes=[pltpu.VMEM((NA_HEADS_PER_STEP, NA_EXT * GRID_W // 256, tq, 256), F32)],
        compiler_params=_cparams(("arbitrary", "arbitrary")),
    )(qkv, qkv, qkv, bias_pairs, mask_tabs)


def kernel(x, t5_bias, ab_w_in, ab_w_out, s5_lam_re, s5_lam_im, s5_log_step, s5_b_re, s5_b_im, s5_c_re, s5_c_im, s5_d, s5_w_glu, c_w_qkv, c_w_out, c_rpb, norm_mix, norm_mlp, mlp_w1, mlp_w2, norm_final):
    batch, seq, d_model = x.shape
    depth = norm_mix.shape[0]
    c_heads = c_rpb.shape[1]
    t5_buckets = _t5_bucket_tables()
    na_mask = jnp.asarray(_na_mask_tables())
    attn_scale = 1.0 / math.sqrt(HEAD_DIM)
    outs = []
    for bi in range(batch):
        h = x[bi]
        for i in range(depth):
            j = i // 2
            if i % 2 == 0:
                proj = norm_matmul(h, norm_mix[i], ab_w_in[j].astype(BF16), F32)
                o_a = dilated_attention(proj, t5_bias, t5_buckets, seq)
                a_re, a_im, b_mat, c_mat = _s5_discretise(
                    s5_lam_re[j], s5_lam_im[j], s5_log_step[j], s5_b_re[j], s5_b_im[j], s5_c_re[j], s5_c_im[j])
                y2 = s5_scan(proj, 3 * A_WIDTH, a_re, a_im, b_mat, c_mat, seq)
                o_b = glu(y2, proj, 3 * A_WIDTH, s5_d[j], s5_w_glu[j].astype(BF16))
                h = matmul_residual([o_a, o_b], ab_w_out[j].astype(BF16), h)
            else:
                qkv = norm_matmul(h, norm_mix[i], c_w_qkv[j].astype(BF16), BF16,
                                  scaled_cols=c_heads * HEAD_DIM, scale=attn_scale)
                o = neighbourhood_attention(qkv, _na_bias_pairs(c_rpb[j]), na_mask, seq, c_heads)
                h = matmul_residual([o], c_w_out[j].astype(BF16), h)
            h = mlp(h, norm_mlp[i], mlp_w1[i].astype(BF16), mlp_w2[i].astype(BF16))
        outs.append(rmsnorm(h, norm_final))
    return jnp.stack(outs)
```

```python
import functools
import math

import numpy as np
import jax
import jax.numpy as jnp
from jax import lax
from jax.experimental import pallas as pl
from jax.experimental.pallas import tpu as pltpu

F32 = jnp.float32
BF16 = jnp.bfloat16

HEAD_DIM = 128
A_HEADS = 8
A_WIDTH = A_HEADS * HEAD_DIM
DILATED_BRANCHES = ((128, 1), (512, 4), (2048, 16))
A_QBLOCK = 128
A_HALF = 64
A_KBLOCK = A_QBLOCK + 2 * A_HALF
A_SUPER = 2048
B_GROUP = 16
B_STATE = 64
S5_SLAB = 256
S5_PAIRS = 8
S5_PITCH = 9
GRID_W = 64
NA_ROWS = 8
NA_COLS = 16
NA_QROWS = 4
NA_KROWS = 12
NA_EXT = NA_KROWS + 8
NA_HEADS_PER_STEP = 2
T5_BUCKETS = 32
T5_MAX_DISTANCE = 1024
RMS_EPS = 1e-6
NEG_INF = -1e30
VMEM_LIMIT = 56 * 1024 * 1024


def _cparams(sem):
    return pltpu.CompilerParams(dimension_semantics=sem, vmem_limit_bytes=VMEM_LIMIT)


def _rms_rows(x, g):
    y = x * lax.rsqrt(jnp.mean(x * x, axis=-1, keepdims=True) + RMS_EPS)
    return y * g


def _norm_matmul_kernel(x_ref, g_ref, w_ref, o_ref, xn_ref, *, scaled_blocks, scale):
    j = pl.program_id(1)

    @pl.when(j == 0)
    def _():
        xn_ref[...] = _rms_rows(x_ref[...], g_ref[...]).astype(BF16)

    acc = jnp.dot(xn_ref[...], w_ref[...], preferred_element_type=F32)
    if scaled_blocks:
        acc = acc * jnp.where(j < scaled_blocks, scale, 1.0)
    o_ref[...] = acc.astype(o_ref.dtype)


def norm_matmul(x, g, w, layer, out_dtype, scaled_cols=0, scale=1.0, tm=512, tn=512):
    m, k = x.shape
    n = w.shape[2]
    assert scaled_cols % tn == 0
    return pl.pallas_call(
        functools.partial(_norm_matmul_kernel, scaled_blocks=scaled_cols // tn, scale=scale),
        grid=(m // tm, n // tn),
        in_specs=[
            pl.BlockSpec((tm, k), lambda i, j: (i, 0)),
            pl.BlockSpec((1, k), lambda i, j: (0, 0)),
            pl.BlockSpec((None, k, tn), lambda i, j: (layer, 0, j)),
        ],
        out_specs=pl.BlockSpec((tm, tn), lambda i, j: (i, j)),
        out_shape=jax.ShapeDtypeStruct((m, n), out_dtype),
        scratch_shapes=[pltpu.VMEM((tm, k), BF16)],
        compiler_params=_cparams(("arbitrary", "arbitrary")),
    )(x, g.reshape(1, k), w)


def _matmul_residual_kernel(*refs, widths):
    a_refs = refs[:len(widths)]
    w_ref, r_ref, o_ref = refs[len(widths):]
    acc = r_ref[...]
    row = 0
    for a_ref, kw in zip(a_refs, widths):
        acc = acc + jnp.dot(a_ref[...], w_ref[pl.ds(row, kw), :], preferred_element_type=F32)
        row += kw
    o_ref[...] = acc


def matmul_residual(a_list, w, layer, res, tm=512):
    m = res.shape[0]
    _, k, n = w.shape
    widths = tuple(a.shape[1] for a in a_list)
    assert sum(widths) == k
    return pl.pallas_call(
        functools.partial(_matmul_residual_kernel, widths=widths),
        grid=(m // tm,),
        in_specs=[pl.BlockSpec((tm, kw), lambda i: (i, 0)) for kw in widths] + [
            pl.BlockSpec((None, k, n), lambda i: (layer, 0, 0)),
            pl.BlockSpec((tm, n), lambda i: (i, 0)),
        ],
        out_specs=pl.BlockSpec((tm, n), lambda i: (i, 0)),
        out_shape=jax.ShapeDtypeStruct((m, n), F32),
        compiler_params=_cparams(("arbitrary",)),
    )(*a_list, w, res)


def _mlp_kernel(x_ref, xres_ref, g_ref, w1_ref, w2_ref, o_ref, xn_ref, h_ref, *, up_steps, tf):
    s = pl.program_id(1)

    @pl.when(s == 0)
    def _():
        xn_ref[...] = _rms_rows(x_ref[...], g_ref[...]).astype(BF16)

    @pl.when(s < up_steps)
    def _():
        h = jnp.dot(xn_ref[...], w1_ref[...], preferred_element_type=F32)
        h_ref[s] = jnp.square(jnp.maximum(h, 0.0)).astype(BF16)

    @pl.when(s >= up_steps)
    def _():
        acc = xres_ref[...]
        for c in range(up_steps):
            acc = acc + jnp.dot(h_ref[c], w2_ref[pl.ds(c * tf, tf), :], preferred_element_type=F32)
        o_ref[...] = acc


def mlp(x, g, w1, w2, layer, tm=512, tf=1024, tn=512):
    m, d = x.shape
    dff = w1.shape[2]
    up_steps = dff // tf
    down_steps = d // tn
    return pl.pallas_call(
        functools.partial(_mlp_kernel, up_steps=up_steps, tf=tf),
        grid=(m // tm, up_steps + down_steps),
        in_specs=[
            pl.BlockSpec((tm, d), lambda i, s: (i, 0)),
            pl.BlockSpec((tm, tn), lambda i, s: (i, jnp.maximum(s - up_steps, 0))),
            pl.BlockSpec((1, d), lambda i, s: (0, 0)),
            pl.BlockSpec((None, d, tf), lambda i, s: (layer, 0, jnp.minimum(s, up_steps - 1))),
            pl.BlockSpec((None, dff, tn), lambda i, s: (layer, 0, jnp.maximum(s - up_steps, 0))),
        ],
        out_specs=pl.BlockSpec((tm, tn), lambda i, s: (i, jnp.maximum(s - up_steps, 0))),
        out_shape=jax.ShapeDtypeStruct((m, d), F32),
        scratch_shapes=[pltpu.VMEM((tm, d), BF16), pltpu.VMEM((up_steps, tm, tf), BF16)],
        compiler_params=_cparams(("arbitrary", "arbitrary")),
    )(x, x, g.reshape(1, d), w1, w2)


def _rmsnorm_kernel(x_ref, g_ref, o_ref):
    o_ref[...] = _rms_rows(x_ref[...], g_ref[...])


def rmsnorm(x, g, tm=512):
    m, d = x.shape
    return pl.pallas_call(
        _rmsnorm_kernel,
        grid=(m // tm,),
        in_specs=[pl.BlockSpec((tm, d), lambda i: (i, 0)), pl.BlockSpec((1, d), lambda i: (0, 0))],
        out_specs=pl.BlockSpec((tm, d), lambda i: (i, 0)),
        out_shape=jax.ShapeDtypeStruct((m, d), F32),
        compiler_params=_cparams(("arbitrary",)),
    )(x, g.reshape(1, d))


def _t5_bucket(rel):
    half = T5_BUCKETS // 2
    max_exact = half // 2
    n = jnp.abs(rel)
    nf = jnp.maximum(n, 1).astype(F32)
    large = max_exact + (jnp.log(nf / max_exact) / math.log(T5_MAX_DISTANCE / max_exact)
                         * (half - max_exact)).astype(jnp.int32)
    large = jnp.minimum(large, half - 1)
    return jnp.where(rel > 0, half, 0) + jnp.where(n < max_exact, n, large)


def _t5_bucket_tables():
    tabs = []
    for _, dil in DILATED_BRANCHES:
        off = jnp.arange(A_KBLOCK)[None, :] - A_HALF - jnp.arange(A_QBLOCK)[:, None]
        tabs.append(jnp.where(jnp.abs(off) <= A_HALF, _t5_bucket(off * dil), T5_BUCKETS))
    return jnp.stack(tabs).astype(jnp.int32)


def _dilated_sections(seq):
    bases, sizes, row = [], [], 0
    for _, dil in DILATED_BRANCHES:
        sec = seq // dil + 2 * A_HALF
        bases.append(row)
        sizes.append(sec)
        row += dil * sec
    return bases, sizes, row


def _dilated_kernel(t5_ref, q_ref, k_ref, v_ref, bucket_ref, o_ref,
                    kd_ref, vd_ref, bias_ref, m_ref, l_ref, n_ref, *, seq):
    h = pl.program_id(0)
    t = pl.program_id(1)
    scale = 1.0 / math.sqrt(HEAD_DIM)
    bases, sizes, _ = _dilated_sections(seq)

    @pl.when(t == 0)
    def _():
        for b in range(len(DILATED_BRANCHES)):
            bk = bucket_ref[b]
            acc = jnp.full((A_QBLOCK, A_KBLOCK), NEG_INF, F32)
            for kbkt in range(T5_BUCKETS):
                acc = jnp.where(bk == kbkt, t5_ref[kbkt * A_HEADS + h], acc)
            bias_ref[b] = acc
        zeros = jnp.zeros((A_HALF, HEAD_DIM), BF16)
        for src, dst in ((k_ref, kd_ref), (v_ref, vd_ref)):
            for b, (_, dil) in enumerate(DILATED_BRANCHES):
                sub_len = seq // dil
                for r in range(dil):
                    o = bases[b] + r * sizes[b]
                    dst[pl.ds(o, A_HALF), :] = zeros
                    dst[pl.ds(o + A_HALF + sub_len, A_HALF), :] = zeros
                    rows = src[...] if dil == 1 else src[pl.ds(r, sub_len, stride=dil), :]
                    dst[pl.ds(o + A_HALF, sub_len), :] = rows.astype(BF16)

    kk = lax.broadcasted_iota(jnp.int32, (1, A_KBLOCK), 1)
    for b, (_, dil) in enumerate(DILATED_BRANCHES):
        sub_len = seq // dil
        blocks_per_residue = A_SUPER // dil // A_QBLOCK

        def body(idx, carry, b=b, dil=dil, sub_len=sub_len, blocks_per_residue=blocks_per_residue):
            r = idx // blocks_per_residue
            n = idx % blocks_per_residue
            qs = r + n * (A_QBLOCK * dil)
            q_idx = pl.ds(qs, A_QBLOCK) if dil == 1 else pl.ds(qs, A_QBLOCK, stride=dil)
            blk = t * blocks_per_residue + n
            k_idx = pl.ds(pl.multiple_of(bases[b] + r * sizes[b] + blk * A_QBLOCK, A_QBLOCK), A_KBLOCK)
            qb = (q_ref[q_idx, :] * scale).astype(BF16)
            s = lax.dot_general(qb, kd_ref[k_idx, :], (((1,), (1,)), ((), ())), preferred_element_type=F32)
            key_l = blk * A_QBLOCK - A_HALF + kk
            edge = jnp.where((key_l >= 0) & (key_l < sub_len), 0.0, NEG_INF)
            s = s + bias_ref[b] + edge
            m = jnp.max(s, axis=1, keepdims=True)
            p = jnp.exp(s - m)
            l = jnp.sum(p, axis=1, keepdims=True)
            num = jnp.dot(p.astype(BF16), vd_ref[k_idx, :], preferred_element_type=F32)
            m_ref[b, q_idx, :] = jnp.broadcast_to(m, (A_QBLOCK, HEAD_DIM))
            l_ref[b, q_idx, :] = jnp.broadcast_to(l, (A_QBLOCK, HEAD_DIM))
            n_ref[b, q_idx, :] = num
            return carry

        lax.fori_loop(0, A_SUPER // A_QBLOCK, body, 0, unroll=4)

    rows = 256
    for c in range(A_SUPER // rows):
        sl = pl.ds(c * rows, rows)
        m0, m1, m2 = m_ref[0, sl, :], m_ref[1, sl, :], m_ref[2, sl, :]
        mx = jnp.maximum(jnp.maximum(m0, m1), m2)
        w0, w1, w2 = jnp.exp(m0 - mx), jnp.exp(m1 - mx), jnp.exp(m2 - mx)
        num = w0 * n_ref[0, sl, :] + w1 * n_ref[1, sl, :] + w2 * n_ref[2, sl, :]
        den = w0 * l_ref[0, sl, :] + w1 * l_ref[1, sl, :] + w2 * l_ref[2, sl, :]
        o_ref[sl, :] = (num / den).astype(o_ref.dtype)


def dilated_attention(proj, t5_table, bucket_tabs, seq):
    assert seq % A_SUPER == 0
    nb = len(DILATED_BRANCHES)
    total_rows = _dilated_sections(seq)[2]
    return pl.pallas_call(
        functools.partial(_dilated_kernel, seq=seq),
        grid=(A_HEADS, seq // A_SUPER),
        in_specs=[
            pl.BlockSpec(memory_space=pltpu.SMEM),
            pl.BlockSpec((A_SUPER, HEAD_DIM), lambda h, t: (t, h)),
            pl.BlockSpec((seq, HEAD_DIM), lambda h, t: (0, A_HEADS + h)),
            pl.BlockSpec((seq, HEAD_DIM), lambda h, t: (0, 2 * A_HEADS + h)),
            pl.BlockSpec((nb, A_QBLOCK, A_KBLOCK), lambda h, t: (0, 0, 0)),
        ],
        out_specs=pl.BlockSpec((A_SUPER, HEAD_DIM), lambda h, t: (t, h)),
        out_shape=jax.ShapeDtypeStruct((seq, A_WIDTH), BF16),
        scratch_shapes=[
            pltpu.VMEM((total_rows, HEAD_DIM), BF16),
            pltpu.VMEM((total_rows, HEAD_DIM), BF16),
            pltpu.VMEM((nb, A_QBLOCK, A_KBLOCK), F32),
            pltpu.VMEM((nb, A_SUPER, HEAD_DIM), F32),
            pltpu.VMEM((nb, A_SUPER, HEAD_DIM), F32),
            pltpu.VMEM((nb, A_SUPER, HEAD_DIM), F32),
        ],
        compiler_params=_cparams(("arbitrary", "arbitrary")),
    )(t5_table.astype(F32).reshape(-1), proj, proj, proj, bucket_tabs)


def _s5_discretise(lam_re, lam_im, log_step, b_re, b_im, c_re, c_im):
    g, p = lam_re.shape[1:]
    slabs = g // (2 * S5_PAIRS)
    step = jnp.exp(log_step.astype(F32))[..., None]
    lr = jnp.minimum(lam_re.astype(F32), -1e-4)
    li = lam_im.astype(F32)
    mag = jnp.exp(lr * step)
    ab_re = mag * jnp.cos(li * step)
    ab_im = mag * jnp.sin(li * step)
    den = lr * lr + li * li
    zr = ((ab_re - 1.0) * lr + ab_im * li) / den
    zi = (ab_im * lr - (ab_re - 1.0) * li) / den
    br = b_re.astype(F32)[None]
    bi = b_im.astype(F32)[None]
    bb_re = zr[..., None] * br - zi[..., None] * bi
    bb_im = zr[..., None] * bi + zi[..., None] * br

    a_re = ab_re.reshape(2, slabs, S5_PAIRS, 2 * p)
    a_im = ab_im.reshape(2, slabs, S5_PAIRS, 2 * p)

    eye2 = jnp.eye(2, dtype=F32)

    def pack_b(bb):
        bb = bb.reshape(2, slabs, S5_PAIRS, 2, p, B_GROUP)
        return jnp.einsum('dsjepc,ef->dsjecfp', bb, eye2).reshape(2, slabs, S5_SLAB, 2 * p)

    def pack_c(cc):
        cc = cc.astype(F32).reshape(2, slabs, S5_PAIRS, 2, B_GROUP, p)
        return jnp.einsum('dsjecp,ef->dsfpjec', cc, eye2).reshape(2, slabs, 2 * p, S5_SLAB)

    b_mat = jnp.concatenate([pack_b(bb_re), pack_b(bb_im)], axis=-1)
    c_mat = jnp.concatenate([pack_c(c_re), -pack_c(c_im)], axis=-2)
    return a_re, a_im, b_mat.astype(BF16), c_mat.astype(BF16)


def _s5_kernel(uf_ref, ub_ref, b_ref, c_ref, are_ref, aim_ref, yf_ref, yb_ref,
               xr_ref, xi_ref, bur_ref, bui_ref, *, tm, slabs):
    c = pl.program_id(0)
    nstate = 2 * B_STATE
    u_refs = (uf_ref, ub_ref)
    y_refs = (yf_ref, yb_ref)

    @pl.when(c == 0)
    def _():
        xr_ref[...] = jnp.zeros_like(xr_ref)
        xi_ref[...] = jnp.zeros_like(xi_ref)

    lane_pair = lax.broadcasted_iota(jnp.int32, (1, S5_SLAB), 1) // (2 * B_GROUP)

    def pair_rows(j):
        return pl.ds(j, tm, stride=S5_PITCH)

    for d in range(2):
        for s in range(slabs):
            u16 = u_refs[d][:, s * S5_SLAB:(s + 1) * S5_SLAB].astype(BF16)
            lhs = jnp.concatenate([jnp.where(lane_pair == j, u16, jnp.zeros_like(u16))
                                   for j in range(S5_PAIRS)], axis=0)
            bu = jnp.dot(lhs, b_ref[d, s], preferred_element_type=F32)
            for j in range(S5_PAIRS):
                bur_ref[d, s, pair_rows(j), :] = bu[j * tm:(j + 1) * tm, :nstate]
                bui_ref[d, s, pair_rows(j), :] = bu[j * tm:(j + 1) * tm, nstate:]

    ar = [[are_ref[d, s] for s in range(slabs)] for d in range(2)]
    ai = [[aim_ref[d, s] for s in range(slabs)] for d in range(2)]

    def step(i, carry):
        new = []
        for d in range(2):
            t = i if d == 0 else tm - 1 - i
            start = t * S5_PITCH
            rows = pl.ds(pl.multiple_of(start, 8) if S5_PITCH % 8 == 0 else start, S5_PAIRS)
            for s in range(slabs):
                k = 2 * (d * slabs + s)
                xr, xi = carry[k], carry[k + 1]
                nxr = ar[d][s] * xr - ai[d][s] * xi + bur_ref[d, s, rows, :]
                nxi = ar[d][s] * xi + ai[d][s] * xr + bui_ref[d, s, rows, :]
                bur_ref[d, s, rows, :] = nxr
                bui_ref[d, s, rows, :] = nxi
                new += [nxr, nxi]
        return tuple(new)

    init = []
    for d in range(2):
        for s in range(slabs):
            init += [xr_ref[d, s], xi_ref[d, s]]
    final = lax.fori_loop(0, tm, step, tuple(init), unroll=4)
    for d in range(2):
        for s in range(slabs):
            k = 2 * (d * slabs + s)
            xr_ref[d, s] = final[k]
            xi_ref[d, s] = final[k + 1]

    for d in range(2):
        for s in range(slabs):
            xcat = jnp.concatenate(
                [jnp.concatenate([bur_ref[d, s, pair_rows(j), :], bui_ref[d, s, pair_rows(j), :]],
                                 axis=1).astype(BF16) for j in range(S5_PAIRS)], axis=0)
            r = jnp.dot(xcat, c_ref[d, s], preferred_element_type=F32)
            y = r[:tm]
            for j in range(1, S5_PAIRS):
                y = jnp.where(lane_pair == j, r[j * tm:(j + 1) * tm], y)
            y_refs[d][:, s * S5_SLAB:(s + 1) * S5_SLAB] = y


def s5_scan(proj, u_col0, a_re, a_im, b_mat, c_mat, seq, tm=256):
    slabs = a_re.shape[1]
    width = slabs * S5_SLAB
    nchunks = seq // tm
    assert u_col0 % width == 0
    ucol = u_col0 // width
    wspec = pl.BlockSpec((2, slabs, S5_SLAB, S5_SLAB), lambda c: (0, 0, 0, 0))
    aspec = pl.BlockSpec((2, slabs, S5_PAIRS, 2 * B_STATE), lambda c: (0, 0, 0, 0))
    state = pltpu.VMEM((2, slabs, S5_PAIRS, 2 * B_STATE), F32)
    drive = pltpu.VMEM((2, slabs, tm * S5_PITCH, 2 * B_STATE), F32)
    return pl.pallas_call(
        functools.partial(_s5_kernel, tm=tm, slabs=slabs),
        grid=(nchunks,),
        in_specs=[pl.BlockSpec((tm, width), lambda c: (c, ucol)),
                  pl.BlockSpec((tm, width), lambda c: (nchunks - 1 - c, ucol)),
                  wspec, wspec, aspec, aspec],
        out_specs=[pl.BlockSpec((tm, width), lambda c: (c, 0)),
                   pl.BlockSpec((tm, width), lambda c: (nchunks - 1 - c, 0))],
        out_shape=[jax.ShapeDtypeStruct((seq, width), F32), jax.ShapeDtypeStruct((seq, width), F32)],
        scratch_shapes=[state, state, drive, drive],
        compiler_params=_cparams(("arbitrary",)),
    )(proj, proj, b_mat, c_mat, a_re, a_im)


def _glu_kernel(yf_ref, yb_ref, u_ref, dsk_ref, w_ref, o_ref):
    y = jax.nn.gelu(yf_ref[...] + yb_ref[...] + dsk_ref[...] * u_ref[...])
    z = jnp.dot(y.astype(BF16), w_ref[...], preferred_element_type=F32)
    o_ref[...] = (y * jax.nn.sigmoid(z)).astype(o_ref.dtype)


def glu(yf, yb, proj, u_col0, d_skip, w, layer, tm=512):
    m, k = yf.shape
    return pl.pallas_call(
        _glu_kernel,
        grid=(m // tm,),
        in_specs=[
            pl.BlockSpec((tm, k), lambda i: (i, 0)),
            pl.BlockSpec((tm, k), lambda i: (i, 0)),
            pl.BlockSpec((tm, k), lambda i: (i, u_col0 // k)),
            pl.BlockSpec((1, k), lambda i: (0, 0)),
            pl.BlockSpec((None, k, k), lambda i: (layer, 0, 0)),
        ],
        out_specs=pl.BlockSpec((tm, k), lambda i: (i, 0)),
        out_shape=jax.ShapeDtypeStruct((m, k), BF16),
        compiler_params=_cparams(("arbitrary",)),
    )(yf, yb, proj, d_skip.reshape(1, k), w)


def _na_bias_pairs(rpb):
    c = np.arange(GRID_W)
    col_idx = np.clip(c[None, :] - c[:, None] + NA_COLS - 1, 0, 2 * NA_COLS - 2)
    onehot = (col_idx[None] == np.arange(2 * NA_COLS - 1)[:, None, None]).astype(np.float32)
    r2 = jnp.einsum('hdk,kqc->hdqc', rpb.astype(F32), jnp.asarray(onehot), precision=lax.Precision.HIGHEST)
    r2 = jnp.pad(r2, ((0, 0), (4, 4), (0, 0), (0, 0)))
    return jnp.concatenate([r2[:, :-1], r2[:, 1:]], axis=-1)


def _na_mask_tables():
    qr = np.arange(NA_QROWS)[:, None, None, None]
    cq = np.arange(GRID_W)[None, :, None, None]
    kr = np.arange(NA_KROWS)[None, None, :, None]
    ck = np.arange(GRID_W)[None, None, None, :]
    cs = np.clip(cq - NA_COLS // 2, 0, GRID_W - NA_COLS)
    col_ok = (ck >= cs) & (ck < cs + NA_COLS)
    row_ok = [(kr < NA_ROWS) & (qr >= 0),
              (kr >= qr) & (kr < qr + NA_ROWS),
              (kr >= NA_KROWS - NA_ROWS) & (qr >= 0)]
    out = [np.where(r & col_ok, 0.0, NEG_INF).reshape(NA_QROWS * GRID_W, NA_KROWS * GRID_W) for r in row_ok]
    return np.stack(out).astype(np.float32)


def _natten_kernel(q_ref, k_ref, v_ref, pair_ref, mask_ref, o_ref, bias_ref, *, groups, rows):
    g = pl.program_id(1)
    nk = NA_KROWS * GRID_W

    @pl.when(g == 0)
    def _():
        for hh in range(NA_HEADS_PER_STEP):
            for qr in range(NA_QROWS):
                for e2 in range(NA_EXT // 2):
                    bias_ref[hh, e2 // 2, pl.ds(qr * GRID_W, GRID_W), pl.ds((e2 % 2) * 128, 128)] = (
                        pair_ref[hh, 2 * e2 - qr + 3])

    key_row0 = jnp.clip(NA_QROWS * g - NA_ROWS // 2, 0, rows - NA_KROWS)
    tok0 = pl.multiple_of(key_row0 * GRID_W, GRID_W)
    variant = jnp.where(g == 0, 0, jnp.where(g == groups - 1, 2, 1))
    bias_blk0 = 2 - variant
    mask = mask_ref[variant]
    for hh in range(NA_HEADS_PER_STEP):
        cols = pl.ds(hh * HEAD_DIM, HEAD_DIM)
        s = lax.dot_general(q_ref[:, cols], k_ref[pl.ds(tok0, nk), cols], (((1,), (1,)), ((), ())),
                            preferred_element_type=F32)
        bias = jnp.concatenate([bias_ref[hh, bias_blk0 + i] for i in range(nk // 256)], axis=1)
        s = s + bias + mask
        m = jnp.max(s, axis=1, keepdims=True)
        p = jnp.exp(s - m)
        l = jnp.sum(p, axis=1, keepdims=True)
        o = jnp.dot(p.astype(BF16), v_ref[pl.ds(tok0, nk), cols], preferred_element_type=F32)
        o_ref[:, cols] = (o / l).astype(o_ref.dtype)


def neighbourhood_attention(qkv, bias_pairs, mask_tabs, seq, heads):
    rows = seq // GRID_W
    assert rows % NA_QROWS == 0 and rows >= NA_KROWS + NA_QROWS and heads % NA_HEADS_PER_STEP == 0
    groups = rows // NA_QROWS
    tq = NA_QROWS * GRID_W
    hw = NA_HEADS_PER_STEP * HEAD_DIM
    hsteps = heads // NA_HEADS_PER_STEP
    return pl.pallas_call(
        functools.partial(_natten_kernel, groups=groups, rows=rows),
        grid=(hsteps, groups),
        in_specs=[
            pl.BlockSpec((tq, hw), lambda h, g: (g, h)),
            pl.BlockSpec((seq, hw), lambda h, g: (0, hsteps + h)),
            pl.BlockSpec((seq, hw), lambda h, g: (0, 2 * hsteps + h)),
            pl.BlockSpec((NA_HEADS_PER_STEP,) + bias_pairs.shape[1:], lambda h, g: (h, 0, 0, 0)),
            pl.BlockSpec(mask_tabs.shape, lambda h, g: (0, 0, 0)),
        ],
        out_specs=pl.BlockSpec((tq, hw), lambda h, g: (g, h)),
        out_shape=jax.ShapeDtypeStruct((seq, heads * HEAD_DIM), BF16),
        scratch_shapes=[pltpu.VMEM((NA_HEADS_PER_STEP, NA_EXT * GRID_W // 256, tq, 256), F32)],
        compiler_params=_cparams(("arbitrary", "arbitrary")),
    )(qkv, qkv, qkv, bias_pairs, mask_tabs)


def kernel(x, t5_bias, ab_w_in, ab_w_out, s5_lam_re, s5_lam_im, s5_log_step, s5_b_re, s5_b_im, s5_c_re, s5_c_im, s5_d, s5_w_glu, c_w_qkv, c_w_out, c_rpb, norm_mix, norm_mlp, mlp_w1, mlp_w2, norm_final):
    batch, seq, d_model = x.shape
    depth = norm_mix.shape[0]
    c_heads = c_rpb.shape[1]
    t5_buckets = _t5_bucket_tables()
    na_mask = jnp.asarray(_na_mask_tables())
    attn_scale = 1.0 / math.sqrt(HEAD_DIM)
    w_in, w_out, w_glu = ab_w_in.astype(BF16), ab_w_out.astype(BF16), s5_w_glu.astype(BF16)
    w_qkv, w_cout = c_w_qkv.astype(BF16), c_w_out.astype(BF16)
    w1, w2 = mlp_w1.astype(BF16), mlp_w2.astype(BF16)
    outs = []
    for bi in range(batch):
        h = x[bi]
        for i in range(depth):
            j = i // 2
            if i % 2 == 0:
                proj = norm_matmul(h, norm_mix[i], w_in, j, F32)
                o_a = dilated_attention(proj, t5_bias, t5_buckets, seq)
                a_re, a_im, b_mat, c_mat = _s5_discretise(
                    s5_lam_re[j], s5_lam_im[j], s5_log_step[j], s5_b_re[j], s5_b_im[j], s5_c_re[j], s5_c_im[j])
                yf, yb = s5_scan(proj, 3 * A_WIDTH, a_re, a_im, b_mat, c_mat, seq)
                o_b = glu(yf, yb, proj, 3 * A_WIDTH, s5_d[j], w_glu, j)
                h = matmul_residual([o_a, o_b], w_out, j, h)
            else:
                qkv = norm_matmul(h, norm_mix[i], w_qkv, j, BF16, scaled_cols=c_heads * HEAD_DIM, scale=attn_scale)
                o = neighbourhood_attention(qkv, _na_bias_pairs(c_rpb[j]), na_mask, seq, c_heads)
                h = matmul_residual([o], w_cout, j, h)
            h = mlp(h, norm_mlp[i], w1, w2, i)
        outs.append(rmsnorm(h, norm_final))
    return jnp.stack(outs)
```

```python
import functools
import math

import numpy as np
import jax
import jax.numpy as jnp
from jax import lax
from jax.experimental import pallas as pl
from jax.experimental.pallas import tpu as pltpu

F32 = jnp.float32
BF16 = jnp.bfloat16

HEAD_DIM = 128
A_HEADS = 8
A_WIDTH = A_HEADS * HEAD_DIM
DILATED_BRANCHES = ((128, 1), (512, 4), (2048, 16))
A_QBLOCK = 128
A_HALF = 64
A_KBLOCK = A_QBLOCK + 2 * A_HALF
A_SUPER = 2048
B_GROUP = 16
B_STATE = 64
S5_SLAB = 256
S5_PAIRS = 8
S5_PITCH = 9
GRID_W = 64
NA_ROWS = 8
NA_COLS = 16
NA_QROWS = 4
NA_KROWS = 12
NA_EXT = NA_KROWS + 8
NA_HEADS_PER_STEP = 2
NA_GROUPS_PER_STEP = 2
T5_BUCKETS = 32
T5_MAX_DISTANCE = 1024
RMS_EPS = 1e-6
NEG_INF = -1e30
VMEM_LIMIT = 56 * 1024 * 1024


def _cparams(sem):
    return pltpu.CompilerParams(dimension_semantics=sem, vmem_limit_bytes=VMEM_LIMIT)


def _rms_rows(x, g):
    y = x * lax.rsqrt(jnp.mean(x * x, axis=-1, keepdims=True) + RMS_EPS)
    return y * g


def _norm_matmul_kernel(x_ref, g_ref, w_ref, o_ref, xn_ref, *, scaled_blocks, scale):
    j = pl.program_id(1)

    @pl.when(j == 0)
    def _():
        xn_ref[...] = _rms_rows(x_ref[...], g_ref[...]).astype(BF16)

    acc = jnp.dot(xn_ref[...], w_ref[...], preferred_element_type=F32)
    if scaled_blocks:
        acc = acc * jnp.where(j < scaled_blocks, scale, 1.0)
    o_ref[...] = acc.astype(o_ref.dtype)


def norm_matmul(x, g, w, layer, out_dtype, scaled_cols=0, scale=1.0, tm=1024, tn=512):
    m, k = x.shape
    n = w.shape[2]
    assert scaled_cols % tn == 0
    return pl.pallas_call(
        functools.partial(_norm_matmul_kernel, scaled_blocks=scaled_cols // tn, scale=scale),
        grid=(m // tm, n // tn),
        in_specs=[
            pl.BlockSpec((tm, k), lambda i, j: (i, 0)),
            pl.BlockSpec((1, k), lambda i, j: (0, 0)),
            pl.BlockSpec((None, k, tn), lambda i, j: (layer, 0, j)),
        ],
        out_specs=pl.BlockSpec((tm, tn), lambda i, j: (i, j)),
        out_shape=jax.ShapeDtypeStruct((m, n), out_dtype),
        scratch_shapes=[pltpu.VMEM((tm, k), BF16)],
        compiler_params=_cparams(("arbitrary", "arbitrary")),
    )(x, g.reshape(1, k), w)


def _matmul_residual_kernel(*refs, widths):
    a_refs = refs[:len(widths)]
    w_ref, r_ref, o_ref = refs[len(widths):]
    acc = r_ref[...]
    row = 0
    for a_ref, kw in zip(a_refs, widths):
        acc = acc + jnp.dot(a_ref[...], w_ref[pl.ds(row, kw), :], preferred_element_type=F32)
        row += kw
    o_ref[...] = acc


def matmul_residual(a_list, w, layer, res, tm=512):
    m = res.shape[0]
    _, k, n = w.shape
    widths = tuple(a.shape[1] for a in a_list)
    assert sum(widths) == k
    return pl.pallas_call(
        functools.partial(_matmul_residual_kernel, widths=widths),
        grid=(m // tm,),
        in_specs=[pl.BlockSpec((tm, kw), lambda i: (i, 0)) for kw in widths] + [
            pl.BlockSpec((None, k, n), lambda i: (layer, 0, 0)),
            pl.BlockSpec((tm, n), lambda i: (i, 0)),
        ],
        out_specs=pl.BlockSpec((tm, n), lambda i: (i, 0)),
        out_shape=jax.ShapeDtypeStruct((m, n), F32),
        compiler_params=_cparams(("arbitrary",)),
    )(*a_list, w, res)


def _mlp_kernel(x_ref, xres_ref, g_ref, w1_ref, w2_ref, o_ref, xn_ref, h_ref, *, up_steps, tf):
    s = pl.program_id(1)

    @pl.when(s == 0)
    def _():
        xn_ref[...] = _rms_rows(x_ref[...], g_ref[...]).astype(BF16)

    @pl.when(s < up_steps)
    def _():
        h = jnp.dot(xn_ref[...], w1_ref[...], preferred_element_type=F32)
        h_ref[s] = jnp.square(jnp.maximum(h, 0.0)).astype(BF16)

    @pl.when(s >= up_steps)
    def _():
        acc = xres_ref[...]
        for c in range(up_steps):
            acc = acc + jnp.dot(h_ref[c], w2_ref[pl.ds(c * tf, tf), :], preferred_element_type=F32)
        o_ref[...] = acc


def mlp(x, g, w1, w2, layer, tm=512, tf=1024, tn=512):
    m, d = x.shape
    dff = w1.shape[2]
    up_steps = dff // tf
    down_steps = d // tn
    return pl.pallas_call(
        functools.partial(_mlp_kernel, up_steps=up_steps, tf=tf),
        grid=(m // tm, up_steps + down_steps),
        in_specs=[
            pl.BlockSpec((tm, d), lambda i, s: (i, 0)),
            pl.BlockSpec((tm, tn), lambda i, s: (i, jnp.maximum(s - up_steps, 0))),
            pl.BlockSpec((1, d), lambda i, s: (0, 0)),
            pl.BlockSpec((None, d, tf), lambda i, s: (layer, 0, jnp.minimum(s, up_steps - 1))),
            pl.BlockSpec((None, dff, tn), lambda i, s: (layer, 0, jnp.maximum(s - up_steps, 0))),
        ],
        out_specs=pl.BlockSpec((tm, tn), lambda i, s: (i, jnp.maximum(s - up_steps, 0))),
        out_shape=jax.ShapeDtypeStruct((m, d), F32),
        scratch_shapes=[pltpu.VMEM((tm, d), BF16), pltpu.VMEM((up_steps, tm, tf), BF16)],
        compiler_params=_cparams(("arbitrary", "arbitrary")),
    )(x, x, g.reshape(1, d), w1, w2)


def _rmsnorm_kernel(x_ref, g_ref, o_ref):
    o_ref[...] = _rms_rows(x_ref[...], g_ref[...])


def rmsnorm(x, g, tm=512):
    m, d = x.shape
    return pl.pallas_call(
        _rmsnorm_kernel,
        grid=(m // tm,),
        in_specs=[pl.BlockSpec((tm, d), lambda i: (i, 0)), pl.BlockSpec((1, d), lambda i: (0, 0))],
        out_specs=pl.BlockSpec((tm, d), lambda i: (i, 0)),
        out_shape=jax.ShapeDtypeStruct((m, d), F32),
        compiler_params=_cparams(("arbitrary",)),
    )(x, g.reshape(1, d))


def _t5_bucket(rel):
    half = T5_BUCKETS // 2
    max_exact = half // 2
    n = jnp.abs(rel)
    nf = jnp.maximum(n, 1).astype(F32)
    large = max_exact + (jnp.log(nf / max_exact) / math.log(T5_MAX_DISTANCE / max_exact)
                         * (half - max_exact)).astype(jnp.int32)
    large = jnp.minimum(large, half - 1)
    return jnp.where(rel > 0, half, 0) + jnp.where(n < max_exact, n, large)


def _t5_bucket_tables():
    tabs = []
    for _, dil in DILATED_BRANCHES:
        off = jnp.arange(A_KBLOCK)[None, :] - A_HALF - jnp.arange(A_QBLOCK)[:, None]
        tabs.append(jnp.where(jnp.abs(off) <= A_HALF, _t5_bucket(off * dil), T5_BUCKETS))
    return jnp.stack(tabs).astype(jnp.int32)


def _dilated_sections(seq):
    bases, sizes, row = [], [], 0
    for _, dil in DILATED_BRANCHES:
        sec = seq // dil + 2 * A_HALF
        bases.append(row)
        sizes.append(sec)
        row += dil * sec
    return bases, sizes, row


def _dilated_kernel(t5_ref, q_ref, k_ref, v_ref, bucket_ref, o_ref,
                    kd_ref, vd_ref, tmp_ref, bias_ref, m_ref, l_ref, n_ref, *, seq):
    h = pl.program_id(0)
    t = pl.program_id(1)
    scale = 1.0 / math.sqrt(HEAD_DIM)
    bases, sizes, _ = _dilated_sections(seq)

    @pl.when(t == 0)
    def _():
        for b in range(len(DILATED_BRANCHES)):
            bk = bucket_ref[b]
            acc = jnp.full((A_QBLOCK, A_KBLOCK), NEG_INF, F32)
            for kbkt in range(T5_BUCKETS):
                acc = jnp.where(bk == kbkt, t5_ref[kbkt * A_HEADS + h], acc)
            bias_ref[b] = acc
        zeros = jnp.zeros((A_HALF, HEAD_DIM), BF16)
        (_, d0), (_, d1), (_, d2) = DILATED_BRANCHES
        assert d0 == 1 and d2 % d1 == 0
        ratio = d2 // d1

        def put(dst, b, r, rows):
            o = bases[b] + r * sizes[b]
            dst[pl.ds(o, A_HALF), :] = zeros
            dst[pl.ds(o + A_HALF + rows.shape[0], A_HALF), :] = zeros
            dst[pl.ds(o + A_HALF, rows.shape[0]), :] = rows.astype(BF16)

        for src, dst in ((k_ref, kd_ref), (v_ref, vd_ref)):
            put(dst, 0, 0, src[...])
            for r1 in range(d1):
                rows1 = src[pl.ds(r1, seq // d1, stride=d1), :]
                put(dst, 1, r1, rows1)
                tmp_ref[...] = rows1
                for a in range(ratio):
                    put(dst, 2, a * d1 + r1, tmp_ref[pl.ds(a, seq // d2, stride=ratio), :])

    kk = lax.broadcasted_iota(jnp.int32, (1, A_KBLOCK), 1)
    for b, (_, dil) in enumerate(DILATED_BRANCHES):
        sub_len = seq // dil
        blocks_per_residue = A_SUPER // dil // A_QBLOCK

        def body(idx, carry, b=b, dil=dil, sub_len=sub_len, blocks_per_residue=blocks_per_residue):
            r = idx // blocks_per_residue
            n = idx % blocks_per_residue
            qs = r + n * (A_QBLOCK * dil)
            q_idx = pl.ds(qs, A_QBLOCK) if dil == 1 else pl.ds(qs, A_QBLOCK, stride=dil)
            blk = t * blocks_per_residue + n
            k_idx = pl.ds(pl.multiple_of(bases[b] + r * sizes[b] + blk * A_QBLOCK, A_QBLOCK), A_KBLOCK)
            qb = (q_ref[q_idx, :] * scale).astype(BF16)
            s = lax.dot_general(qb, kd_ref[k_idx, :], (((1,), (1,)), ((), ())), preferred_element_type=F32)
            key_l = blk * A_QBLOCK - A_HALF + kk
            edge = jnp.where((key_l >= 0) & (key_l < sub_len), 0.0, NEG_INF)
            s = s + bias_ref[b] + edge
            m = jnp.max(s, axis=1, keepdims=True)
            p = jnp.exp(s - m)
            l = jnp.sum(p, axis=1, keepdims=True)
            num = jnp.dot(p.astype(BF16), vd_ref[k_idx, :], preferred_element_type=F32)
            m_ref[b, q_idx, :] = jnp.broadcast_to(m, (A_QBLOCK, HEAD_DIM))
            l_ref[b, q_idx, :] = jnp.broadcast_to(l, (A_QBLOCK, HEAD_DIM))
            n_ref[b, q_idx, :] = num
            return carry

        lax.fori_loop(0, A_SUPER // A_QBLOCK, body, 0, unroll=16)

    rows = 256
    for c in range(A_SUPER // rows):
        sl = pl.ds(c * rows, rows)
        m0, m1, m2 = m_ref[0, sl, :], m_ref[1, sl, :], m_ref[2, sl, :]
        mx = jnp.maximum(jnp.maximum(m0, m1), m2)
        w0, w1, w2 = jnp.exp(m0 - mx), jnp.exp(m1 - mx), jnp.exp(m2 - mx)
        num = w0 * n_ref[0, sl, :] + w1 * n_ref[1, sl, :] + w2 * n_ref[2, sl, :]
        den = w0 * l_ref[0, sl, :] + w1 * l_ref[1, sl, :] + w2 * l_ref[2, sl, :]
        o_ref[sl, :] = (num / den).astype(o_ref.dtype)


def dilated_attention(proj, t5_table, bucket_tabs, seq):
    assert seq % A_SUPER == 0
    nb = len(DILATED_BRANCHES)
    total_rows = _dilated_sections(seq)[2]
    return pl.pallas_call(
        functools.partial(_dilated_kernel, seq=seq),
        grid=(A_HEADS, seq // A_SUPER),
        in_specs=[
            pl.BlockSpec(memory_space=pltpu.SMEM),
            pl.BlockSpec((A_SUPER, HEAD_DIM), lambda h, t: (t, h)),
            pl.BlockSpec((seq, HEAD_DIM), lambda h, t: (0, A_HEADS + h)),
            pl.BlockSpec((seq, HEAD_DIM), lambda h, t: (0, 2 * A_HEADS + h)),
            pl.BlockSpec((nb, A_QBLOCK, A_KBLOCK), lambda h, t: (0, 0, 0)),
        ],
        out_specs=pl.BlockSpec((A_SUPER, HEAD_DIM), lambda h, t: (t, h)),
        out_shape=jax.ShapeDtypeStruct((seq, A_WIDTH), BF16),
        scratch_shapes=[
            pltpu.VMEM((total_rows, HEAD_DIM), BF16),
            pltpu.VMEM((total_rows, HEAD_DIM), BF16),
            pltpu.VMEM((seq // DILATED_BRANCHES[1][1], HEAD_DIM), F32),
            pltpu.VMEM((nb, A_QBLOCK, A_KBLOCK), F32),
            pltpu.VMEM((nb, A_SUPER, HEAD_DIM), F32),
            pltpu.VMEM((nb, A_SUPER, HEAD_DIM), F32),
            pltpu.VMEM((nb, A_SUPER, HEAD_DIM), F32),
        ],
        compiler_params=_cparams(("arbitrary", "arbitrary")),
    )(t5_table.astype(F32).reshape(-1), proj, proj, proj, bucket_tabs)


def _s5_discretise(lam_re, lam_im, log_step, b_re, b_im, c_re, c_im):
    g, p = lam_re.shape[1:]
    slabs = g // (2 * S5_PAIRS)
    step = jnp.exp(log_step.astype(F32))[..., None]
    lr = jnp.minimum(lam_re.astype(F32), -1e-4)
    li = lam_im.astype(F32)
    mag = jnp.exp(lr * step)
    ab_re = mag * jnp.cos(li * step)
    ab_im = mag * jnp.sin(li * step)
    den = lr * lr + li * li
    zr = ((ab_re - 1.0) * lr + ab_im * li) / den
    zi = (ab_im * lr - (ab_re - 1.0) * li) / den
    br = b_re.astype(F32)[None]
    bi = b_im.astype(F32)[None]
    bb_re = zr[..., None] * br - zi[..., None] * bi
    bb_im = zr[..., None] * bi + zi[..., None] * br

    a_re = ab_re.reshape(2, slabs, S5_PAIRS, 2 * p)
    a_im = ab_im.reshape(2, slabs, S5_PAIRS, 2 * p)

    eye2 = jnp.eye(2, dtype=F32)

    def pack_b(bb):
        bb = bb.reshape(2, slabs, S5_PAIRS, 2, p, B_GROUP)
        return jnp.einsum('dsjepc,ef->dsjecfp', bb, eye2).reshape(2, slabs, S5_SLAB, 2 * p)

    def pack_c(cc):
        cc = cc.astype(F32).reshape(2, slabs, S5_PAIRS, 2, B_GROUP, p)
        return jnp.einsum('dsjecp,ef->dsfpjec', cc, eye2).reshape(2, slabs, 2 * p, S5_SLAB)

    b_mat = jnp.concatenate([pack_b(bb_re), pack_b(bb_im)], axis=-1)
    c_mat = jnp.concatenate([pack_c(c_re), -pack_c(c_im)], axis=-2)
    return a_re, a_im, b_mat.astype(BF16), c_mat.astype(BF16)


def _s5_kernel(uf_ref, ub_ref, b_ref, c_ref, are_ref, aim_ref, yf_ref, yb_ref,
               xr_ref, xi_ref, bur_ref, bui_ref, *, tm, slabs):
    c = pl.program_id(0)
    nstate = 2 * B_STATE
    u_refs = (uf_ref, ub_ref)
    y_refs = (yf_ref, yb_ref)

    @pl.when(c == 0)
    def _():
        xr_ref[...] = jnp.zeros_like(xr_ref)
        xi_ref[...] = jnp.zeros_like(xi_ref)

    lane_pair = lax.broadcasted_iota(jnp.int32, (1, S5_SLAB), 1) // (2 * B_GROUP)

    def pair_rows(j):
        return pl.ds(j, tm, stride=S5_PITCH)

    for d in range(2):
        for s in range(slabs):
            u16 = u_refs[d][:, s * S5_SLAB:(s + 1) * S5_SLAB].astype(BF16)
            lhs = jnp.concatenate([jnp.where(lane_pair == j, u16, jnp.zeros_like(u16))
                                   for j in range(S5_PAIRS)], axis=0)
            bu = jnp.dot(lhs, b_ref[d, s], preferred_element_type=F32)
            for j in range(S5_PAIRS):
                bur_ref[d, s, pair_rows(j), :] = bu[j * tm:(j + 1) * tm, :nstate]
                bui_ref[d, s, pair_rows(j), :] = bu[j * tm:(j + 1) * tm, nstate:]

    ar = [[are_ref[d, s] for s in range(slabs)] for d in range(2)]
    ai = [[aim_ref[d, s] for s in range(slabs)] for d in range(2)]

    def step(i, carry):
        new = []
        for d in range(2):
            t = i if d == 0 else tm - 1 - i
            start = t * S5_PITCH
            rows = pl.ds(pl.multiple_of(start, 8) if S5_PITCH % 8 == 0 else start, S5_PAIRS)
            for s in range(slabs):
                k = 2 * (d * slabs + s)
                xr, xi = carry[k], carry[k + 1]
                nxr = ar[d][s] * xr - ai[d][s] * xi + bur_ref[d, s, rows, :]
                nxi = ar[d][s] * xi + ai[d][s] * xr + bui_ref[d, s, rows, :]
                bur_ref[d, s, rows, :] = nxr
                bui_ref[d, s, rows, :] = nxi
                new += [nxr, nxi]
        return tuple(new)

    init = []
    for d in range(2):
        for s in range(slabs):
            init += [xr_ref[d, s], xi_ref[d, s]]
    final = lax.fori_loop(0, tm, step, tuple(init), unroll=4)
    for d in range(2):
        for s in range(slabs):
            k = 2 * (d * slabs + s)
            xr_ref[d, s] = final[k]
            xi_ref[d, s] = final[k + 1]

    for d in range(2):
        for s in range(slabs):
            xcat = jnp.concatenate(
                [jnp.concatenate([bur_ref[d, s, pair_rows(j), :], bui_ref[d, s, pair_rows(j), :]],
                                 axis=1).astype(BF16) for j in range(S5_PAIRS)], axis=0)
            r = jnp.dot(xcat, c_ref[d, s], preferred_element_type=F32)
            y = r[:tm]
            for j in range(1, S5_PAIRS):
                y = jnp.where(lane_pair == j, r[j * tm:(j + 1) * tm], y)
            y_refs[d][:, s * S5_SLAB:(s + 1) * S5_SLAB] = y


def s5_scan(proj, u_col0, a_re, a_im, b_mat, c_mat, seq, tm=256):
    slabs = a_re.shape[1]
    width = slabs * S5_SLAB
    nchunks = seq // tm
    assert u_col0 % width == 0
    ucol = u_col0 // width
    wspec = pl.BlockSpec((2, slabs, S5_SLAB, S5_SLAB), lambda c: (0, 0, 0, 0))
    aspec = pl.BlockSpec((2, slabs, S5_PAIRS, 2 * B_STATE), lambda c: (0, 0, 0, 0))
    state = pltpu.VMEM((2, slabs, S5_PAIRS, 2 * B_STATE), F32)
    drive = pltpu.VMEM((2, slabs, tm * S5_PITCH, 2 * B_STATE), F32)
    return pl.pallas_call(
        functools.partial(_s5_kernel, tm=tm, slabs=slabs),
        grid=(nchunks,),
        in_specs=[pl.BlockSpec((tm, width), lambda c: (c, ucol)),
                  pl.BlockSpec((tm, width), lambda c: (nchunks - 1 - c, ucol)),
                  wspec, wspec, aspec, aspec],
        out_specs=[pl.BlockSpec((tm, width), lambda c: (c, 0)),
                   pl.BlockSpec((tm, width), lambda c: (nchunks - 1 - c, 0))],
        out_shape=[jax.ShapeDtypeStruct((seq, width), F32), jax.ShapeDtypeStruct((seq, width), F32)],
        scratch_shapes=[state, state, drive, drive],
        compiler_params=_cparams(("arbitrary",)),
    )(proj, proj, b_mat, c_mat, a_re, a_im)


def _glu_kernel(yf_ref, yb_ref, u_ref, dsk_ref, w_ref, o_ref):
    y = jax.nn.gelu(yf_ref[...] + yb_ref[...] + dsk_ref[...] * u_ref[...])
    z = jnp.dot(y.astype(BF16), w_ref[...], preferred_element_type=F32)
    o_ref[...] = (y * jax.nn.sigmoid(z)).astype(o_ref.dtype)


def glu(yf, yb, proj, u_col0, d_skip, w, layer, tm=512):
    m, k = yf.shape
    return pl.pallas_call(
        _glu_kernel,
        grid=(m // tm,),
        in_specs=[
            pl.BlockSpec((tm, k), lambda i: (i, 0)),
            pl.BlockSpec((tm, k), lambda i: (i, 0)),
            pl.BlockSpec((tm, k), lambda i: (i, u_col0 // k)),
            pl.BlockSpec((1, k), lambda i: (0, 0)),
            pl.BlockSpec((None, k, k), lambda i: (layer, 0, 0)),
        ],
        out_specs=pl.BlockSpec((tm, k), lambda i: (i, 0)),
        out_shape=jax.ShapeDtypeStruct((m, k), BF16),
        compiler_params=_cparams(("arbitrary",)),
    )(yf, yb, proj, d_skip.reshape(1, k), w)


def _na_bias_pairs(rpb):
    c = np.arange(GRID_W)
    col_idx = np.clip(c[None, :] - c[:, None] + NA_COLS - 1, 0, 2 * NA_COLS - 2)
    onehot = (col_idx[None] == np.arange(2 * NA_COLS - 1)[:, None, None]).astype(np.float32)
    r2 = jnp.einsum('hdk,kqc->hdqc', rpb.astype(F32), jnp.asarray(onehot), precision=lax.Precision.HIGHEST)
    r2 = jnp.pad(r2, ((0, 0), (4, 4), (0, 0), (0, 0)))
    return jnp.concatenate([r2[:, :-1], r2[:, 1:]], axis=-1)


def _na_mask_tables():
    qr = np.arange(NA_QROWS)[:, None, None, None]
    cq = np.arange(GRID_W)[None, :, None, None]
    kr = np.arange(NA_KROWS)[None, None, :, None]
    ck = np.arange(GRID_W)[None, None, None, :]
    cs = np.clip(cq - NA_COLS // 2, 0, GRID_W - NA_COLS)
    col_ok = (ck >= cs) & (ck < cs + NA_COLS)
    row_ok = [(kr < NA_ROWS) & (qr >= 0),
              (kr >= qr) & (kr < qr + NA_ROWS),
              (kr >= NA_KROWS - NA_ROWS) & (qr >= 0)]
    out = [np.where(r & col_ok, 0.0, NEG_INF).reshape(NA_QROWS * GRID_W, NA_KROWS * GRID_W) for r in row_ok]
    return np.stack(out).astype(np.float32)


def _natten_kernel(q_ref, k_ref, v_ref, pair_ref, mask_ref, o_ref, bias_ref, *, groups, rows):
    gs = pl.program_id(1)
    nk = NA_KROWS * GRID_W
    tq = NA_QROWS * GRID_W

    @pl.when(gs == 0)
    def _():
        for hh in range(NA_HEADS_PER_STEP):
            for v in range(3):
                for qr in range(NA_QROWS):
                    q_rows = pl.ds(qr * GRID_W, GRID_W)
                    for kr2 in range(NA_KROWS // 2):
                        k_cols = pl.ds(kr2 * 128, 128)
                        bias_ref[hh, v, q_rows, k_cols] = (
                            pair_ref[hh, 2 * (kr2 + 2 * (2 - v)) - qr + 3] + mask_ref[v, q_rows, k_cols])

    for gi in range(NA_GROUPS_PER_STEP):
        g = gs * NA_GROUPS_PER_STEP + gi
        q_rows = pl.ds(gi * tq, tq)
        key_row0 = jnp.clip(NA_QROWS * g - NA_ROWS // 2, 0, rows - NA_KROWS)
        tok0 = pl.multiple_of(key_row0 * GRID_W, GRID_W)
        variant = jnp.where(g == 0, 0, jnp.where(g == groups - 1, 2, 1))
        for hh in range(NA_HEADS_PER_STEP):
            cols = pl.ds(hh * HEAD_DIM, HEAD_DIM)
            s = lax.dot_general(q_ref[q_rows, cols], k_ref[pl.ds(tok0, nk), cols], (((1,), (1,)), ((), ())),
                                preferred_element_type=F32)
            s = s + bias_ref[hh, variant]
            m = jnp.max(s, axis=1, keepdims=True)
            p = jnp.exp(s - m)
            l = jnp.sum(p, axis=1, keepdims=True)
            o = jnp.dot(p.astype(BF16), v_ref[pl.ds(tok0, nk), cols], preferred_element_type=F32)
            o_ref[q_rows, cols] = (o / l).astype(o_ref.dtype)


def neighbourhood_attention(qkv, bias_pairs, mask_tabs, seq, heads):
    rows = seq // GRID_W
    assert rows % (NA_QROWS * NA_GROUPS_PER_STEP) == 0 and rows >= NA_KROWS + NA_QROWS
    assert heads % NA_HEADS_PER_STEP == 0
    groups = rows // NA_QROWS
    tq = NA_QROWS * GRID_W * NA_GROUPS_PER_STEP
    hw = NA_HEADS_PER_STEP * HEAD_DIM
    hsteps = heads // NA_HEADS_PER_STEP
    return pl.pallas_call(
        functools.partial(_natten_kernel, groups=groups, rows=rows),
        grid=(hsteps, groups // NA_GROUPS_PER_STEP),
        in_specs=[
            pl.BlockSpec((tq, hw), lambda h, g: (g, h)),
            pl.BlockSpec((seq, hw), lambda h, g: (0, hsteps + h)),
            pl.BlockSpec((seq, hw), lambda h, g: (0, 2 * hsteps + h)),
            pl.BlockSpec((NA_HEADS_PER_STEP,) + bias_pairs.shape[1:], lambda h, g: (h, 0, 0, 0)),
            pl.BlockSpec(mask_tabs.shape, lambda h, g: (0, 0, 0)),
        ],
        out_specs=pl.BlockSpec((tq, hw), lambda h, g: (g, h)),
        out_shape=jax.ShapeDtypeStruct((seq, heads * HEAD_DIM), BF16),
        scratch_shapes=[pltpu.VMEM((NA_HEADS_PER_STEP, 3, NA_QROWS * GRID_W, NA_KROWS * GRID_W), F32)],
        compiler_params=_cparams(("arbitrary", "arbitrary")),
    )(qkv, qkv, qkv, bias_pairs, mask_tabs)


def kernel(x, t5_bias, ab_w_in, ab_w_out, s5_lam_re, s5_lam_im, s5_log_step, s5_b_re, s5_b_im, s5_c_re, s5_c_im, s5_d, s5_w_glu, c_w_qkv, c_w_out, c_rpb, norm_mix, norm_mlp, mlp_w1, mlp_w2, norm_final):
    batch, seq, d_model = x.shape
    depth = norm_mix.shape[0]
    c_heads = c_rpb.shape[1]
    t5_buckets = _t5_bucket_tables()
    na_mask = jnp.asarray(_na_mask_tables())
    attn_scale = 1.0 / math.sqrt(HEAD_DIM)
    w_in, w_out, w_glu = ab_w_in.astype(BF16), ab_w_out.astype(BF16), s5_w_glu.astype(BF16)
    w_qkv, w_cout = c_w_qkv.astype(BF16), c_w_out.astype(BF16)
    w1, w2 = mlp_w1.astype(BF16), mlp_w2.astype(BF16)
    outs = []
    for bi in range(batch):
        h = x[bi]
        for i in range(depth):
            j = i // 2
            if i % 2 == 0:
                proj = norm_matmul(h, norm_mix[i], w_in, j, F32)
                o_a = dilated_attention(proj, t5_bias, t5_buckets, seq)
                a_re, a_im, b_mat, c_mat = _s5_discretise(
                    s5_lam_re[j], s5_lam_im[j], s5_log_step[j], s5_b_re[j], s5_b_im[j], s5_c_re[j], s5_c_im[j])
                yf, yb = s5_scan(proj, 3 * A_WIDTH, a_re, a_im, b_mat, c_mat, seq)
                o_b = glu(yf, yb, proj, 3 * A_WIDTH, s5_d[j], w_glu, j)
                h = matmul_residual([o_a, o_b], w_out, j, h)
            else:
                qkv = norm_matmul(h, norm_mix[i], w_qkv, j, BF16, scaled_cols=c_heads * HEAD_DIM, scale=attn_scale)
                o = neighbourhood_attention(qkv, _na_bias_pairs(c_rpb[j]), na_mask, seq, c_heads)
                h = matmul_residual([o], w_cout, j, h)
            h = mlp(h, norm_mlp[i], w1, w2, i)
        outs.append(rmsnorm(h, norm_final))
    return jnp.stack(outs)
```

```python
import functools
import math

import numpy as np
import jax
import jax.numpy as jnp
from jax import lax
from jax.experimental import pallas as pl
from jax.experimental.pallas import tpu as pltpu

F32 = jnp.float32
BF16 = jnp.bfloat16

HEAD_DIM = 128
A_HEADS = 8
A_WIDTH = A_HEADS * HEAD_DIM
DILATED_BRANCHES = ((128, 1), (512, 4), (2048, 16))
A_QBLOCK = 128
A_HALF = 64
A_KBLOCK = A_QBLOCK + 2 * A_HALF
A_SUPER = 2048
B_GROUP = 16
B_STATE = 64
S5_SLAB = 256
S5_PAIRS = 8
S5_PITCH = 9
GRID_W = 64
NA_ROWS = 8
NA_COLS = 16
NA_QROWS = 4
NA_KROWS = 12
NA_EXT = NA_KROWS + 8
NA_HEADS_PER_STEP = 2
NA_GROUPS_PER_STEP = 4
T5_BUCKETS = 32
T5_MAX_DISTANCE = 1024
RMS_EPS = 1e-6
NEG_INF = -1e30
LOG2E = math.log2(math.e)
VMEM_LIMIT = 56 * 1024 * 1024


def _cparams(sem):
    return pltpu.CompilerParams(dimension_semantics=sem, vmem_limit_bytes=VMEM_LIMIT)


def _rms_rows(x, g):
    y = x * lax.rsqrt(jnp.mean(x * x, axis=-1, keepdims=True) + RMS_EPS)
    return y * g


def _norm_matmul_kernel(x_ref, g_ref, w_ref, o_ref, xn_ref, *, scaled_blocks, scale):
    j = pl.program_id(1)

    @pl.when(j == 0)
    def _():
        xn_ref[...] = _rms_rows(x_ref[...], g_ref[...]).astype(BF16)

    acc = jnp.dot(xn_ref[...], w_ref[...], preferred_element_type=F32)
    if scaled_blocks:
        acc = acc * jnp.where(j < scaled_blocks, scale, 1.0)
    o_ref[...] = acc.astype(o_ref.dtype)


def norm_matmul(x, g, w, layer, out_dtype, scaled_cols=0, scale=1.0, tm=1024, tn=512):
    m, k = x.shape
    n = w.shape[2]
    assert scaled_cols % tn == 0
    return pl.pallas_call(
        functools.partial(_norm_matmul_kernel, scaled_blocks=scaled_cols // tn, scale=scale),
        grid=(m // tm, n // tn),
        in_specs=[
            pl.BlockSpec((tm, k), lambda i, j: (i, 0)),
            pl.BlockSpec((1, k), lambda i, j: (0, 0)),
            pl.BlockSpec((None, k, tn), lambda i, j: (layer, 0, j)),
        ],
        out_specs=pl.BlockSpec((tm, tn), lambda i, j: (i, j)),
        out_shape=jax.ShapeDtypeStruct((m, n), out_dtype),
        scratch_shapes=[pltpu.VMEM((tm, k), BF16)],
        compiler_params=_cparams(("arbitrary", "arbitrary")),
    )(x, g.reshape(1, k), w)


def _matmul_residual_kernel(*refs, widths):
    a_refs = refs[:len(widths)]
    w_ref, r_ref, o_ref = refs[len(widths):]
    acc = r_ref[...]
    row = 0
    for a_ref, kw in zip(a_refs, widths):
        acc = acc + jnp.dot(a_ref[...], w_ref[pl.ds(row, kw), :], preferred_element_type=F32)
        row += kw
    o_ref[...] = acc


def matmul_residual(a_list, w, layer, res, tm=512):
    m = res.shape[0]
    _, k, n = w.shape
    widths = tuple(a.shape[1] for a in a_list)
    assert sum(widths) == k
    return pl.pallas_call(
        functools.partial(_matmul_residual_kernel, widths=widths),
        grid=(m // tm,),
        in_specs=[pl.BlockSpec((tm, kw), lambda i: (i, 0)) for kw in widths] + [
            pl.BlockSpec((None, k, n), lambda i: (layer, 0, 0)),
            pl.BlockSpec((tm, n), lambda i: (i, 0)),
        ],
        out_specs=pl.BlockSpec((tm, n), lambda i: (i, 0)),
        out_shape=jax.ShapeDtypeStruct((m, n), F32),
        compiler_params=_cparams(("arbitrary",)),
    )(*a_list, w, res)


def _mlp_kernel(x_ref, xres_ref, g_ref, w1_ref, w2_ref, o_ref, xn_ref, h_ref, *, up_steps, tf):
    s = pl.program_id(1)

    @pl.when(s == 0)
    def _():
        xn_ref[...] = _rms_rows(x_ref[...], g_ref[...]).astype(BF16)

    @pl.when(s < up_steps)
    def _():
        h = jnp.dot(xn_ref[...], w1_ref[...], preferred_element_type=F32)
        h_ref[s] = jnp.square(jnp.maximum(h, 0.0)).astype(BF16)

    @pl.when(s >= up_steps)
    def _():
        acc = xres_ref[...]
        for c in range(up_steps):
            acc = acc + jnp.dot(h_ref[c], w2_ref[pl.ds(c * tf, tf), :], preferred_element_type=F32)
        o_ref[...] = acc


def mlp(x, g, w1, w2, layer, tm=512, tf=1024, tn=512):
    m, d = x.shape
    dff = w1.shape[2]
    up_steps = dff // tf
    down_steps = d // tn
    return pl.pallas_call(
        functools.partial(_mlp_kernel, up_steps=up_steps, tf=tf),
        grid=(m // tm, up_steps + down_steps),
        in_specs=[
            pl.BlockSpec((tm, d), lambda i, s: (i, 0)),
            pl.BlockSpec((tm, tn), lambda i, s: (i, jnp.maximum(s - up_steps, 0))),
            pl.BlockSpec((1, d), lambda i, s: (0, 0)),
            pl.BlockSpec((None, d, tf), lambda i, s: (layer, 0, jnp.minimum(s, up_steps - 1))),
            pl.BlockSpec((None, dff, tn), lambda i, s: (layer, 0, jnp.maximum(s - up_steps, 0))),
        ],
        out_specs=pl.BlockSpec((tm, tn), lambda i, s: (i, jnp.maximum(s - up_steps, 0))),
        out_shape=jax.ShapeDtypeStruct((m, d), F32),
        scratch_shapes=[pltpu.VMEM((tm, d), BF16), pltpu.VMEM((up_steps, tm, tf), BF16)],
        compiler_params=_cparams(("arbitrary", "arbitrary")),
    )(x, x, g.reshape(1, d), w1, w2)


def _rmsnorm_kernel(x_ref, g_ref, o_ref):
    o_ref[...] = _rms_rows(x_ref[...], g_ref[...])


def rmsnorm(x, g, tm=512):
    m, d = x.shape
    return pl.pallas_call(
        _rmsnorm_kernel,
        grid=(m // tm,),
        in_specs=[pl.BlockSpec((tm, d), lambda i: (i, 0)), pl.BlockSpec((1, d), lambda i: (0, 0))],
        out_specs=pl.BlockSpec((tm, d), lambda i: (i, 0)),
        out_shape=jax.ShapeDtypeStruct((m, d), F32),
        compiler_params=_cparams(("arbitrary",)),
    )(x, g.reshape(1, d))


def _t5_bucket(rel):
    half = T5_BUCKETS // 2
    max_exact = half // 2
    n = jnp.abs(rel)
    nf = jnp.maximum(n, 1).astype(F32)
    large = max_exact + (jnp.log(nf / max_exact) / math.log(T5_MAX_DISTANCE / max_exact)
                         * (half - max_exact)).astype(jnp.int32)
    large = jnp.minimum(large, half - 1)
    return jnp.where(rel > 0, half, 0) + jnp.where(n < max_exact, n, large)


def _t5_bucket_tables():
    tabs = []
    for _, dil in DILATED_BRANCHES:
        off = jnp.arange(A_KBLOCK)[None, :] - A_HALF - jnp.arange(A_QBLOCK)[:, None]
        tabs.append(jnp.where(jnp.abs(off) <= A_HALF, _t5_bucket(off * dil), T5_BUCKETS))
    return jnp.stack(tabs).astype(jnp.int32)


def _dilated_sections(seq):
    bases, sizes, row = [], [], 0
    for _, dil in DILATED_BRANCHES:
        sec = seq // dil + 2 * A_HALF
        bases.append(row)
        sizes.append(sec)
        row += dil * sec
    return bases, sizes, row


def _dilated_kernel(t5_ref, q_ref, k_ref, v_ref, bucket_ref, o_ref,
                    kd_ref, vd_ref, tmp_ref, bias_ref, m_ref, l_ref, n_ref, *, seq):
    h = pl.program_id(0)
    t = pl.program_id(1)
    scale = LOG2E / math.sqrt(HEAD_DIM)
    bases, sizes, _ = _dilated_sections(seq)

    @pl.when(t == 0)
    def _():
        for b in range(len(DILATED_BRANCHES)):
            bk = bucket_ref[b]
            acc = jnp.full((A_QBLOCK, A_KBLOCK), NEG_INF, F32)
            for kbkt in range(T5_BUCKETS):
                acc = jnp.where(bk == kbkt, t5_ref[kbkt * A_HEADS + h], acc)
            bias_ref[b] = acc * LOG2E
        zeros = jnp.zeros((A_HALF, HEAD_DIM), BF16)
        (_, d0), (_, d1), (_, d2) = DILATED_BRANCHES
        assert d0 == 1 and d2 % d1 == 0
        ratio = d2 // d1

        def put(dst, b, r, rows):
            o = bases[b] + r * sizes[b]
            dst[pl.ds(o, A_HALF), :] = zeros
            dst[pl.ds(o + A_HALF + rows.shape[0], A_HALF), :] = zeros
            dst[pl.ds(o + A_HALF, rows.shape[0]), :] = rows.astype(BF16)

        for src, dst in ((k_ref, kd_ref), (v_ref, vd_ref)):
            put(dst, 0, 0, src[...])
            for r1 in range(d1):
                rows1 = src[pl.ds(r1, seq // d1, stride=d1), :]
                put(dst, 1, r1, rows1)
                tmp_ref[...] = rows1
                for a in range(ratio):
                    put(dst, 2, a * d1 + r1, tmp_ref[pl.ds(a, seq // d2, stride=ratio), :])

    kk = lax.broadcasted_iota(jnp.int32, (1, A_KBLOCK), 1)
    for b, (_, dil) in enumerate(DILATED_BRANCHES):
        sub_len = seq // dil
        blocks_per_residue = A_SUPER // dil // A_QBLOCK

        def body(idx, carry, b=b, dil=dil, sub_len=sub_len, blocks_per_residue=blocks_per_residue):
            r = idx // blocks_per_residue
            n = idx % blocks_per_residue
            qs = r + n * (A_QBLOCK * dil)
            q_idx = pl.ds(qs, A_QBLOCK) if dil == 1 else pl.ds(qs, A_QBLOCK, stride=dil)
            blk = t * blocks_per_residue + n
            k_idx = pl.ds(pl.multiple_of(bases[b] + r * sizes[b] + blk * A_QBLOCK, A_QBLOCK), A_KBLOCK)
            qb = (q_ref[q_idx, :] * scale).astype(BF16)
            s = lax.dot_general(qb, kd_ref[k_idx, :], (((1,), (1,)), ((), ())), preferred_element_type=F32)
            key_l = blk * A_QBLOCK - A_HALF + kk
            edge = jnp.where((key_l >= 0) & (key_l < sub_len), 0.0, NEG_INF)
            s = s + bias_ref[b] + edge
            m = jnp.max(s, axis=1, keepdims=True)
            p = jnp.exp2(s - m)
            l = jnp.sum(p, axis=1, keepdims=True)
            num = jnp.dot(p.astype(BF16), vd_ref[k_idx, :], preferred_element_type=F32)
            m_ref[b, q_idx, :] = jnp.broadcast_to(m, (A_QBLOCK, HEAD_DIM))
            l_ref[b, q_idx, :] = jnp.broadcast_to(l, (A_QBLOCK, HEAD_DIM))
            n_ref[b, q_idx, :] = num
            return carry

        lax.fori_loop(0, A_SUPER // A_QBLOCK, body, 0, unroll=16)

    rows = 256
    for c in range(A_SUPER // rows):
        sl = pl.ds(c * rows, rows)
        m0, m1, m2 = m_ref[0, sl, :], m_ref[1, sl, :], m_ref[2, sl, :]
        mx = jnp.maximum(jnp.maximum(m0, m1), m2)
        w0, w1, w2 = jnp.exp2(m0 - mx), jnp.exp2(m1 - mx), jnp.exp2(m2 - mx)
        num = w0 * n_ref[0, sl, :] + w1 * n_ref[1, sl, :] + w2 * n_ref[2, sl, :]
        den = w0 * l_ref[0, sl, :] + w1 * l_ref[1, sl, :] + w2 * l_ref[2, sl, :]
        o_ref[sl, :] = (num / den).astype(o_ref.dtype)


def dilated_attention(proj, t5_table, bucket_tabs, seq):
    assert seq % A_SUPER == 0
    nb = len(DILATED_BRANCHES)
    total_rows = _dilated_sections(seq)[2]
    return pl.pallas_call(
        functools.partial(_dilated_kernel, seq=seq),
        grid=(A_HEADS, seq // A_SUPER),
        in_specs=[
            pl.BlockSpec(memory_space=pltpu.SMEM),
            pl.BlockSpec((A_SUPER, HEAD_DIM), lambda h, t: (t, h)),
            pl.BlockSpec((seq, HEAD_DIM), lambda h, t: (0, A_HEADS + h)),
            pl.BlockSpec((seq, HEAD_DIM), lambda h, t: (0, 2 * A_HEADS + h)),
            pl.BlockSpec((nb, A_QBLOCK, A_KBLOCK), lambda h, t: (0, 0, 0)),
        ],
        out_specs=pl.BlockSpec((A_SUPER, HEAD_DIM), lambda h, t: (t, h)),
        out_shape=jax.ShapeDtypeStruct((seq, A_WIDTH), BF16),
        scratch_shapes=[
            pltpu.VMEM((total_rows, HEAD_DIM), BF16),
            pltpu.VMEM((total_rows, HEAD_DIM), BF16),
            pltpu.VMEM((seq // DILATED_BRANCHES[1][1], HEAD_DIM), F32),
            pltpu.VMEM((nb, A_QBLOCK, A_KBLOCK), F32),
            pltpu.VMEM((nb, A_SUPER, HEAD_DIM), F32),
            pltpu.VMEM((nb, A_SUPER, HEAD_DIM), F32),
            pltpu.VMEM((nb, A_SUPER, HEAD_DIM), F32),
        ],
        compiler_params=_cparams(("arbitrary", "arbitrary")),
    )(t5_table.astype(F32).reshape(-1), proj, proj, proj, bucket_tabs)


def _s5_discretise(lam_re, lam_im, log_step, b_re, b_im, c_re, c_im):
    g, p = lam_re.shape[1:]
    slabs = g // (2 * S5_PAIRS)
    step = jnp.exp(log_step.astype(F32))[..., None]
    lr = jnp.minimum(lam_re.astype(F32), -1e-4)
    li = lam_im.astype(F32)
    mag = jnp.exp(lr * step)
    ab_re = mag * jnp.cos(li * step)
    ab_im = mag * jnp.sin(li * step)
    den = lr * lr + li * li
    zr = ((ab_re - 1.0) * lr + ab_im * li) / den
    zi = (ab_im * lr - (ab_re - 1.0) * li) / den
    br = b_re.astype(F32)[None]
    bi = b_im.astype(F32)[None]
    bb_re = zr[..., None] * br - zi[..., None] * bi
    bb_im = zr[..., None] * bi + zi[..., None] * br

    a_re = ab_re.reshape(2, slabs, S5_PAIRS, 2 * p)
    a_im = ab_im.reshape(2, slabs, S5_PAIRS, 2 * p)

    eye2 = jnp.eye(2, dtype=F32)

    def pack_b(bb):
        bb = bb.reshape(2, slabs, S5_PAIRS, 2, p, B_GROUP)
        return jnp.einsum('dsjepc,ef->dsjecfp', bb, eye2).reshape(2, slabs, S5_SLAB, 2 * p)

    def pack_c(cc):
        cc = cc.astype(F32).reshape(2, slabs, S5_PAIRS, 2, B_GROUP, p)
        return jnp.einsum('dsjecp,ef->dsfpjec', cc, eye2).reshape(2, slabs, 2 * p, S5_SLAB)

    b_mat = jnp.concatenate([pack_b(bb_re), pack_b(bb_im)], axis=-1)
    c_mat = jnp.concatenate([pack_c(c_re), -pack_c(c_im)], axis=-2)
    return a_re, a_im, b_mat.astype(BF16), c_mat.astype(BF16)


def _s5_kernel(uf_ref, ub_ref, b_ref, c_ref, are_ref, aim_ref, yf_ref, yb_ref,
               xr_ref, xi_ref, bur_ref, bui_ref, *, tm, slabs):
    c = pl.program_id(0)
    nstate = 2 * B_STATE
    u_refs = (uf_ref, ub_ref)
    y_refs = (yf_ref, yb_ref)

    @pl.when(c == 0)
    def _():
        xr_ref[...] = jnp.zeros_like(xr_ref)
        xi_ref[...] = jnp.zeros_like(xi_ref)

    lane_pair = lax.broadcasted_iota(jnp.int32, (1, S5_SLAB), 1) // (2 * B_GROUP)

    def pair_rows(j):
        return pl.ds(j, tm, stride=S5_PITCH)

    for d in range(2):
        for s in range(slabs):
            u16 = u_refs[d][:, s * S5_SLAB:(s + 1) * S5_SLAB].astype(BF16)
            lhs = jnp.concatenate([jnp.where(lane_pair == j, u16, jnp.zeros_like(u16))
                                   for j in range(S5_PAIRS)], axis=0)
            bu = jnp.dot(lhs, b_ref[d, s], preferred_element_type=F32)
            for j in range(S5_PAIRS):
                bur_ref[d, s, pair_rows(j), :] = bu[j * tm:(j + 1) * tm, :nstate]
                bui_ref[d, s, pair_rows(j), :] = bu[j * tm:(j + 1) * tm, nstate:]

    ar = [[are_ref[d, s] for s in range(slabs)] for d in range(2)]
    ai = [[aim_ref[d, s] for s in range(slabs)] for d in range(2)]

    def step(i, carry):
        new = []
        for d in range(2):
            t = i if d == 0 else tm - 1 - i
            start = t * S5_PITCH
            rows = pl.ds(pl.multiple_of(start, 8) if S5_PITCH % 8 == 0 else start, S5_PAIRS)
            for s in range(slabs):
                k = 2 * (d * slabs + s)
                xr, xi = carry[k], carry[k + 1]
                nxr = ar[d][s] * xr - ai[d][s] * xi + bur_ref[d, s, rows, :]
                nxi = ar[d][s] * xi + ai[d][s] * xr + bui_ref[d, s, rows, :]
                bur_ref[d, s, rows, :] = nxr
                bui_ref[d, s, rows, :] = nxi
                new += [nxr, nxi]
        return tuple(new)

    init = []
    for d in range(2):
        for s in range(slabs):
            init += [xr_ref[d, s], xi_ref[d, s]]
    final = tuple(init)
    for i in range(tm):
        final = step(i, final)
    for d in range(2):
        for s in range(slabs):
            k = 2 * (d * slabs + s)
            xr_ref[d, s] = final[k]
            xi_ref[d, s] = final[k + 1]

    for d in range(2):
        for s in range(slabs):
            xcat = jnp.concatenate(
                [jnp.concatenate([bur_ref[d, s, pair_rows(j), :], bui_ref[d, s, pair_rows(j), :]],
                                 axis=1).astype(BF16) for j in range(S5_PAIRS)], axis=0)
            r = jnp.dot(xcat, c_ref[d, s], preferred_element_type=F32)
            y = r[:tm]
            for j in range(1, S5_PAIRS):
                y = jnp.where(lane_pair == j, r[j * tm:(j + 1) * tm], y)
            y_refs[d][:, s * S5_SLAB:(s + 1) * S5_SLAB] = y


def s5_scan(proj, u_col0, a_re, a_im, b_mat, c_mat, seq, tm=256):
    slabs = a_re.shape[1]
    width = slabs * S5_SLAB
    nchunks = seq // tm
    assert u_col0 % width == 0
    ucol = u_col0 // width
    wspec = pl.BlockSpec((2, slabs, S5_SLAB, S5_SLAB), lambda c: (0, 0, 0, 0))
    aspec = pl.BlockSpec((2, slabs, S5_PAIRS, 2 * B_STATE), lambda c: (0, 0, 0, 0))
    state = pltpu.VMEM((2, slabs, S5_PAIRS, 2 * B_STATE), F32)
    drive = pltpu.VMEM((2, slabs, tm * S5_PITCH, 2 * B_STATE), F32)
    return pl.pallas_call(
        functools.partial(_s5_kernel, tm=tm, slabs=slabs),
        grid=(nchunks,),
        in_specs=[pl.BlockSpec((tm, width), lambda c: (c, ucol)),
                  pl.BlockSpec((tm, width), lambda c: (nchunks - 1 - c, ucol)),
                  wspec, wspec, aspec, aspec],
        out_specs=[pl.BlockSpec((tm, width), lambda c: (c, 0)),
                   pl.BlockSpec((tm, width), lambda c: (nchunks - 1 - c, 0))],
        out_shape=[jax.ShapeDtypeStruct((seq, width), F32), jax.ShapeDtypeStruct((seq, width), F32)],
        scratch_shapes=[state, state, drive, drive],
        compiler_params=_cparams(("arbitrary",)),
    )(proj, proj, b_mat, c_mat, a_re, a_im)


def _glu_kernel(yf_ref, yb_ref, u_ref, dsk_ref, w_ref, o_ref):
    y = jax.nn.gelu(yf_ref[...] + yb_ref[...] + dsk_ref[...] * u_ref[...])
    z = jnp.dot(y.astype(BF16), w_ref[...], preferred_element_type=F32)
    o_ref[...] = (y * jax.nn.sigmoid(z)).astype(o_ref.dtype)


def glu(yf, yb, proj, u_col0, d_skip, w, layer, tm=512):
    m, k = yf.shape
    return pl.pallas_call(
        _glu_kernel,
        grid=(m // tm,),
        in_specs=[
            pl.BlockSpec((tm, k), lambda i: (i, 0)),
            pl.BlockSpec((tm, k), lambda i: (i, 0)),
            pl.BlockSpec((tm, k), lambda i: (i, u_col0 // k)),
            pl.BlockSpec((1, k), lambda i: (0, 0)),
            pl.BlockSpec((None, k, k), lambda i: (layer, 0, 0)),
        ],
        out_specs=pl.BlockSpec((tm, k), lambda i: (i, 0)),
        out_shape=jax.ShapeDtypeStruct((m, k), BF16),
        compiler_params=_cparams(("arbitrary",)),
    )(yf, yb, proj, d_skip.reshape(1, k), w)


def _na_bias_pairs(rpb):
    c = np.arange(GRID_W)
    col_idx = np.clip(c[None, :] - c[:, None] + NA_COLS - 1, 0, 2 * NA_COLS - 2)
    onehot = (col_idx[None] == np.arange(2 * NA_COLS - 1)[:, None, None]).astype(np.float32)
    r2 = jnp.einsum('hdk,kqc->hdqc', rpb.astype(F32), jnp.asarray(onehot), precision=lax.Precision.HIGHEST)
    r2 = jnp.pad(r2, ((0, 0), (4, 4), (0, 0), (0, 0)))
    return jnp.concatenate([r2[:, :-1], r2[:, 1:]], axis=-1)


def _na_mask_tables():
    qr = np.arange(NA_QROWS)[:, None, None, None]
    cq = np.arange(GRID_W)[None, :, None, None]
    kr = np.arange(NA_KROWS)[None, None, :, None]
    ck = np.arange(GRID_W)[None, None, None, :]
    cs = np.clip(cq - NA_COLS // 2, 0, GRID_W - NA_COLS)
    col_ok = (ck >= cs) & (ck < cs + NA_COLS)
    row_ok = [(kr < NA_ROWS) & (qr >= 0),
              (kr >= qr) & (kr < qr + NA_ROWS),
              (kr >= NA_KROWS - NA_ROWS) & (qr >= 0)]
    out = [np.where(r & col_ok, 0.0, NEG_INF).reshape(NA_QROWS * GRID_W, NA_KROWS * GRID_W) for r in row_ok]
    return np.stack(out).astype(np.float32)


def _natten_kernel(q_ref, k_ref, v_ref, pair_ref, mask_ref, o_ref, bias_ref, *, groups, rows):
    gs = pl.program_id(1)
    nk = NA_KROWS * GRID_W
    tq = NA_QROWS * GRID_W

    @pl.when(gs == 0)
    def _():
        for hh in range(NA_HEADS_PER_STEP):
            for v in range(3):
                for qr in range(NA_QROWS):
                    q_rows = pl.ds(qr * GRID_W, GRID_W)
                    for kr2 in range(NA_KROWS // 2):
                        k_cols = pl.ds(kr2 * 128, 128)
                        bias_ref[hh, v, q_rows, k_cols] = (
                            pair_ref[hh, 2 * (kr2 + 2 * (2 - v)) - qr + 3] + mask_ref[v, q_rows, k_cols]) * LOG2E

    for gi in range(NA_GROUPS_PER_STEP):
        g = gs * NA_GROUPS_PER_STEP + gi
        q_rows = pl.ds(gi * tq, tq)
        key_row0 = jnp.clip(NA_QROWS * g - NA_ROWS // 2, 0, rows - NA_KROWS)
        tok0 = pl.multiple_of(key_row0 * GRID_W, GRID_W)
        variant = jnp.where(g == 0, 0, jnp.where(g == groups - 1, 2, 1))
        for hh in range(NA_HEADS_PER_STEP):
            cols = pl.ds(hh * HEAD_DIM, HEAD_DIM)
            s = lax.dot_general(q_ref[q_rows, cols], k_ref[pl.ds(tok0, nk), cols], (((1,), (1,)), ((), ())),
                                preferred_element_type=F32)
            s = s + bias_ref[hh, variant]
            m = jnp.max(s, axis=1, keepdims=True)
            p = jnp.exp2(s - m)
            l = jnp.sum(p, axis=1, keepdims=True)
            o = jnp.dot(p.astype(BF16), v_ref[pl.ds(tok0, nk), cols], preferred_element_type=F32)
            o_ref[q_rows, cols] = (o / l).astype(o_ref.dtype)


def neighbourhood_attention(qkv, bias_pairs, mask_tabs, seq, heads):
    rows = seq // GRID_W
    assert rows % (NA_QROWS * NA_GROUPS_PER_STEP) == 0 and rows >= NA_KROWS + NA_QROWS
    assert heads % NA_HEADS_PER_STEP == 0
    groups = rows // NA_QROWS
    tq = NA_QROWS * GRID_W * NA_GROUPS_PER_STEP
    hw = NA_HEADS_PER_STEP * HEAD_DIM
    hsteps = heads // NA_HEADS_PER_STEP
    return pl.pallas_call(
        functools.partial(_natten_kernel, groups=groups, rows=rows),
        grid=(hsteps, groups // NA_GROUPS_PER_STEP),
        in_specs=[
            pl.BlockSpec((tq, hw), lambda h, g: (g, h)),
            pl.BlockSpec((seq, hw), lambda h, g: (0, hsteps + h)),
            pl.BlockSpec((seq, hw), lambda h, g: (0, 2 * hsteps + h)),
            pl.BlockSpec((NA_HEADS_PER_STEP,) + bias_pairs.shape[1:], lambda h, g: (h, 0, 0, 0)),
            pl.BlockSpec(mask_tabs.shape, lambda h, g: (0, 0, 0)),
        ],
        out_specs=pl.BlockSpec((tq, hw), lambda h, g: (g, h)),
        out_shape=jax.ShapeDtypeStruct((seq, heads * HEAD_DIM), BF16),
        scratch_shapes=[pltpu.VMEM((NA_HEADS_PER_STEP, 3, NA_QROWS * GRID_W, NA_KROWS * GRID_W), F32)],
        compiler_params=_cparams(("arbitrary", "arbitrary")),
    )(qkv, qkv, qkv, bias_pairs, mask_tabs)


def kernel(x, t5_bias, ab_w_in, ab_w_out, s5_lam_re, s5_lam_im, s5_log_step, s5_b_re, s5_b_im, s5_c_re, s5_c_im, s5_d, s5_w_glu, c_w_qkv, c_w_out, c_rpb, norm_mix, norm_mlp, mlp_w1, mlp_w2, norm_final):
    batch, seq, d_model = x.shape
    depth = norm_mix.shape[0]
    c_heads = c_rpb.shape[1]
    t5_buckets = _t5_bucket_tables()
    na_mask = jnp.asarray(_na_mask_tables())
    attn_scale = LOG2E / math.sqrt(HEAD_DIM)
    w_in, w_out, w_glu = ab_w_in.astype(BF16), ab_w_out.astype(BF16), s5_w_glu.astype(BF16)
    w_qkv, w_cout = c_w_qkv.astype(BF16), c_w_out.astype(BF16)
    w1, w2 = mlp_w1.astype(BF16), mlp_w2.astype(BF16)
    outs = []
    for bi in range(batch):
        h = x[bi]
        for i in range(depth):
            j = i // 2
            if i % 2 == 0:
                proj = norm_matmul(h, norm_mix[i], w_in, j, F32)
                o_a = dilated_attention(proj, t5_bias, t5_buckets, seq)
                a_re, a_im, b_mat, c_mat = _s5_discretise(
                    s5_lam_re[j], s5_lam_im[j], s5_log_step[j], s5_b_re[j], s5_b_im[j], s5_c_re[j], s5_c_im[j])
                yf, yb = s5_scan(proj, 3 * A_WIDTH, a_re, a_im, b_mat, c_mat, seq)
                o_b = glu(yf, yb, proj, 3 * A_WIDTH, s5_d[j], w_glu, j)
                h = matmul_residual([o_a, o_b], w_out, j, h)
            else:
                qkv = norm_matmul(h, norm_mix[i], w_qkv, j, BF16, scaled_cols=c_heads * HEAD_DIM, scale=attn_scale)
                o = neighbourhood_attention(qkv, _na_bias_pairs(c_rpb[j]), na_mask, seq, c_heads)
                h = matmul_residual([o], w_cout, j, h)
            h = mlp(h, norm_mlp[i], w1, w2, i)
        outs.append(rmsnorm(h, norm_final))
    return jnp.stack(outs)
```

```python
import functools
import math

import numpy as np
import jax
import jax.numpy as jnp
from jax import lax
from jax.experimental import pallas as pl
from jax.experimental.pallas import tpu as pltpu

F32 = jnp.float32
BF16 = jnp.bfloat16

HEAD_DIM = 128
A_HEADS = 8
A_WIDTH = A_HEADS * HEAD_DIM
DILATED_BRANCHES = ((128, 1), (512, 4), (2048, 16))
A_QBLOCK = 128
A_HALF = 64
A_KBLOCK = A_QBLOCK + 2 * A_HALF
A_SUPER = 2048
B_GROUP = 16
B_STATE = 64
S5_SLAB = 256
S5_PAIRS = 8
S5_PITCH = 9
GRID_W = 64
NA_ROWS = 8
NA_COLS = 16
NA_QROWS = 4
NA_KROWS = 12
NA_EXT = NA_KROWS + 8
NA_HEADS_PER_STEP = 2
NA_GROUPS_PER_STEP = 4
T5_BUCKETS = 32
T5_MAX_DISTANCE = 1024
RMS_EPS = 1e-6
NEG_INF = -1e30
LOG2E = math.log2(math.e)
VMEM_LIMIT = 56 * 1024 * 1024


def _cparams(sem):
    return pltpu.CompilerParams(dimension_semantics=sem, vmem_limit_bytes=VMEM_LIMIT)


def _rms_rows(x, g):
    y = x * lax.rsqrt(jnp.mean(x * x, axis=-1, keepdims=True) + RMS_EPS)
    return y * g


def _norm_matmul_kernel(x_ref, g_ref, w_ref, o_ref, xn_ref, *, scaled_blocks, scale):
    j = pl.program_id(1)

    @pl.when(j == 0)
    def _():
        xn_ref[...] = _rms_rows(x_ref[...], g_ref[...]).astype(BF16)

    acc = jnp.dot(xn_ref[...], w_ref[...], preferred_element_type=F32)
    if scaled_blocks:
        acc = acc * jnp.where(j < scaled_blocks, scale, 1.0)
    o_ref[...] = acc.astype(o_ref.dtype)


def norm_matmul(x, g, w, layer, out_dtype, scaled_cols=0, scale=1.0, tm=1024, tn=1024):
    m, k = x.shape
    n = w.shape[2]
    assert scaled_cols % tn == 0
    return pl.pallas_call(
        functools.partial(_norm_matmul_kernel, scaled_blocks=scaled_cols // tn, scale=scale),
        grid=(m // tm, n // tn),
        in_specs=[
            pl.BlockSpec((tm, k), lambda i, j: (i, 0)),
            pl.BlockSpec((1, k), lambda i, j: (0, 0)),
            pl.BlockSpec((None, k, tn), lambda i, j: (layer, 0, j)),
        ],
        out_specs=pl.BlockSpec((tm, tn), lambda i, j: (i, j)),
        out_shape=jax.ShapeDtypeStruct((m, n), out_dtype),
        scratch_shapes=[pltpu.VMEM((tm, k), BF16)],
        compiler_params=_cparams(("arbitrary", "arbitrary")),
    )(x, g.reshape(1, k), w)


def _matmul_residual_kernel(*refs, widths):
    a_refs = refs[:len(widths)]
    w_ref, r_ref, o_ref = refs[len(widths):]
    acc = r_ref[...]
    row = 0
    for a_ref, kw in zip(a_refs, widths):
        acc = acc + jnp.dot(a_ref[...], w_ref[pl.ds(row, kw), :], preferred_element_type=F32)
        row += kw
    o_ref[...] = acc


def matmul_residual(a_list, w, layer, res, tm=512):
    m = res.shape[0]
    _, k, n = w.shape
    widths = tuple(a.shape[1] for a in a_list)
    assert sum(widths) == k
    return pl.pallas_call(
        functools.partial(_matmul_residual_kernel, widths=widths),
        grid=(m // tm,),
        in_specs=[pl.BlockSpec((tm, kw), lambda i: (i, 0)) for kw in widths] + [
            pl.BlockSpec((None, k, n), lambda i: (layer, 0, 0)),
            pl.BlockSpec((tm, n), lambda i: (i, 0)),
        ],
        out_specs=pl.BlockSpec((tm, n), lambda i: (i, 0)),
        out_shape=jax.ShapeDtypeStruct((m, n), F32),
        compiler_params=_cparams(("arbitrary",)),
    )(*a_list, w, res)


def _mlp_kernel(*refs, up_steps, tf, ncast):
    x_ref, xres_ref, g_ref, w1_ref, w2_ref = refs[:5]
    cast_in = refs[5:5 + ncast]
    o_ref = refs[5 + ncast]
    cast_out = refs[6 + ncast:6 + 2 * ncast]
    xn_ref, h_ref = refs[6 + 2 * ncast:]
    s = pl.program_id(1)

    @pl.when(s == 0)
    def _():
        xn_ref[...] = _rms_rows(x_ref[...], g_ref[...]).astype(BF16)

    @pl.when(s < up_steps)
    def _():
        h = jnp.dot(xn_ref[...], w1_ref[...], preferred_element_type=F32)
        h_ref[s] = jnp.square(jnp.maximum(h, 0.0)).astype(BF16)
        for src, dst in zip(cast_in, cast_out):
            dst[...] = src[...].astype(BF16)

    @pl.when(s >= up_steps)
    def _():
        acc = xres_ref[...]
        for c in range(up_steps):
            acc = acc + jnp.dot(h_ref[c], w2_ref[pl.ds(c * tf, tf), :], preferred_element_type=F32)
        o_ref[...] = acc


def mlp(x, g, w1, w2, layer, casts=(), tm=512, tf=1024, tn=512):
    m, d = x.shape
    dff = w1.shape[2]
    up_steps = dff // tf
    down_steps = d // tn
    cast_blocks = (m // tm) * up_steps

    def cast_index(i, s):
        return i * up_steps + jnp.minimum(s, up_steps - 1)

    cast_in_specs, cast_out_specs, cast_shapes = [], [], []
    for arr, cast_layer in casts:
        _, rows, cols = arr.shape
        assert rows % (16 * cast_blocks) == 0
        blk = (None, rows // cast_blocks, cols)
        cast_in_specs.append(pl.BlockSpec(blk, lambda i, s, cast_layer=cast_layer: (cast_layer, cast_index(i, s), 0)))
        cast_out_specs.append(pl.BlockSpec(blk, lambda i, s: (0, cast_index(i, s), 0)))
        cast_shapes.append(jax.ShapeDtypeStruct((1, rows, cols), BF16))
    out = pl.pallas_call(
        functools.partial(_mlp_kernel, up_steps=up_steps, tf=tf, ncast=len(casts)),
        grid=(m // tm, up_steps + down_steps),
        in_specs=[
            pl.BlockSpec((tm, d), lambda i, s: (i, 0)),
            pl.BlockSpec((tm, tn), lambda i, s: (i, jnp.maximum(s - up_steps, 0))),
            pl.BlockSpec((1, d), lambda i, s: (0, 0)),
            pl.BlockSpec((None, d, tf), lambda i, s: (layer, 0, jnp.minimum(s, up_steps - 1))),
            pl.BlockSpec((None, dff, tn), lambda i, s: (layer, 0, jnp.maximum(s - up_steps, 0))),
        ] + cast_in_specs,
        out_specs=[pl.BlockSpec((tm, tn), lambda i, s: (i, jnp.maximum(s - up_steps, 0)))] + cast_out_specs,
        out_shape=[jax.ShapeDtypeStruct((m, d), F32)] + cast_shapes,
        scratch_shapes=[pltpu.VMEM((tm, d), BF16), pltpu.VMEM((up_steps, tm, tf), BF16)],
        compiler_params=_cparams(("arbitrary", "arbitrary")),
    )(x, x, g.reshape(1, d), w1, w2, *[arr for arr, _ in casts])
    return out[0], list(out[1:])


def _rmsnorm_kernel(x_ref, g_ref, o_ref):
    o_ref[...] = _rms_rows(x_ref[...], g_ref[...])


def rmsnorm(x, g, tm=512):
    m, d = x.shape
    return pl.pallas_call(
        _rmsnorm_kernel,
        grid=(m // tm,),
        in_specs=[pl.BlockSpec((tm, d), lambda i: (i, 0)), pl.BlockSpec((1, d), lambda i: (0, 0))],
        out_specs=pl.BlockSpec((tm, d), lambda i: (i, 0)),
        out_shape=jax.ShapeDtypeStruct((m, d), F32),
        compiler_params=_cparams(("arbitrary",)),
    )(x, g.reshape(1, d))


def _t5_bucket(rel):
    half = T5_BUCKETS // 2
    max_exact = half // 2
    n = jnp.abs(rel)
    nf = jnp.maximum(n, 1).astype(F32)
    large = max_exact + (jnp.log(nf / max_exact) / math.log(T5_MAX_DISTANCE / max_exact)
                         * (half - max_exact)).astype(jnp.int32)
    large = jnp.minimum(large, half - 1)
    return jnp.where(rel > 0, half, 0) + jnp.where(n < max_exact, n, large)


def _t5_bucket_tables():
    tabs = []
    for _, dil in DILATED_BRANCHES:
        off = jnp.arange(A_KBLOCK)[None, :] - A_HALF - jnp.arange(A_QBLOCK)[:, None]
        tabs.append(jnp.where(jnp.abs(off) <= A_HALF, _t5_bucket(off * dil), T5_BUCKETS))
    return jnp.stack(tabs).astype(jnp.int32)


def _dilated_sections(seq):
    bases, sizes, row = [], [], 0
    for _, dil in DILATED_BRANCHES:
        sec = seq // dil + 2 * A_HALF
        bases.append(row)
        sizes.append(sec)
        row += dil * sec
    return bases, sizes, row


def _dilated_kernel(t5_ref, q_ref, k_ref, v_ref, bucket_ref, o_ref,
                    kd_ref, vd_ref, tmp_ref, bias_ref, m_ref, l_ref, n_ref, *, seq):
    h = pl.program_id(0)
    t = pl.program_id(1)
    scale = LOG2E / math.sqrt(HEAD_DIM)
    bases, sizes, _ = _dilated_sections(seq)

    @pl.when(t == 0)
    def _():
        for b in range(len(DILATED_BRANCHES)):
            bk = bucket_ref[b]
            acc = jnp.full((A_QBLOCK, A_KBLOCK), NEG_INF, F32)
            for kbkt in range(T5_BUCKETS):
                acc = jnp.where(bk == kbkt, t5_ref[kbkt * A_HEADS + h], acc)
            bias_ref[b] = acc * LOG2E
        zeros = jnp.zeros((A_HALF, HEAD_DIM), BF16)
        (_, d0), (_, d1), (_, d2) = DILATED_BRANCHES
        assert d0 == 1 and d2 % d1 == 0
        ratio = d2 // d1

        def put(dst, b, r, rows):
            o = bases[b] + r * sizes[b]
            dst[pl.ds(o, A_HALF), :] = zeros
            dst[pl.ds(o + A_HALF + rows.shape[0], A_HALF), :] = zeros
            dst[pl.ds(o + A_HALF, rows.shape[0]), :] = rows.astype(BF16)

        for src, dst in ((k_ref, kd_ref), (v_ref, vd_ref)):
            put(dst, 0, 0, src[...])
            for r1 in range(d1):
                rows1 = src[pl.ds(r1, seq // d1, stride=d1), :]
                put(dst, 1, r1, rows1)
                tmp_ref[...] = rows1
                for a in range(ratio):
                    put(dst, 2, a * d1 + r1, tmp_ref[pl.ds(a, seq // d2, stride=ratio), :])

    kk = lax.broadcasted_iota(jnp.int32, (1, A_KBLOCK), 1)
    for b, (_, dil) in enumerate(DILATED_BRANCHES):
        sub_len = seq // dil
        blocks_per_residue = A_SUPER // dil // A_QBLOCK

        def body(idx, carry, b=b, dil=dil, sub_len=sub_len, blocks_per_residue=blocks_per_residue):
            r = idx // blocks_per_residue
            n = idx % blocks_per_residue
            qs = r + n * (A_QBLOCK * dil)
            q_idx = pl.ds(qs, A_QBLOCK) if dil == 1 else pl.ds(qs, A_QBLOCK, stride=dil)
            blk = t * blocks_per_residue + n
            k_idx = pl.ds(pl.multiple_of(bases[b] + r * sizes[b] + blk * A_QBLOCK, A_QBLOCK), A_KBLOCK)
            qb = (q_ref[q_idx, :] * scale).astype(BF16)
            s = lax.dot_general(qb, kd_ref[k_idx, :], (((1,), (1,)), ((), ())), preferred_element_type=F32)
            key_l = blk * A_QBLOCK - A_HALF + kk
            edge = jnp.where((key_l >= 0) & (key_l < sub_len), 0.0, NEG_INF)
            s = s + bias_ref[b] + edge
            m = jnp.max(s, axis=1, keepdims=True)
            p = jnp.exp2(s - m)
            l = jnp.sum(p, axis=1, keepdims=True)
            num = jnp.dot(p.astype(BF16), vd_ref[k_idx, :], preferred_element_type=F32)
            m_ref[b, q_idx, :] = jnp.broadcast_to(m, (A_QBLOCK, HEAD_DIM))
            l_ref[b, q_idx, :] = jnp.broadcast_to(l, (A_QBLOCK, HEAD_DIM))
            n_ref[b, q_idx, :] = num
            return carry

        lax.fori_loop(0, A_SUPER // A_QBLOCK, body, 0, unroll=16)

    rows = 256
    for c in range(A_SUPER // rows):
        sl = pl.ds(c * rows, rows)
        m0, m1, m2 = m_ref[0, sl, :], m_ref[1, sl, :], m_ref[2, sl, :]
        mx = jnp.maximum(jnp.maximum(m0, m1), m2)
        w0, w1, w2 = jnp.exp2(m0 - mx), jnp.exp2(m1 - mx), jnp.exp2(m2 - mx)
        num = w0 * n_ref[0, sl, :] + w1 * n_ref[1, sl, :] + w2 * n_ref[2, sl, :]
        den = w0 * l_ref[0, sl, :] + w1 * l_ref[1, sl, :] + w2 * l_ref[2, sl, :]
        o_ref[sl, :] = (num / den).astype(o_ref.dtype)


def dilated_attention(proj, t5_table, bucket_tabs, seq):
    assert seq % A_SUPER == 0
    nb = len(DILATED_BRANCHES)
    total_rows = _dilated_sections(seq)[2]
    return pl.pallas_call(
        functools.partial(_dilated_kernel, seq=seq),
        grid=(A_HEADS, seq // A_SUPER),
        in_specs=[
            pl.BlockSpec(memory_space=pltpu.SMEM),
            pl.BlockSpec((A_SUPER, HEAD_DIM), lambda h, t: (t, h)),
            pl.BlockSpec((seq, HEAD_DIM), lambda h, t: (0, A_HEADS + h)),
            pl.BlockSpec((seq, HEAD_DIM), lambda h, t: (0, 2 * A_HEADS + h)),
            pl.BlockSpec((nb, A_QBLOCK, A_KBLOCK), lambda h, t: (0, 0, 0)),
        ],
        out_specs=pl.BlockSpec((A_SUPER, HEAD_DIM), lambda h, t: (t, h)),
        out_shape=jax.ShapeDtypeStruct((seq, A_WIDTH), BF16),
        scratch_shapes=[
            pltpu.VMEM((total_rows, HEAD_DIM), BF16),
            pltpu.VMEM((total_rows, HEAD_DIM), BF16),
            pltpu.VMEM((seq // DILATED_BRANCHES[1][1], HEAD_DIM), F32),
            pltpu.VMEM((nb, A_QBLOCK, A_KBLOCK), F32),
            pltpu.VMEM((nb, A_SUPER, HEAD_DIM), F32),
            pltpu.VMEM((nb, A_SUPER, HEAD_DIM), F32),
            pltpu.VMEM((nb, A_SUPER, HEAD_DIM), F32),
        ],
        compiler_params=_cparams(("arbitrary", "arbitrary")),
    )(t5_table.astype(F32).reshape(-1), proj, proj, proj, bucket_tabs)


def _s5_discretise(lam_re, lam_im, log_step, b_re, b_im, c_re, c_im):
    g, p = lam_re.shape[1:]
    slabs = g // (2 * S5_PAIRS)
    step = jnp.exp(log_step.astype(F32))[..., None]
    lr = jnp.minimum(lam_re.astype(F32), -1e-4)
    li = lam_im.astype(F32)
    mag = jnp.exp(lr * step)
    ab_re = mag * jnp.cos(li * step)
    ab_im = mag * jnp.sin(li * step)
    den = lr * lr + li * li
    zr = ((ab_re - 1.0) * lr + ab_im * li) / den
    zi = (ab_im * lr - (ab_re - 1.0) * li) / den
    br = b_re.astype(F32)[None]
    bi = b_im.astype(F32)[None]
    bb_re = zr[..., None] * br - zi[..., None] * bi
    bb_im = zr[..., None] * bi + zi[..., None] * br

    a_re = ab_re.reshape(2, slabs, S5_PAIRS, 2 * p)
    a_im = ab_im.reshape(2, slabs, S5_PAIRS, 2 * p)

    eye2 = jnp.eye(2, dtype=F32)

    def pack_b(bb):
        bb = bb.reshape(2, slabs, S5_PAIRS, 2, p, B_GROUP)
        return jnp.einsum('dsjepc,ef->dsjecfp', bb, eye2).reshape(2, slabs, S5_SLAB, 2 * p)

    def pack_c(cc):
        cc = cc.astype(F32).reshape(2, slabs, S5_PAIRS, 2, B_GROUP, p)
        return jnp.einsum('dsjecp,ef->dsfpjec', cc, eye2).reshape(2, slabs, 2 * p, S5_SLAB)

    b_mat = jnp.concatenate([pack_b(bb_re), pack_b(bb_im)], axis=-1)
    c_mat = jnp.concatenate([pack_c(c_re), -pack_c(c_im)], axis=-2)
    return a_re, a_im, b_mat.astype(BF16), c_mat.astype(BF16)


def _s5_kernel(uf_ref, ub_ref, b_ref, c_ref, are_ref, aim_ref, yf_ref, yb_ref,
               xr_ref, xi_ref, bur_ref, bui_ref, *, tm, slabs):
    c = pl.program_id(0)
    nstate = 2 * B_STATE
    u_refs = (uf_ref, ub_ref)
    y_refs = (yf_ref, yb_ref)

    @pl.when(c == 0)
    def _():
        xr_ref[...] = jnp.zeros_like(xr_ref)
        xi_ref[...] = jnp.zeros_like(xi_ref)

    lane_pair = lax.broadcasted_iota(jnp.int32, (1, S5_SLAB), 1) // (2 * B_GROUP)

    def pair_rows(j):
        return pl.ds(j, tm, stride=S5_PITCH)

    for d in range(2):
        for s in range(slabs):
            u16 = u_refs[d][:, s * S5_SLAB:(s + 1) * S5_SLAB].astype(BF16)
            lhs = jnp.concatenate([jnp.where(lane_pair == j, u16, jnp.zeros_like(u16))
                                   for j in range(S5_PAIRS)], axis=0)
            bu = jnp.dot(lhs, b_ref[d, s], preferred_element_type=F32)
            for j in range(S5_PAIRS):
                bur_ref[d, s, pair_rows(j), :] = bu[j * tm:(j + 1) * tm, :nstate]
                bui_ref[d, s, pair_rows(j), :] = bu[j * tm:(j + 1) * tm, nstate:]

    ar = [[are_ref[d, s] for s in range(slabs)] for d in range(2)]
    ai = [[aim_ref[d, s] for s in range(slabs)] for d in range(2)]

    def step(i, carry):
        new = []
        for d in range(2):
            t = i if d == 0 else tm - 1 - i
            start = t * S5_PITCH
            rows = pl.ds(pl.multiple_of(start, 8) if S5_PITCH % 8 == 0 else start, S5_PAIRS)
            for s in range(slabs):
                k = 2 * (d * slabs + s)
                xr, xi = carry[k], carry[k + 1]
                nxr = ar[d][s] * xr - ai[d][s] * xi + bur_ref[d, s, rows, :]
                nxi = ar[d][s] * xi + ai[d][s] * xr + bui_ref[d, s, rows, :]
                bur_ref[d, s, rows, :] = nxr
                bui_ref[d, s, rows, :] = nxi
                new += [nxr, nxi]
        return tuple(new)

    init = []
    for d in range(2):
        for s in range(slabs):
            init += [xr_ref[d, s], xi_ref[d, s]]
    final = tuple(init)
    for i in range(tm):
        final = step(i, final)
    for d in range(2):
        for s in range(slabs):
            k = 2 * (d * slabs + s)
            xr_ref[d, s] = final[k]
            xi_ref[d, s] = final[k + 1]

    for d in range(2):
        for s in range(slabs):
            xcat = jnp.concatenate(
                [jnp.concatenate([bur_ref[d, s, pair_rows(j), :], bui_ref[d, s, pair_rows(j), :]],
                                 axis=1).astype(BF16) for j in range(S5_PAIRS)], axis=0)
            r = jnp.dot(xcat, c_ref[d, s], preferred_element_type=F32)
            y = r[:tm]
            for j in range(1, S5_PAIRS):
                y = jnp.where(lane_pair == j, r[j * tm:(j + 1) * tm], y)
            y_refs[d][:, s * S5_SLAB:(s + 1) * S5_SLAB] = y


def s5_scan(proj, u_col0, a_re, a_im, b_mat, c_mat, seq, tm=256):
    slabs = a_re.shape[1]
    width = slabs * S5_SLAB
    nchunks = seq // tm
    assert u_col0 % width == 0
    ucol = u_col0 // width
    wspec = pl.BlockSpec((2, slabs, S5_SLAB, S5_SLAB), lambda c: (0, 0, 0, 0))
    aspec = pl.BlockSpec((2, slabs, S5_PAIRS, 2 * B_STATE), lambda c: (0, 0, 0, 0))
    state = pltpu.VMEM((2, slabs, S5_PAIRS, 2 * B_STATE), F32)
    drive = pltpu.VMEM((2, slabs, tm * S5_PITCH, 2 * B_STATE), F32)
    return pl.pallas_call(
        functools.partial(_s5_kernel, tm=tm, slabs=slabs),
        grid=(nchunks,),
        in_specs=[pl.BlockSpec((tm, width), lambda c: (c, ucol)),
                  pl.BlockSpec((tm, width), lambda c: (nchunks - 1 - c, ucol)),
                  wspec, wspec, aspec, aspec],
        out_specs=[pl.BlockSpec((tm, width), lambda c: (c, 0)),
                   pl.BlockSpec((tm, width), lambda c: (nchunks - 1 - c, 0))],
        out_shape=[jax.ShapeDtypeStruct((seq, width), F32), jax.ShapeDtypeStruct((seq, width), F32)],
        scratch_shapes=[state, state, drive, drive],
        compiler_params=_cparams(("arbitrary",)),
    )(proj, proj, b_mat, c_mat, a_re, a_im)


def _glu_kernel(yf_ref, yb_ref, u_ref, dsk_ref, w_ref, o_ref):
    y = jax.nn.gelu(yf_ref[...] + yb_ref[...] + dsk_ref[...] * u_ref[...])
    z = jnp.dot(y.astype(BF16), w_ref[...], preferred_element_type=F32)
    o_ref[...] = (y * jax.nn.sigmoid(z)).astype(o_ref.dtype)


def glu(yf, yb, proj, u_col0, d_skip, w, layer, tm=512):
    m, k = yf.shape
    return pl.pallas_call(
        _glu_kernel,
        grid=(m // tm,),
        in_specs=[
            pl.BlockSpec((tm, k), lambda i: (i, 0)),
            pl.BlockSpec((tm, k), lambda i: (i, 0)),
            pl.BlockSpec((tm, k), lambda i: (i, u_col0 // k)),
            pl.BlockSpec((1, k), lambda i: (0, 0)),
            pl.BlockSpec((None, k, k), lambda i: (layer, 0, 0)),
        ],
        out_specs=pl.BlockSpec((tm, k), lambda i: (i, 0)),
        out_shape=jax.ShapeDtypeStruct((m, k), BF16),
        compiler_params=_cparams(("arbitrary",)),
    )(yf, yb, proj, d_skip.reshape(1, k), w)


def _na_bias_pairs(rpb):
    c = np.arange(GRID_W)
    col_idx = np.clip(c[None, :] - c[:, None] + NA_COLS - 1, 0, 2 * NA_COLS - 2)
    onehot = (col_idx[None] == np.arange(2 * NA_COLS - 1)[:, None, None]).astype(np.float32)
    r2 = jnp.einsum('hdk,kqc->hdqc', rpb.astype(F32), jnp.asarray(onehot), precision=lax.Precision.HIGHEST)
    r2 = jnp.pad(r2, ((0, 0), (4, 4), (0, 0), (0, 0)))
    return jnp.concatenate([r2[:, :-1], r2[:, 1:]], axis=-1)


def _na_mask_tables():
    qr = np.arange(NA_QROWS)[:, None, None, None]
    cq = np.arange(GRID_W)[None, :, None, None]
    kr = np.arange(NA_KROWS)[None, None, :, None]
    ck = np.arange(GRID_W)[None, None, None, :]
    cs = np.clip(cq - NA_COLS // 2, 0, GRID_W - NA_COLS)
    col_ok = (ck >= cs) & (ck < cs + NA_COLS)
    row_ok = [(kr < NA_ROWS) & (qr >= 0),
              (kr >= qr) & (kr < qr + NA_ROWS),
              (kr >= NA_KROWS - NA_ROWS) & (qr >= 0)]
    out = [np.where(r & col_ok, 0.0, NEG_INF).reshape(NA_QROWS * GRID_W, NA_KROWS * GRID_W) for r in row_ok]
    return np.stack(out).astype(np.float32)


def _natten_kernel(q_ref, k_ref, v_ref, pair_ref, mask_ref, o_ref, bias_ref, *, groups, rows):
    gs = pl.program_id(1)
    nk = NA_KROWS * GRID_W
    tq = NA_QROWS * GRID_W

    @pl.when(gs == 0)
    def _():
        for hh in range(NA_HEADS_PER_STEP):
            for v in range(3):
                for qr in range(NA_QROWS):
                    q_rows = pl.ds(qr * GRID_W, GRID_W)
                    for kr2 in range(NA_KROWS // 2):
                        k_cols = pl.ds(kr2 * 128, 128)
                        bias_ref[hh, v, q_rows, k_cols] = (
                            pair_ref[hh, 2 * (kr2 + 2 * (2 - v)) - qr + 3] + mask_ref[v, q_rows, k_cols]) * LOG2E

    for gi in range(NA_GROUPS_PER_STEP):
        g = gs * NA_GROUPS_PER_STEP + gi
        q_rows = pl.ds(gi * tq, tq)
        key_row0 = jnp.clip(NA_QROWS * g - NA_ROWS // 2, 0, rows - NA_KROWS)
        tok0 = pl.multiple_of(key_row0 * GRID_W, GRID_W)
        variant = jnp.where(g == 0, 0, jnp.where(g == groups - 1, 2, 1))
        for hh in range(NA_HEADS_PER_STEP):
            cols = pl.ds(hh * HEAD_DIM, HEAD_DIM)
            s = lax.dot_general(q_ref[q_rows, cols], k_ref[pl.ds(tok0, nk), cols], (((1,), (1,)), ((), ())),
                                preferred_element_type=F32)
            s = s + bias_ref[hh, variant]
            m = jnp.max(s, axis=1, keepdims=True)
            p = jnp.exp2(s - m)
            l = jnp.sum(p, axis=1, keepdims=True)
            o = jnp.dot(p.astype(BF16), v_ref[pl.ds(tok0, nk), cols], preferred_element_type=F32)
            o_ref[q_rows, cols] = (o / l).astype(o_ref.dtype)


def neighbourhood_attention(qkv, bias_pairs, mask_tabs, seq, heads):
    rows = seq // GRID_W
    assert rows % (NA_QROWS * NA_GROUPS_PER_STEP) == 0 and rows >= NA_KROWS + NA_QROWS
    assert heads % NA_HEADS_PER_STEP == 0
    groups = rows // NA_QROWS
    tq = NA_QROWS * GRID_W * NA_GROUPS_PER_STEP
    hw = NA_HEADS_PER_STEP * HEAD_DIM
    hsteps = heads // NA_HEADS_PER_STEP
    return pl.pallas_call(
        functools.partial(_natten_kernel, groups=groups, rows=rows),
        grid=(hsteps, groups // NA_GROUPS_PER_STEP),
        in_specs=[
            pl.BlockSpec((tq, hw), lambda h, g: (g, h)),
            pl.BlockSpec((seq, hw), lambda h, g: (0, hsteps + h)),
            pl.BlockSpec((seq, hw), lambda h, g: (0, 2 * hsteps + h)),
            pl.BlockSpec((NA_HEADS_PER_STEP,) + bias_pairs.shape[1:], lambda h, g: (h, 0, 0, 0)),
            pl.BlockSpec(mask_tabs.shape, lambda h, g: (0, 0, 0)),
        ],
        out_specs=pl.BlockSpec((tq, hw), lambda h, g: (g, h)),
        out_shape=jax.ShapeDtypeStruct((seq, heads * HEAD_DIM), BF16),
        scratch_shapes=[pltpu.VMEM((NA_HEADS_PER_STEP, 3, NA_QROWS * GRID_W, NA_KROWS * GRID_W), F32)],
        compiler_params=_cparams(("arbitrary", "arbitrary")),
    )(qkv, qkv, qkv, bias_pairs, mask_tabs)


def kernel(x, t5_bias, ab_w_in, ab_w_out, s5_lam_re, s5_lam_im, s5_log_step, s5_b_re, s5_b_im, s5_c_re, s5_c_im, s5_d, s5_w_glu, c_w_qkv, c_w_out, c_rpb, norm_mix, norm_mlp, mlp_w1, mlp_w2, norm_final):
    batch, seq, d_model = x.shape
    depth = norm_mix.shape[0]
    c_heads = c_rpb.shape[1]
    t5_buckets = _t5_bucket_tables()
    na_mask = jnp.asarray(_na_mask_tables())
    attn_scale = LOG2E / math.sqrt(HEAD_DIM)
    w_glu = s5_w_glu.astype(BF16)

    def mixer_weights(i):
        return ((ab_w_in, ab_w_out) if i % 2 == 0 else (c_w_qkv, c_w_out)), i // 2

    (mix_in0, mix_out0), _ = mixer_weights(0)
    bf = {0: [w[:1].astype(BF16) for w in (mlp_w1, mlp_w2, mix_in0, mix_out0)]}
    outs = []
    for bi in range(batch):
        h = x[bi]
        for i in range(depth):
            j = i // 2
            w1, w2, w_mix_in, w_mix_out = bf[i]
            if i % 2 == 0:
                proj = norm_matmul(h, norm_mix[i], w_mix_in, 0, F32)
                o_a = dilated_attention(proj, t5_bias, t5_buckets, seq)
                a_re, a_im, b_mat, c_mat = _s5_discretise(
                    s5_lam_re[j], s5_lam_im[j], s5_log_step[j], s5_b_re[j], s5_b_im[j], s5_c_re[j], s5_c_im[j])
                yf, yb = s5_scan(proj, 3 * A_WIDTH, a_re, a_im, b_mat, c_mat, seq)
                o_b = glu(yf, yb, proj, 3 * A_WIDTH, s5_d[j], w_glu, j)
                h = matmul_residual([o_a, o_b], w_mix_out, 0, h)
            else:
                qkv = norm_matmul(h, norm_mix[i], w_mix_in, 0, BF16, scaled_cols=c_heads * HEAD_DIM, scale=attn_scale)
                o = neighbourhood_attention(qkv, _na_bias_pairs(c_rpb[j]), na_mask, seq, c_heads)
                h = matmul_residual([o], w_mix_out, 0, h)
            casts = []
            if i + 1 < depth and i + 1 not in bf:
                (nxt_in, nxt_out), jn = mixer_weights(i + 1)
                casts = [(mlp_w1, i + 1), (mlp_w2, i + 1), (nxt_in, jn), (nxt_out, jn)]
            h, cast_out = mlp(h, norm_mlp[i], w1, w2, 0, casts)
            if casts:
                bf[i + 1] = cast_out
        outs.append(rmsnorm(h, norm_final))
    return jnp.stack(outs)
```

```python
import functools
import math

import numpy as np
import jax
import jax.numpy as jnp
from jax import lax
from jax.experimental import pallas as pl
from jax.experimental.pallas import tpu as pltpu

F32 = jnp.float32
BF16 = jnp.bfloat16

HEAD_DIM = 128
A_HEADS = 8
A_WIDTH = A_HEADS * HEAD_DIM
DILATED_BRANCHES = ((128, 1), (512, 4), (2048, 16))
A_QBLOCK = 128
A_HALF = 64
A_KBLOCK = A_QBLOCK + 2 * A_HALF
A_SUPER = 2048
B_GROUP = 16
B_STATE = 64
S5_SLAB = 256
S5_PAIRS = 8
S5_PITCH = 9
GRID_W = 64
NA_ROWS = 8
NA_COLS = 16
NA_QROWS = 4
NA_KROWS = 12
NA_EXT = NA_KROWS + 8
NA_HEADS_PER_STEP = 2
NA_GROUPS_PER_STEP = 4
T5_BUCKETS = 32
T5_MAX_DISTANCE = 1024
RMS_EPS = 1e-6
NEG_INF = -1e30
LOG2E = math.log2(math.e)
VMEM_LIMIT = 56 * 1024 * 1024


def _cparams(sem):
    return pltpu.CompilerParams(dimension_semantics=sem, vmem_limit_bytes=VMEM_LIMIT)


def _rms_rows(x, g):
    y = x * lax.rsqrt(jnp.mean(x * x, axis=-1, keepdims=True) + RMS_EPS)
    return y * g


def _with_casts(body, n_in, n_out, ncast):
    def wrapped(*refs):
        ins = refs[:n_in]
        cast_in = refs[n_in:n_in + ncast]
        outs = refs[n_in + ncast:n_in + ncast + n_out]
        cast_out = refs[n_in + ncast + n_out:n_in + 2 * ncast + n_out]
        for src, dst in zip(cast_in, cast_out):
            dst[...] = src[...].astype(BF16)
        body(*ins, *outs, *refs[n_in + 2 * ncast + n_out:])
    return wrapped


def _cast_plumbing(casts, nblocks, block_id):
    in_specs, out_specs, shapes = [], [], []
    for arr, layer in casts:
        _, rows, cols = arr.shape
        assert rows % (16 * nblocks) == 0
        blk = (None, rows // nblocks, cols)
        in_specs.append(pl.BlockSpec(blk, lambda *g, layer=layer: (layer, block_id(*g), 0)))
        out_specs.append(pl.BlockSpec(blk, lambda *g: (0, block_id(*g), 0)))
        shapes.append(jax.ShapeDtypeStruct((1, rows, cols), BF16))
    return in_specs, out_specs, shapes


def _norm_matmul_kernel(x_ref, g_ref, w_ref, o_ref, xn_ref, *, scaled_blocks, scale):
    j = pl.program_id(1)

    @pl.when(j == 0)
    def _():
        xn_ref[...] = _rms_rows(x_ref[...], g_ref[...]).astype(BF16)

    acc = jnp.dot(xn_ref[...], w_ref[...], preferred_element_type=F32)
    if scaled_blocks:
        acc = acc * jnp.where(j < scaled_blocks, scale, 1.0)
    o_ref[...] = acc.astype(o_ref.dtype)


def norm_matmul(x, g, w, layer, out_dtype, scaled_cols=0, scale=1.0, tm=1024, tn=1024):
    m, k = x.shape
    n = w.shape[2]
    assert scaled_cols % tn == 0
    return pl.pallas_call(
        functools.partial(_norm_matmul_kernel, scaled_blocks=scaled_cols // tn, scale=scale),
        grid=(m // tm, n // tn),
        in_specs=[
            pl.BlockSpec((tm, k), lambda i, j: (i, 0)),
            pl.BlockSpec((1, k), lambda i, j: (0, 0)),
            pl.BlockSpec((None, k, tn), lambda i, j: (layer, 0, j)),
        ],
        out_specs=pl.BlockSpec((tm, tn), lambda i, j: (i, j)),
        out_shape=jax.ShapeDtypeStruct((m, n), out_dtype),
        scratch_shapes=[pltpu.VMEM((tm, k), BF16)],
        compiler_params=_cparams(("arbitrary", "arbitrary")),
    )(x, g.reshape(1, k), w)


def _matmul_residual_kernel(*refs, widths):
    a_refs = refs[:len(widths)]
    w_ref, r_ref, o_ref = refs[len(widths):]
    acc = r_ref[...]
    row = 0
    for a_ref, kw in zip(a_refs, widths):
        acc = acc + jnp.dot(a_ref[...], w_ref[pl.ds(row, kw), :], preferred_element_type=F32)
        row += kw
    o_ref[...] = acc


def matmul_residual(a_list, w, layer, res, tm=512):
    m = res.shape[0]
    _, k, n = w.shape
    widths = tuple(a.shape[1] for a in a_list)
    assert sum(widths) == k
    return pl.pallas_call(
        functools.partial(_matmul_residual_kernel, widths=widths),
        grid=(m // tm,),
        in_specs=[pl.BlockSpec((tm, kw), lambda i: (i, 0)) for kw in widths] + [
            pl.BlockSpec((None, k, n), lambda i: (layer, 0, 0)),
            pl.BlockSpec((tm, n), lambda i: (i, 0)),
        ],
        out_specs=pl.BlockSpec((tm, n), lambda i: (i, 0)),
        out_shape=jax.ShapeDtypeStruct((m, n), F32),
        compiler_params=_cparams(("arbitrary",)),
    )(*a_list, w, res)


def _mlp_kernel(x_ref, xres_ref, g_ref, w1_ref, w2_ref, o_ref, xn_ref, h_ref, *, up_steps, tf):
    s = pl.program_id(1)

    @pl.when(s == 0)
    def _():
        xn_ref[...] = _rms_rows(x_ref[...], g_ref[...]).astype(BF16)

    @pl.when(s < up_steps)
    def _():
        h = jnp.dot(xn_ref[...], w1_ref[...], preferred_element_type=F32)
        h_ref[s] = jnp.square(jnp.maximum(h, 0.0)).astype(BF16)

    @pl.when(s >= up_steps)
    def _():
        acc = xres_ref[...]
        for c in range(up_steps):
            acc = acc + jnp.dot(h_ref[c], w2_ref[pl.ds(c * tf, tf), :], preferred_element_type=F32)
        o_ref[...] = acc


def mlp(x, g, w1, w2, layer, casts=(), tm=512, tf=1024, tn=512):
    m, d = x.shape
    dff = w1.shape[2]
    up_steps = dff // tf
    down_steps = d // tn
    cast_in_specs, cast_out_specs, cast_shapes = _cast_plumbing(
        casts, (m // tm) * up_steps, lambda i, s: i * up_steps + jnp.minimum(s, up_steps - 1))
    out = pl.pallas_call(
        _with_casts(functools.partial(_mlp_kernel, up_steps=up_steps, tf=tf), 5, 1, len(casts)),
        grid=(m // tm, up_steps + down_steps),
        in_specs=[
            pl.BlockSpec((tm, d), lambda i, s: (i, 0)),
            pl.BlockSpec((tm, tn), lambda i, s: (i, jnp.maximum(s - up_steps, 0))),
            pl.BlockSpec((1, d), lambda i, s: (0, 0)),
            pl.BlockSpec((None, d, tf), lambda i, s: (layer, 0, jnp.minimum(s, up_steps - 1))),
            pl.BlockSpec((None, dff, tn), lambda i, s: (layer, 0, jnp.maximum(s - up_steps, 0))),
        ] + cast_in_specs,
        out_specs=[pl.BlockSpec((tm, tn), lambda i, s: (i, jnp.maximum(s - up_steps, 0)))] + cast_out_specs,
        out_shape=[jax.ShapeDtypeStruct((m, d), F32)] + cast_shapes,
        scratch_shapes=[pltpu.VMEM((tm, d), BF16), pltpu.VMEM((up_steps, tm, tf), BF16)],
        compiler_params=_cparams(("arbitrary", "arbitrary")),
    )(x, x, g.reshape(1, d), w1, w2, *[arr for arr, _ in casts])
    return out[0], list(out[1:])


def _rmsnorm_kernel(x_ref, g_ref, o_ref):
    o_ref[...] = _rms_rows(x_ref[...], g_ref[...])


def rmsnorm(x, g, tm=512):
    m, d = x.shape
    return pl.pallas_call(
        _rmsnorm_kernel,
        grid=(m // tm,),
        in_specs=[pl.BlockSpec((tm, d), lambda i: (i, 0)), pl.BlockSpec((1, d), lambda i: (0, 0))],
        out_specs=pl.BlockSpec((tm, d), lambda i: (i, 0)),
        out_shape=jax.ShapeDtypeStruct((m, d), F32),
        compiler_params=_cparams(("arbitrary",)),
    )(x, g.reshape(1, d))


def _t5_bucket(rel):
    half = T5_BUCKETS // 2
    max_exact = half // 2
    n = jnp.abs(rel)
    nf = jnp.maximum(n, 1).astype(F32)
    large = max_exact + (jnp.log(nf / max_exact) / math.log(T5_MAX_DISTANCE / max_exact)
                         * (half - max_exact)).astype(jnp.int32)
    large = jnp.minimum(large, half - 1)
    return jnp.where(rel > 0, half, 0) + jnp.where(n < max_exact, n, large)


def _t5_bucket_tables():
    tabs = []
    for _, dil in DILATED_BRANCHES:
        off = jnp.arange(A_KBLOCK)[None, :] - A_HALF - jnp.arange(A_QBLOCK)[:, None]
        tabs.append(jnp.where(jnp.abs(off) <= A_HALF, _t5_bucket(off * dil), T5_BUCKETS))
    return jnp.stack(tabs).astype(jnp.int32)


def _dilated_sections(seq):
    bases, sizes, row = [], [], 0
    for _, dil in DILATED_BRANCHES:
        sec = seq // dil + 2 * A_HALF
        bases.append(row)
        sizes.append(sec)
        row += dil * sec
    return bases, sizes, row


def _dilated_kernel(t5_ref, q_ref, k_ref, v_ref, bucket_ref, o_ref,
                    kd_ref, vd_ref, tmp_ref, bias_ref, m_ref, l_ref, n_ref, *, seq):
    h = pl.program_id(0)
    t = pl.program_id(1)
    scale = LOG2E / math.sqrt(HEAD_DIM)
    bases, sizes, _ = _dilated_sections(seq)

    @pl.when(t == 0)
    def _():
        for b in range(len(DILATED_BRANCHES)):
            bk = bucket_ref[b]
            acc = jnp.full((A_QBLOCK, A_KBLOCK), NEG_INF, F32)
            for kbkt in range(T5_BUCKETS):
                acc = jnp.where(bk == kbkt, t5_ref[kbkt * A_HEADS + h], acc)
            bias_ref[b] = acc * LOG2E
        zeros = jnp.zeros((A_HALF, HEAD_DIM), BF16)
        (_, d0), (_, d1), (_, d2) = DILATED_BRANCHES
        assert d0 == 1 and d2 % d1 == 0
        ratio = d2 // d1

        def put(dst, b, r, rows):
            o = bases[b] + r * sizes[b]
            dst[pl.ds(o, A_HALF), :] = zeros
            dst[pl.ds(o + A_HALF + rows.shape[0], A_HALF), :] = zeros
            dst[pl.ds(o + A_HALF, rows.shape[0]), :] = rows.astype(BF16)

        for src, dst in ((k_ref, kd_ref), (v_ref, vd_ref)):
            put(dst, 0, 0, src[...])
            for r1 in range(d1):
                rows1 = src[pl.ds(r1, seq // d1, stride=d1), :]
                put(dst, 1, r1, rows1)
                tmp_ref[...] = rows1
                for a in range(ratio):
                    put(dst, 2, a * d1 + r1, tmp_ref[pl.ds(a, seq // d2, stride=ratio), :])

    kk = lax.broadcasted_iota(jnp.int32, (1, A_KBLOCK), 1)
    for b, (_, dil) in enumerate(DILATED_BRANCHES):
        sub_len = seq // dil
        blocks_per_residue = A_SUPER // dil // A_QBLOCK

        def body(idx, carry, b=b, dil=dil, sub_len=sub_len, blocks_per_residue=blocks_per_residue):
            r = idx // blocks_per_residue
            n = idx % blocks_per_residue
            qs = r + n * (A_QBLOCK * dil)
            q_idx = pl.ds(qs, A_QBLOCK) if dil == 1 else pl.ds(qs, A_QBLOCK, stride=dil)
            blk = t * blocks_per_residue + n
            k_idx = pl.ds(pl.multiple_of(bases[b] + r * sizes[b] + blk * A_QBLOCK, A_QBLOCK), A_KBLOCK)
            qb = (q_ref[q_idx, :] * scale).astype(BF16)
            s = lax.dot_general(qb, kd_ref[k_idx, :], (((1,), (1,)), ((), ())), preferred_element_type=F32)
            key_l = blk * A_QBLOCK - A_HALF + kk
            edge = jnp.where((key_l >= 0) & (key_l < sub_len), 0.0, NEG_INF)
            s = s + bias_ref[b] + edge
            m = jnp.max(s, axis=1, keepdims=True)
            p = jnp.exp2(s - m)
            l = jnp.sum(p, axis=1, keepdims=True)
            num = jnp.dot(p.astype(BF16), vd_ref[k_idx, :], preferred_element_type=F32)
            m_ref[b, q_idx, :] = jnp.broadcast_to(m, (A_QBLOCK, HEAD_DIM))
            l_ref[b, q_idx, :] = jnp.broadcast_to(l, (A_QBLOCK, HEAD_DIM))
            n_ref[b, q_idx, :] = num
            return carry

        lax.fori_loop(0, A_SUPER // A_QBLOCK, body, 0, unroll=16)

    rows = 256
    for c in range(A_SUPER // rows):
        sl = pl.ds(c * rows, rows)
        m0, m1, m2 = m_ref[0, sl, :], m_ref[1, sl, :], m_ref[2, sl, :]
        mx = jnp.maximum(jnp.maximum(m0, m1), m2)
        w0, w1, w2 = jnp.exp2(m0 - mx), jnp.exp2(m1 - mx), jnp.exp2(m2 - mx)
        num = w0 * n_ref[0, sl, :] + w1 * n_ref[1, sl, :] + w2 * n_ref[2, sl, :]
        den = w0 * l_ref[0, sl, :] + w1 * l_ref[1, sl, :] + w2 * l_ref[2, sl, :]
        o_ref[sl, :] = (num / den).astype(o_ref.dtype)


def dilated_attention(proj, t5_table, bucket_tabs, seq, casts=()):
    assert seq % A_SUPER == 0
    nb = len(DILATED_BRANCHES)
    total_rows = _dilated_sections(seq)[2]
    tiles = seq // A_SUPER
    cast_in_specs, cast_out_specs, cast_shapes = _cast_plumbing(casts, A_HEADS * tiles, lambda h, t: h * tiles + t)
    out = pl.pallas_call(
        _with_casts(functools.partial(_dilated_kernel, seq=seq), 5, 1, len(casts)),
        grid=(A_HEADS, tiles),
        in_specs=[
            pl.BlockSpec(memory_space=pltpu.SMEM),
            pl.BlockSpec((A_SUPER, HEAD_DIM), lambda h, t: (t, h)),
            pl.BlockSpec((seq, HEAD_DIM), lambda h, t: (0, A_HEADS + h)),
            pl.BlockSpec((seq, HEAD_DIM), lambda h, t: (0, 2 * A_HEADS + h)),
            pl.BlockSpec((nb, A_QBLOCK, A_KBLOCK), lambda h, t: (0, 0, 0)),
        ] + cast_in_specs,
        out_specs=[pl.BlockSpec((A_SUPER, HEAD_DIM), lambda h, t: (t, h))] + cast_out_specs,
        out_shape=[jax.ShapeDtypeStruct((seq, A_WIDTH), BF16)] + cast_shapes,
        scratch_shapes=[
            pltpu.VMEM((total_rows, HEAD_DIM), BF16),
            pltpu.VMEM((total_rows, HEAD_DIM), BF16),
            pltpu.VMEM((seq // DILATED_BRANCHES[1][1], HEAD_DIM), F32),
            pltpu.VMEM((nb, A_QBLOCK, A_KBLOCK), F32),
            pltpu.VMEM((nb, A_SUPER, HEAD_DIM), F32),
            pltpu.VMEM((nb, A_SUPER, HEAD_DIM), F32),
            pltpu.VMEM((nb, A_SUPER, HEAD_DIM), F32),
        ],
        compiler_params=_cparams(("arbitrary", "arbitrary")),
    )(t5_table.astype(F32).reshape(-1), proj, proj, proj, bucket_tabs, *[arr for arr, _ in casts])
    return out[0], list(out[1:])


def _s5_discretise(lam_re, lam_im, log_step, b_re, b_im, c_re, c_im):
    g, p = lam_re.shape[1:]
    slabs = g // (2 * S5_PAIRS)
    step = jnp.exp(log_step.astype(F32))[..., None]
    lr = jnp.minimum(lam_re.astype(F32), -1e-4)
    li = lam_im.astype(F32)
    mag = jnp.exp(lr * step)
    ab_re = mag * jnp.cos(li * step)
    ab_im = mag * jnp.sin(li * step)
    den = lr * lr + li * li
    zr = ((ab_re - 1.0) * lr + ab_im * li) / den
    zi = (ab_im * lr - (ab_re - 1.0) * li) / den
    br = b_re.astype(F32)[None]
    bi = b_im.astype(F32)[None]
    bb_re = zr[..., None] * br - zi[..., None] * bi
    bb_im = zr[..., None] * bi + zi[..., None] * br

    a_re = ab_re.reshape(2, slabs, S5_PAIRS, 2 * p)
    a_im = ab_im.reshape(2, slabs, S5_PAIRS, 2 * p)

    eye2 = jnp.eye(2, dtype=F32)

    def pack_b(bb):
        bb = bb.reshape(2, slabs, S5_PAIRS, 2, p, B_GROUP)
        return jnp.einsum('dsjepc,ef->dsjecfp', bb, eye2).reshape(2, slabs, S5_SLAB, 2 * p)

    def pack_c(cc):
        cc = cc.astype(F32).reshape(2, slabs, S5_PAIRS, 2, B_GROUP, p)
        return jnp.einsum('dsjecp,ef->dsfpjec', cc, eye2).reshape(2, slabs, 2 * p, S5_SLAB)

    b_mat = jnp.concatenate([pack_b(bb_re), pack_b(bb_im)], axis=-1)
    c_mat = jnp.concatenate([pack_c(c_re), -pack_c(c_im)], axis=-2)
    return a_re, a_im, b_mat.astype(BF16), c_mat.astype(BF16)


def _s5_kernel(uf_ref, ub_ref, b_ref, c_ref, are_ref, aim_ref, yf_ref, yb_ref,
               xr_ref, xi_ref, bur_ref, bui_ref, *, tm, slabs):
    c = pl.program_id(0)
    nstate = 2 * B_STATE
    u_refs = (uf_ref, ub_ref)
    y_refs = (yf_ref, yb_ref)

    @pl.when(c == 0)
    def _():
        xr_ref[...] = jnp.zeros_like(xr_ref)
        xi_ref[...] = jnp.zeros_like(xi_ref)

    lane_pair = lax.broadcasted_iota(jnp.int32, (1, S5_SLAB), 1) // (2 * B_GROUP)

    def pair_rows(j):
        return pl.ds(j, tm, stride=S5_PITCH)

    for d in range(2):
        for s in range(slabs):
            u16 = u_refs[d][:, s * S5_SLAB:(s + 1) * S5_SLAB].astype(BF16)
            lhs = jnp.concatenate([jnp.where(lane_pair == j, u16, jnp.zeros_like(u16))
                                   for j in range(S5_PAIRS)], axis=0)
            bu = jnp.dot(lhs, b_ref[d, s], preferred_element_type=F32)
            for j in range(S5_PAIRS):
                bur_ref[d, s, pair_rows(j), :] = bu[j * tm:(j + 1) * tm, :nstate]
                bui_ref[d, s, pair_rows(j), :] = bu[j * tm:(j + 1) * tm, nstate:]

    ar = [[are_ref[d, s] for s in range(slabs)] for d in range(2)]
    ai = [[aim_ref[d, s] for s in range(slabs)] for d in range(2)]

    def step(i, carry):
        new = []
        for d in range(2):
            t = i if d == 0 else tm - 1 - i
            start = t * S5_PITCH
            rows = pl.ds(pl.multiple_of(start, 8) if S5_PITCH % 8 == 0 else start, S5_PAIRS)
            for s in range(slabs):
                k = 2 * (d * slabs + s)
                xr, xi = carry[k], carry[k + 1]
                nxr = ar[d][s] * xr - ai[d][s] * xi + bur_ref[d, s, rows, :]
                nxi = ar[d][s] * xi + ai[d][s] * xr + bui_ref[d, s, rows, :]
                bur_ref[d, s, rows, :] = nxr
                bui_ref[d, s, rows, :] = nxi
                new += [nxr, nxi]
        return tuple(new)

    init = []
    for d in range(2):
        for s in range(slabs):
            init += [xr_ref[d, s], xi_ref[d, s]]
    final = tuple(init)
    for i in range(tm):
        final = step(i, final)
    for d in range(2):
        for s in range(slabs):
            k = 2 * (d * slabs + s)
            xr_ref[d, s] = final[k]
            xi_ref[d, s] = final[k + 1]

    for d in range(2):
        for s in range(slabs):
            xcat = jnp.concatenate(
                [jnp.concatenate([bur_ref[d, s, pair_rows(j), :], bui_ref[d, s, pair_rows(j), :]],
                                 axis=1).astype(BF16) for j in range(S5_PAIRS)], axis=0)
            r = jnp.dot(xcat, c_ref[d, s], preferred_element_type=F32)
            y = r[:tm]
            for j in range(1, S5_PAIRS):
                y = jnp.where(lane_pair == j, r[j * tm:(j + 1) * tm], y)
            y_refs[d][:, s * S5_SLAB:(s + 1) * S5_SLAB] = y


def s5_scan(proj, u_col0, a_re, a_im, b_mat, c_mat, seq, casts=(), tm=256):
    slabs = a_re.shape[1]
    width = slabs * S5_SLAB
    nchunks = seq // tm
    assert u_col0 % width == 0
    ucol = u_col0 // width
    wspec = pl.BlockSpec((2, slabs, S5_SLAB, S5_SLAB), lambda c: (0, 0, 0, 0))
    aspec = pl.BlockSpec((2, slabs, S5_PAIRS, 2 * B_STATE), lambda c: (0, 0, 0, 0))
    state = pltpu.VMEM((2, slabs, S5_PAIRS, 2 * B_STATE), F32)
    drive = pltpu.VMEM((2, slabs, tm * S5_PITCH, 2 * B_STATE), F32)
    cast_in_specs, cast_out_specs, cast_shapes = _cast_plumbing(casts, nchunks, lambda c: c)
    out = pl.pallas_call(
        _with_casts(functools.partial(_s5_kernel, tm=tm, slabs=slabs), 6, 2, len(casts)),
        grid=(nchunks,),
        in_specs=[pl.BlockSpec((tm, width), lambda c: (c, ucol)),
                  pl.BlockSpec((tm, width), lambda c: (nchunks - 1 - c, ucol)),
                  wspec, wspec, aspec, aspec] + cast_in_specs,
        out_specs=[pl.BlockSpec((tm, width), lambda c: (c, 0)),
                   pl.BlockSpec((tm, width), lambda c: (nchunks - 1 - c, 0))] + cast_out_specs,
        out_shape=[jax.ShapeDtypeStruct((seq, width), F32), jax.ShapeDtypeStruct((seq, width), F32)] + cast_shapes,
        scratch_shapes=[state, state, drive, drive],
        compiler_params=_cparams(("arbitrary",)),
    )(proj, proj, b_mat, c_mat, a_re, a_im, *[arr for arr, _ in casts])
    return out[0], out[1], list(out[2:])


def _s5_out_kernel(oa_ref, yf_ref, yb_ref, u_ref, dsk_ref, wg_ref, w_ref, r_ref, o_ref):
    y = jax.nn.gelu(yf_ref[...] + yb_ref[...] + dsk_ref[...] * u_ref[...])
    z = jnp.dot(y.astype(BF16), wg_ref[...], preferred_element_type=F32)
    ob = (y * jax.nn.sigmoid(z)).astype(BF16)
    ka = oa_ref.shape[1]
    acc = r_ref[...] + jnp.dot(oa_ref[...], w_ref[pl.ds(0, ka), :], preferred_element_type=F32)
    o_ref[...] = acc + jnp.dot(ob, w_ref[pl.ds(ka, ob.shape[1]), :], preferred_element_type=F32)


def s5_glu_out_projection(o_a, yf, yb, proj, u_col0, d_skip, w_glu, glu_layer, w_out, res, tm=512):
    m, k = yf.shape
    ka = o_a.shape[1]
    n = w_out.shape[2]
    row = lambda i: (i, 0)
    return pl.pallas_call(
        _s5_out_kernel,
        grid=(m // tm,),
        in_specs=[
            pl.BlockSpec((tm, ka), row),
            pl.BlockSpec((tm, k), row),
            pl.BlockSpec((tm, k), row),
            pl.BlockSpec((tm, k), lambda i: (i, u_col0 // k)),
            pl.BlockSpec((1, k), lambda i: (0, 0)),
            pl.BlockSpec((None, k, k), lambda i: (glu_layer, 0, 0)),
            pl.BlockSpec((None, ka + k, n), lambda i: (0, 0, 0)),
            pl.BlockSpec((tm, n), row),
        ],
        out_specs=pl.BlockSpec((tm, n), row),
        out_shape=jax.ShapeDtypeStruct((m, n), F32),
        compiler_params=_cparams(("arbitrary",)),
    )(o_a, yf, yb, proj, d_skip.reshape(1, k), w_glu, w_out, res)


def _na_bias_pairs(rpb):
    c = np.arange(GRID_W)
    col_idx = np.clip(c[None, :] - c[:, None] + NA_COLS - 1, 0, 2 * NA_COLS - 2)
    onehot = (col_idx[None] == np.arange(2 * NA_COLS - 1)[:, None, None]).astype(np.float32)
    r2 = jnp.einsum('hdk,kqc->hdqc', rpb.astype(F32), jnp.asarray(onehot), precision=lax.Precision.HIGHEST)
    r2 = jnp.pad(r2, ((0, 0), (4, 4), (0, 0), (0, 0)))
    return jnp.concatenate([r2[:, :-1], r2[:, 1:]], axis=-1)


def _na_mask_tables():
    qr = np.arange(NA_QROWS)[:, None, None, None]
    cq = np.arange(GRID_W)[None, :, None, None]
    kr = np.arange(NA_KROWS)[None, None, :, None]
    ck = np.arange(GRID_W)[None, None, None, :]
    cs = np.clip(cq - NA_COLS // 2, 0, GRID_W - NA_COLS)
    col_ok = (ck >= cs) & (ck < cs + NA_COLS)
    row_ok = [(kr < NA_ROWS) & (qr >= 0),
              (kr >= qr) & (kr < qr + NA_ROWS),
              (kr >= NA_KROWS - NA_ROWS) & (qr >= 0)]
    out = [np.where(r & col_ok, 0.0, NEG_INF).reshape(NA_QROWS * GRID_W, NA_KROWS * GRID_W) for r in row_ok]
    return np.stack(out).astype(np.float32)


def _natten_kernel(q_ref, k_ref, v_ref, pair_ref, mask_ref, o_ref, bias_ref, *, groups, rows):
    gs = pl.program_id(1)
    nk = NA_KROWS * GRID_W
    tq = NA_QROWS * GRID_W

    @pl.when(gs == 0)
    def _():
        for hh in range(NA_HEADS_PER_STEP):
            for v in range(3):
                for qr in range(NA_QROWS):
                    q_rows = pl.ds(qr * GRID_W, GRID_W)
                    for kr2 in range(NA_KROWS // 2):
                        k_cols = pl.ds(kr2 * 128, 128)
                        bias_ref[hh, v, q_rows, k_cols] = (
                            pair_ref[hh, 2 * (kr2 + 2 * (2 - v)) - qr + 3] + mask_ref[v, q_rows, k_cols]) * LOG2E

    for gi in range(NA_GROUPS_PER_STEP):
        g = gs * NA_GROUPS_PER_STEP + gi
        q_rows = pl.ds(gi * tq, tq)
        key_row0 = jnp.clip(NA_QROWS * g - NA_ROWS // 2, 0, rows - NA_KROWS)
        tok0 = pl.multiple_of(key_row0 * GRID_W, GRID_W)
        variant = jnp.where(g == 0, 0, jnp.where(g == groups - 1, 2, 1))
        for hh in range(NA_HEADS_PER_STEP):
            cols = pl.ds(hh * HEAD_DIM, HEAD_DIM)
            s = lax.dot_general(q_ref[q_rows, cols], k_ref[pl.ds(tok0, nk), cols], (((1,), (1,)), ((), ())),
                                preferred_element_type=F32)
            s = s + bias_ref[hh, variant]
            m = jnp.max(s, axis=1, keepdims=True)
            p = jnp.exp2(s - m)
            l = jnp.sum(p, axis=1, keepdims=True)
            o = jnp.dot(p.astype(BF16), v_ref[pl.ds(tok0, nk), cols], preferred_element_type=F32)
            o_ref[q_rows, cols] = (o / l).astype(o_ref.dtype)


def neighbourhood_attention(qkv, bias_pairs, mask_tabs, seq, heads, casts=()):
    rows = seq // GRID_W
    assert rows % (NA_QROWS * NA_GROUPS_PER_STEP) == 0 and rows >= NA_KROWS + NA_QROWS
    assert heads % NA_HEADS_PER_STEP == 0
    groups = rows // NA_QROWS
    tq = NA_QROWS * GRID_W * NA_GROUPS_PER_STEP
    hw = NA_HEADS_PER_STEP * HEAD_DIM
    hsteps = heads // NA_HEADS_PER_STEP
    gsteps = groups // NA_GROUPS_PER_STEP
    cast_in_specs, cast_out_specs, cast_shapes = _cast_plumbing(casts, hsteps * gsteps, lambda h, g: h * gsteps + g)
    out = pl.pallas_call(
        _with_casts(functools.partial(_natten_kernel, groups=groups, rows=rows), 5, 1, len(casts)),
        grid=(hsteps, gsteps),
        in_specs=[
            pl.BlockSpec((tq, hw), lambda h, g: (g, h)),
            pl.BlockSpec((seq, hw), lambda h, g: (0, hsteps + h)),
            pl.BlockSpec((seq, hw), lambda h, g: (0, 2 * hsteps + h)),
            pl.BlockSpec((NA_HEADS_PER_STEP,) + bias_pairs.shape[1:], lambda h, g: (h, 0, 0, 0)),
            pl.BlockSpec(mask_tabs.shape, lambda h, g: (0, 0, 0)),
        ] + cast_in_specs,
        out_specs=[pl.BlockSpec((tq, hw), lambda h, g: (g, h))] + cast_out_specs,
        out_shape=[jax.ShapeDtypeStruct((seq, heads * HEAD_DIM), BF16)] + cast_shapes,
        scratch_shapes=[pltpu.VMEM((NA_HEADS_PER_STEP, 3, NA_QROWS * GRID_W, NA_KROWS * GRID_W), F32)],
        compiler_params=_cparams(("arbitrary", "arbitrary")),
    )(qkv, qkv, qkv, bias_pairs, mask_tabs, *[arr for arr, _ in casts])
    return out[0], list(out[1:])


def kernel(x, t5_bias, ab_w_in, ab_w_out, s5_lam_re, s5_lam_im, s5_log_step, s5_b_re, s5_b_im, s5_c_re, s5_c_im, s5_d, s5_w_glu, c_w_qkv, c_w_out, c_rpb, norm_mix, norm_mlp, mlp_w1, mlp_w2, norm_final):
    batch, seq, d_model = x.shape
    depth = norm_mix.shape[0]
    c_heads = c_rpb.shape[1]
    t5_buckets = _t5_bucket_tables()
    na_mask = jnp.asarray(_na_mask_tables())
    attn_scale = LOG2E / math.sqrt(HEAD_DIM)
    w_glu = s5_w_glu.astype(BF16)

    def mixer_weights(i):
        return ((ab_w_in, ab_w_out) if i % 2 == 0 else (c_w_qkv, c_w_out)), i // 2

    (mix_in0, mix_out0), _ = mixer_weights(0)
    mix_bf = {0: [mix_in0[:1].astype(BF16), mix_out0[:1].astype(BF16)]}
    outs = []
    for bi in range(batch):
        h = x[bi]
        for i in range(depth):
            j = i // 2
            w_mix_in, w_mix_out = mix_bf[i]
            if i % 2 == 0:
                proj = norm_matmul(h, norm_mix[i], w_mix_in, 0, F32)
                o_a, _ = dilated_attention(proj, t5_bias, t5_buckets, seq)
                a_re, a_im, b_mat, c_mat = _s5_discretise(
                    s5_lam_re[j], s5_lam_im[j], s5_log_step[j], s5_b_re[j], s5_b_im[j], s5_c_re[j], s5_c_im[j])
                yf, yb, (w1, w2) = s5_scan(proj, 3 * A_WIDTH, a_re, a_im, b_mat, c_mat, seq,
                                           casts=[(mlp_w1, i), (mlp_w2, i)])
                h = s5_glu_out_projection(o_a, yf, yb, proj, 3 * A_WIDTH, s5_d[j], w_glu, j, w_mix_out, h)
            else:
                qkv = norm_matmul(h, norm_mix[i], w_mix_in, 0, BF16, scaled_cols=c_heads * HEAD_DIM, scale=attn_scale)
                o, (w1, w2) = neighbourhood_attention(qkv, _na_bias_pairs(c_rpb[j]), na_mask, seq, c_heads,
                                                      casts=[(mlp_w1, i), (mlp_w2, i)])
                h = matmul_residual([o], w_mix_out, 0, h)
            casts = []
            if i + 1 < depth and i + 1 not in mix_bf:
                (nxt_in, nxt_out), jn = mixer_weights(i + 1)
                casts = [(nxt_in, jn), (nxt_out, jn)]
            h, cast_out = mlp(h, norm_mlp[i], w1, w2, 0, casts)
            if casts:
                mix_bf[i + 1] = cast_out
        outs.append(rmsnorm(h, norm_final))
    return jnp.stack(outs)
```

```python
import functools
import math

import numpy as np
import jax
import jax.numpy as jnp
from jax import lax
from jax.experimental import pallas as pl
from jax.experimental.pallas import tpu as pltpu

F32 = jnp.float32
BF16 = jnp.bfloat16

HEAD_DIM = 128
A_HEADS = 8
A_WIDTH = A_HEADS * HEAD_DIM
DILATED_BRANCHES = ((128, 1), (512, 4), (2048, 16))
A_QBLOCK = 128
A_HALF = 64
A_KBLOCK = A_QBLOCK + 2 * A_HALF
A_SUPER = 2048
B_GROUP = 16
B_STATE = 64
S5_SLAB = 256
S5_PAIRS = 8
S5_PITCH = 9
GRID_W = 64
NA_ROWS = 8
NA_COLS = 16
NA_QROWS = 4
NA_KROWS = 12
NA_EXT = NA_KROWS + 8
NA_HEADS_PER_STEP = 2
NA_GROUPS_PER_STEP = 4
T5_BUCKETS = 32
T5_MAX_DISTANCE = 1024
RMS_EPS = 1e-6
NEG_INF = -1e30
LOG2E = math.log2(math.e)
VMEM_LIMIT = 56 * 1024 * 1024


def _cparams(sem):
    return pltpu.CompilerParams(dimension_semantics=sem, vmem_limit_bytes=VMEM_LIMIT)


def _rms_rows(x, g):
    y = x * lax.rsqrt(jnp.mean(x * x, axis=-1, keepdims=True) + RMS_EPS)
    return y * g


def _with_casts(body, n_in, n_out, ncast):
    def wrapped(*refs):
        ins = refs[:n_in]
        cast_in = refs[n_in:n_in + ncast]
        outs = refs[n_in + ncast:n_in + ncast + n_out]
        cast_out = refs[n_in + ncast + n_out:n_in + 2 * ncast + n_out]
        for src, dst in zip(cast_in, cast_out):
            dst[...] = src[...].astype(BF16)
        body(*ins, *outs, *refs[n_in + 2 * ncast + n_out:])
    return wrapped


def _cast_plumbing(casts, nblocks, block_id):
    in_specs, out_specs, shapes = [], [], []
    for arr, layer in casts:
        _, rows, cols = arr.shape
        assert rows % (16 * nblocks) == 0
        blk = (None, rows // nblocks, cols)
        in_specs.append(pl.BlockSpec(blk, lambda *g, layer=layer: (layer, block_id(*g), 0)))
        out_specs.append(pl.BlockSpec(blk, lambda *g: (0, block_id(*g), 0)))
        shapes.append(jax.ShapeDtypeStruct((1, rows, cols), BF16))
    return in_specs, out_specs, shapes


def _serpentine(i, j, n):
    return jnp.where(i % 2 == 0, j, n - 1 - j)


def _norm_matmul_kernel(x_ref, g_ref, w_ref, o_ref, xn_ref, *, col_blocks, scaled_blocks, scale):
    j = pl.program_id(1)

    @pl.when(j == 0)
    def _():
        xn_ref[...] = _rms_rows(x_ref[...], g_ref[...]).astype(BF16)

    acc = jnp.dot(xn_ref[...], w_ref[...], preferred_element_type=F32)
    if scaled_blocks:
        acc = acc * jnp.where(_serpentine(pl.program_id(0), j, col_blocks) < scaled_blocks, scale, 1.0)
    o_ref[...] = acc.astype(o_ref.dtype)


def norm_matmul(x, g, w, layer, out_dtype, scaled_cols=0, scale=1.0, tm=1024, tn=1024):
    m, k = x.shape
    n = w.shape[2]
    assert scaled_cols % tn == 0
    nj = n // tn
    return pl.pallas_call(
        functools.partial(_norm_matmul_kernel, col_blocks=nj, scaled_blocks=scaled_cols // tn, scale=scale),
        grid=(m // tm, nj),
        in_specs=[
            pl.BlockSpec((tm, k), lambda i, j: (i, 0)),
            pl.BlockSpec((1, k), lambda i, j: (0, 0)),
            pl.BlockSpec((None, k, tn), lambda i, j: (layer, 0, _serpentine(i, j, nj))),
        ],
        out_specs=pl.BlockSpec((tm, tn), lambda i, j: (i, _serpentine(i, j, nj))),
        out_shape=jax.ShapeDtypeStruct((m, n), out_dtype),
        scratch_shapes=[pltpu.VMEM((tm, k), BF16)],
        compiler_params=_cparams(("arbitrary", "arbitrary")),
    )(x, g.reshape(1, k), w)


def _matmul_residual_kernel(*refs, widths):
    a_refs = refs[:len(widths)]
    w_ref, r_ref, o_ref = refs[len(widths):]
    acc = r_ref[...]
    row = 0
    for a_ref, kw in zip(a_refs, widths):
        acc = acc + jnp.dot(a_ref[...], w_ref[pl.ds(row, kw), :], preferred_element_type=F32)
        row += kw
    o_ref[...] = acc


def matmul_residual(a_list, w, layer, res, tm=512):
    m = res.shape[0]
    _, k, n = w.shape
    widths = tuple(a.shape[1] for a in a_list)
    assert sum(widths) == k
    return pl.pallas_call(
        functools.partial(_matmul_residual_kernel, widths=widths),
        grid=(m // tm,),
        in_specs=[pl.BlockSpec((tm, kw), lambda i: (i, 0)) for kw in widths] + [
            pl.BlockSpec((None, k, n), lambda i: (layer, 0, 0)),
            pl.BlockSpec((tm, n), lambda i: (i, 0)),
        ],
        out_specs=pl.BlockSpec((tm, n), lambda i: (i, 0)),
        out_shape=jax.ShapeDtypeStruct((m, n), F32),
        compiler_params=_cparams(("arbitrary",)),
    )(*a_list, w, res)


def _mlp_kernel(x_ref, xres_ref, g_ref, w1_ref, w2_ref, o_ref, xn_ref, h_ref, *, up_steps, tf):
    i = pl.program_id(0)
    s = pl.program_id(1)

    @pl.when(s == 0)
    def _():
        xn_ref[...] = _rms_rows(x_ref[...], g_ref[...]).astype(BF16)

    @pl.when(s < up_steps)
    def _():
        h = jnp.dot(xn_ref[...], w1_ref[...], preferred_element_type=F32)
        h_ref[_serpentine(i, s, up_steps)] = jnp.square(jnp.maximum(h, 0.0)).astype(BF16)

    @pl.when(s >= up_steps)
    def _():
        acc = xres_ref[...]
        for c in range(up_steps):
            acc = acc + jnp.dot(h_ref[c], w2_ref[pl.ds(c * tf, tf), :], preferred_element_type=F32)
        o_ref[...] = acc


def mlp(x, g, w1, w2, layer, casts=(), tm=512, tf=1024, tn=512):
    m, d = x.shape
    dff = w1.shape[2]
    up_steps = dff // tf
    down_steps = d // tn
    cast_in_specs, cast_out_specs, cast_shapes = _cast_plumbing(
        casts, (m // tm) * up_steps, lambda i, s: i * up_steps + jnp.minimum(s, up_steps - 1))

    def up_block(i, s):
        return _serpentine(i, jnp.minimum(s, up_steps - 1), up_steps)

    def down_block(i, s):
        return _serpentine(i, jnp.maximum(s - up_steps, 0), down_steps)

    out = pl.pallas_call(
        _with_casts(functools.partial(_mlp_kernel, up_steps=up_steps, tf=tf), 5, 1, len(casts)),
        grid=(m // tm, up_steps + down_steps),
        in_specs=[
            pl.BlockSpec((tm, d), lambda i, s: (i, 0)),
            pl.BlockSpec((tm, tn), lambda i, s: (i, down_block(i, s))),
            pl.BlockSpec((1, d), lambda i, s: (0, 0)),
            pl.BlockSpec((None, d, tf), lambda i, s: (layer, 0, up_block(i, s))),
            pl.BlockSpec((None, dff, tn), lambda i, s: (layer, 0, down_block(i, s))),
        ] + cast_in_specs,
        out_specs=[pl.BlockSpec((tm, tn), lambda i, s: (i, down_block(i, s)))] + cast_out_specs,
        out_shape=[jax.ShapeDtypeStruct((m, d), F32)] + cast_shapes,
        scratch_shapes=[pltpu.VMEM((tm, d), BF16), pltpu.VMEM((up_steps, tm, tf), BF16)],
        compiler_params=_cparams(("arbitrary", "arbitrary")),
    )(x, x, g.reshape(1, d), w1, w2, *[arr for arr, _ in casts])
    return out[0], list(out[1:])


def _rmsnorm_kernel(x_ref, g_ref, o_ref):
    o_ref[...] = _rms_rows(x_ref[...], g_ref[...])


def rmsnorm(x, g, tm=512):
    m, d = x.shape
    return pl.pallas_call(
        _rmsnorm_kernel,
        grid=(m // tm,),
        in_specs=[pl.BlockSpec((tm, d), lambda i: (i, 0)), pl.BlockSpec((1, d), lambda i: (0, 0))],
        out_specs=pl.BlockSpec((tm, d), lambda i: (i, 0)),
        out_shape=jax.ShapeDtypeStruct((m, d), F32),
        compiler_params=_cparams(("arbitrary",)),
    )(x, g.reshape(1, d))


def _t5_bucket(rel):
    half = T5_BUCKETS // 2
    max_exact = half // 2
    n = jnp.abs(rel)
    nf = jnp.maximum(n, 1).astype(F32)
    large = max_exact + (jnp.log(nf / max_exact) / math.log(T5_MAX_DISTANCE / max_exact)
                         * (half - max_exact)).astype(jnp.int32)
    large = jnp.minimum(large, half - 1)
    return jnp.where(rel > 0, half, 0) + jnp.where(n < max_exact, n, large)


def _t5_bucket_tables():
    tabs = []
    for _, dil in DILATED_BRANCHES:
        off = jnp.arange(A_KBLOCK)[None, :] - A_HALF - jnp.arange(A_QBLOCK)[:, None]
        tabs.append(jnp.where(jnp.abs(off) <= A_HALF, _t5_bucket(off * dil), T5_BUCKETS))
    return jnp.stack(tabs).astype(jnp.int32)


def _dilated_sections(seq):
    bases, sizes, row = [], [], 0
    for _, dil in DILATED_BRANCHES:
        sec = seq // dil + 2 * A_HALF
        bases.append(row)
        sizes.append(sec)
        row += dil * sec
    return bases, sizes, row


def _dilated_kernel(t5_ref, q_ref, k_ref, v_ref, bucket_ref, o_ref,
                    kd_ref, vd_ref, tmp_ref, bias_ref, m_ref, l_ref, n_ref, *, seq):
    h = pl.program_id(0)
    t = pl.program_id(1)
    scale = LOG2E / math.sqrt(HEAD_DIM)
    bases, sizes, _ = _dilated_sections(seq)

    @pl.when(t == 0)
    def _():
        for b in range(len(DILATED_BRANCHES)):
            bk = bucket_ref[b]
            acc = jnp.full((A_QBLOCK, A_KBLOCK), NEG_INF, F32)
            for kbkt in range(T5_BUCKETS):
                acc = jnp.where(bk == kbkt, t5_ref[kbkt * A_HEADS + h], acc)
            bias_ref[b] = acc * LOG2E
        zeros = jnp.zeros((A_HALF, HEAD_DIM), BF16)
        (_, d0), (_, d1), (_, d2) = DILATED_BRANCHES
        assert d0 == 1 and d2 % d1 == 0
        ratio = d2 // d1

        def put(dst, b, r, rows):
            o = bases[b] + r * sizes[b]
            dst[pl.ds(o, A_HALF), :] = zeros
            dst[pl.ds(o + A_HALF + rows.shape[0], A_HALF), :] = zeros
            dst[pl.ds(o + A_HALF, rows.shape[0]), :] = rows.astype(BF16)

        for src, dst in ((k_ref, kd_ref), (v_ref, vd_ref)):
            put(dst, 0, 0, src[...])
            for r1 in range(d1):
                rows1 = src[pl.ds(r1, seq // d1, stride=d1), :]
                put(dst, 1, r1, rows1)
                tmp_ref[...] = rows1
                for a in range(ratio):
                    put(dst, 2, a * d1 + r1, tmp_ref[pl.ds(a, seq // d2, stride=ratio), :])

    kk = lax.broadcasted_iota(jnp.int32, (1, A_KBLOCK), 1)
    for b, (_, dil) in enumerate(DILATED_BRANCHES):
        sub_len = seq // dil
        blocks_per_residue = A_SUPER // dil // A_QBLOCK

        def body(idx, carry, b=b, dil=dil, sub_len=sub_len, blocks_per_residue=blocks_per_residue):
            r = idx // blocks_per_residue
            n = idx % blocks_per_residue
            qs = r + n * (A_QBLOCK * dil)
            q_idx = pl.ds(qs, A_QBLOCK) if dil == 1 else pl.ds(qs, A_QBLOCK, stride=dil)
            blk = t * blocks_per_residue + n
            k_idx = pl.ds(pl.multiple_of(bases[b] + r * sizes[b] + blk * A_QBLOCK, A_QBLOCK), A_KBLOCK)
            qb = (q_ref[q_idx, :] * scale).astype(BF16)
            s = lax.dot_general(qb, kd_ref[k_idx, :], (((1,), (1,)), ((), ())), preferred_element_type=F32)
            key_l = blk * A_QBLOCK - A_HALF + kk
            edge = jnp.where((key_l >= 0) & (key_l < sub_len), 0.0, NEG_INF)
            s = s + bias_ref[b] + edge
            m = jnp.max(s, axis=1, keepdims=True)
            p = jnp.exp2(s - m)
            l = jnp.sum(p, axis=1, keepdims=True)
            num = jnp.dot(p.astype(BF16), vd_ref[k_idx, :], preferred_element_type=F32)
            m_ref[b, q_idx, :] = jnp.broadcast_to(m, (A_QBLOCK, HEAD_DIM))
            l_ref[b, q_idx, :] = jnp.broadcast_to(l, (A_QBLOCK, HEAD_DIM))
            n_ref[b, q_idx, :] = num
            return carry

        lax.fori_loop(0, A_SUPER // A_QBLOCK, body, 0, unroll=16)

    rows = 256
    for c in range(A_SUPER // rows):
        sl = pl.ds(c * rows, rows)
        m0, m1, m2 = m_ref[0, sl, :], m_ref[1, sl, :], m_ref[2, sl, :]
        mx = jnp.maximum(jnp.maximum(m0, m1), m2)
        w0, w1, w2 = jnp.exp2(m0 - mx), jnp.exp2(m1 - mx), jnp.exp2(m2 - mx)
        num = w0 * n_ref[0, sl, :] + w1 * n_ref[1, sl, :] + w2 * n_ref[2, sl, :]
        den = w0 * l_ref[0, sl, :] + w1 * l_ref[1, sl, :] + w2 * l_ref[2, sl, :]
        o_ref[sl, :] = (num / den).astype(o_ref.dtype)


def dilated_attention(proj, t5_table, bucket_tabs, seq, casts=()):
    assert seq % A_SUPER == 0
    nb = len(DILATED_BRANCHES)
    total_rows = _dilated_sections(seq)[2]
    tiles = seq // A_SUPER
    cast_in_specs, cast_out_specs, cast_shapes = _cast_plumbing(casts, A_HEADS * tiles, lambda h, t: h * tiles + t)
    out = pl.pallas_call(
        _with_casts(functools.partial(_dilated_kernel, seq=seq), 5, 1, len(casts)),
        grid=(A_HEADS, tiles),
        in_specs=[
            pl.BlockSpec(memory_space=pltpu.SMEM),
            pl.BlockSpec((A_SUPER, HEAD_DIM), lambda h, t: (t, h)),
            pl.BlockSpec((seq, HEAD_DIM), lambda h, t: (0, A_HEADS + h)),
            pl.BlockSpec((seq, HEAD_DIM), lambda h, t: (0, 2 * A_HEADS + h)),
            pl.BlockSpec((nb, A_QBLOCK, A_KBLOCK), lambda h, t: (0, 0, 0)),
        ] + cast_in_specs,
        out_specs=[pl.BlockSpec((A_SUPER, HEAD_DIM), lambda h, t: (t, h))] + cast_out_specs,
        out_shape=[jax.ShapeDtypeStruct((seq, A_WIDTH), BF16)] + cast_shapes,
        scratch_shapes=[
            pltpu.VMEM((total_rows, HEAD_DIM), BF16),
            pltpu.VMEM((total_rows, HEAD_DIM), BF16),
            pltpu.VMEM((seq // DILATED_BRANCHES[1][1], HEAD_DIM), F32),
            pltpu.VMEM((nb, A_QBLOCK, A_KBLOCK), F32),
            pltpu.VMEM((nb, A_SUPER, HEAD_DIM), F32),
            pltpu.VMEM((nb, A_SUPER, HEAD_DIM), F32),
            pltpu.VMEM((nb, A_SUPER, HEAD_DIM), F32),
        ],
        compiler_params=_cparams(("arbitrary", "arbitrary")),
    )(t5_table.astype(F32).reshape(-1), proj, proj, proj, bucket_tabs, *[arr for arr, _ in casts])
    return out[0], list(out[1:])


def _s5_discretise(lam_re, lam_im, log_step, b_re, b_im, c_re, c_im):
    g, p = lam_re.shape[1:]
    slabs = g // (2 * S5_PAIRS)
    step = jnp.exp(log_step.astype(F32))[..., None]
    lr = jnp.minimum(lam_re.astype(F32), -1e-4)
    li = lam_im.astype(F32)
    mag = jnp.exp(lr * step)
    ab_re = mag * jnp.cos(li * step)
    ab_im = mag * jnp.sin(li * step)
    den = lr * lr + li * li
    zr = ((ab_re - 1.0) * lr + ab_im * li) / den
    zi = (ab_im * lr - (ab_re - 1.0) * li) / den
    br = b_re.astype(F32)[None]
    bi = b_im.astype(F32)[None]
    bb_re = zr[..., None] * br - zi[..., None] * bi
    bb_im = zr[..., None] * bi + zi[..., None] * br

    a_re = ab_re.reshape(2, slabs, S5_PAIRS, 2 * p)
    a_im = ab_im.reshape(2, slabs, S5_PAIRS, 2 * p)

    eye2 = jnp.eye(2, dtype=F32)

    def pack_b(bb):
        bb = bb.reshape(2, slabs, S5_PAIRS, 2, p, B_GROUP)
        return jnp.einsum('dsjepc,ef->dsjecfp', bb, eye2).reshape(2, slabs, S5_SLAB, 2 * p)

    def pack_c(cc):
        cc = cc.astype(F32).reshape(2, slabs, S5_PAIRS, 2, B_GROUP, p)
        return jnp.einsum('dsjecp,ef->dsfpjec', cc, eye2).reshape(2, slabs, 2 * p, S5_SLAB)

    b_mat = jnp.concatenate([pack_b(bb_re), pack_b(bb_im)], axis=-1)
    c_mat = jnp.concatenate([pack_c(c_re), -pack_c(c_im)], axis=-2)
    return a_re, a_im, b_mat.astype(BF16), c_mat.astype(BF16)


def _s5_kernel(uf_ref, ub_ref, b_ref, c_ref, are_ref, aim_ref, yf_ref, yb_ref,
               xr_ref, xi_ref, bur_ref, bui_ref, *, tm, slabs):
    c = pl.program_id(0)
    nstate = 2 * B_STATE
    u_refs = (uf_ref, ub_ref)
    y_refs = (yf_ref, yb_ref)

    @pl.when(c == 0)
    def _():
        xr_ref[...] = jnp.zeros_like(xr_ref)
        xi_ref[...] = jnp.zeros_like(xi_ref)

    lane_pair = lax.broadcasted_iota(jnp.int32, (1, S5_SLAB), 1) // (2 * B_GROUP)

    def pair_rows(j):
        return pl.ds(j, tm, stride=S5_PITCH)

    for d in range(2):
        for s in range(slabs):
            u16 = u_refs[d][:, s * S5_SLAB:(s + 1) * S5_SLAB].astype(BF16)
            lhs = jnp.concatenate([jnp.where(lane_pair == j, u16, jnp.zeros_like(u16))
                                   for j in range(S5_PAIRS)], axis=0)
            bu = jnp.dot(lhs, b_ref[d, s], preferred_element_type=F32)
            for j in range(S5_PAIRS):
                bur_ref[d, s, pair_rows(j), :] = bu[j * tm:(j + 1) * tm, :nstate]
                bui_ref[d, s, pair_rows(j), :] = bu[j * tm:(j + 1) * tm, nstate:]

    ar = [[are_ref[d, s] for s in range(slabs)] for d in range(2)]
    ai = [[aim_ref[d, s] for s in range(slabs)] for d in range(2)]

    def step(i, carry):
        new = []
        for d in range(2):
            t = i if d == 0 else tm - 1 - i
            start = t * S5_PITCH
            rows = pl.ds(pl.multiple_of(start, 8) if S5_PITCH % 8 == 0 else start, S5_PAIRS)
            for s in range(slabs):
                k = 2 * (d * slabs + s)
                xr, xi = carry[k], carry[k + 1]
                nxr = ar[d][s] * xr - ai[d][s] * xi + bur_ref[d, s, rows, :]
                nxi = ar[d][s] * xi + ai[d][s] * xr + bui_ref[d, s, rows, :]
                bur_ref[d, s, rows, :] = nxr
                bui_ref[d, s, rows, :] = nxi
                new += [nxr, nxi]
        return tuple(new)

    init = []
    for d in range(2):
        for s in range(slabs):
            init += [xr_ref[d, s], xi_ref[d, s]]
    final = tuple(init)
    for i in range(tm):
        final = step(i, final)
    for d in range(2):
        for s in range(slabs):
            k = 2 * (d * slabs + s)
            xr_ref[d, s] = final[k]
            xi_ref[d, s] = final[k + 1]

    for d in range(2):
        for s in range(slabs):
            xcat = jnp.concatenate(
                [jnp.concatenate([bur_ref[d, s, pair_rows(j), :], bui_ref[d, s, pair_rows(j), :]],
                                 axis=1).astype(BF16) for j in range(S5_PAIRS)], axis=0)
            r = jnp.dot(xcat, c_ref[d, s], preferred_element_type=F32)
            y = r[:tm]
            for j in range(1, S5_PAIRS):
                y = jnp.where(lane_pair == j, r[j * tm:(j + 1) * tm], y)
            y_refs[d][:, s * S5_SLAB:(s + 1) * S5_SLAB] = y


def s5_scan(proj, u_col0, a_re, a_im, b_mat, c_mat, seq, casts=(), tm=256):
    slabs = a_re.shape[1]
    width = slabs * S5_SLAB
    nchunks = seq // tm
    assert u_col0 % width == 0
    ucol = u_col0 // width
    wspec = pl.BlockSpec((2, slabs, S5_SLAB, S5_SLAB), lambda c: (0, 0, 0, 0))
    aspec = pl.BlockSpec((2, slabs, S5_PAIRS, 2 * B_STATE), lambda c: (0, 0, 0, 0))
    state = pltpu.VMEM((2, slabs, S5_PAIRS, 2 * B_STATE), F32)
    drive = pltpu.VMEM((2, slabs, tm * S5_PITCH, 2 * B_STATE), F32)
    cast_in_specs, cast_out_specs, cast_shapes = _cast_plumbing(casts, nchunks, lambda c: c)
    out = pl.pallas_call(
        _with_casts(functools.partial(_s5_kernel, tm=tm, slabs=slabs), 6, 2, len(casts)),
        grid=(nchunks,),
        in_specs=[pl.BlockSpec((tm, width), lambda c: (c, ucol)),
                  pl.BlockSpec((tm, width), lambda c: (nchunks - 1 - c, ucol)),
                  wspec, wspec, aspec, aspec] + cast_in_specs,
        out_specs=[pl.BlockSpec((tm, width), lambda c: (c, 0)),
                   pl.BlockSpec((tm, width), lambda c: (nchunks - 1 - c, 0))] + cast_out_specs,
        out_shape=[jax.ShapeDtypeStruct((seq, width), F32), jax.ShapeDtypeStruct((seq, width), F32)] + cast_shapes,
        scratch_shapes=[state, state, drive, drive],
        compiler_params=_cparams(("arbitrary",)),
    )(proj, proj, b_mat, c_mat, a_re, a_im, *[arr for arr, _ in casts])
    return out[0], out[1], list(out[2:])


def _s5_out_kernel(oa_ref, yf_ref, yb_ref, u_ref, dsk_ref, wg_ref, w_ref, r_ref, o_ref):
    y = jax.nn.gelu(yf_ref[...] + yb_ref[...] + dsk_ref[...] * u_ref[...])
    z = jnp.dot(y.astype(BF16), wg_ref[...], preferred_element_type=F32)
    ob = (y * jax.nn.sigmoid(z)).astype(BF16)
    ka = oa_ref.shape[1]
    acc = r_ref[...] + jnp.dot(oa_ref[...], w_ref[pl.ds(0, ka), :], preferred_element_type=F32)
    o_ref[...] = acc + jnp.dot(ob, w_ref[pl.ds(ka, ob.shape[1]), :], preferred_element_type=F32)


def s5_glu_out_projection(o_a, yf, yb, proj, u_col0, d_skip, w_glu, glu_layer, w_out, res, tm=512):
    m, k = yf.shape
    ka = o_a.shape[1]
    n = w_out.shape[2]
    row = lambda i: (i, 0)
    return pl.pallas_call(
        _s5_out_kernel,
        grid=(m // tm,),
        in_specs=[
            pl.BlockSpec((tm, ka), row),
            pl.BlockSpec((tm, k), row),
            pl.BlockSpec((tm, k), row),
            pl.BlockSpec((tm, k), lambda i: (i, u_col0 // k)),
            pl.BlockSpec((1, k), lambda i: (0, 0)),
            pl.BlockSpec((None, k, k), lambda i: (glu_layer, 0, 0)),
            pl.BlockSpec((None, ka + k, n), lambda i: (0, 0, 0)),
            pl.BlockSpec((tm, n), row),
        ],
        out_specs=pl.BlockSpec((tm, n), row),
        out_shape=jax.ShapeDtypeStruct((m, n), F32),
        compiler_params=_cparams(("arbitrary",)),
    )(o_a, yf, yb, proj, d_skip.reshape(1, k), w_glu, w_out, res)


def _na_bias_pairs(rpb):
    c = np.arange(GRID_W)
    col_idx = np.clip(c[None, :] - c[:, None] + NA_COLS - 1, 0, 2 * NA_COLS - 2)
    onehot = (col_idx[None] == np.arange(2 * NA_COLS - 1)[:, None, None]).astype(np.float32)
    r2 = jnp.einsum('hdk,kqc->hdqc', rpb.astype(F32), jnp.asarray(onehot), precision=lax.Precision.HIGHEST)
    r2 = jnp.pad(r2, ((0, 0), (4, 4), (0, 0), (0, 0)))
    return jnp.concatenate([r2[:, :-1], r2[:, 1:]], axis=-1)


def _na_mask_tables():
    qr = np.arange(NA_QROWS)[:, None, None, None]
    cq = np.arange(GRID_W)[None, :, None, None]
    kr = np.arange(NA_KROWS)[None, None, :, None]
    ck = np.arange(GRID_W)[None, None, None, :]
    cs = np.clip(cq - NA_COLS // 2, 0, GRID_W - NA_COLS)
    col_ok = (ck >= cs) & (ck < cs + NA_COLS)
    row_ok = [(kr < NA_ROWS) & (qr >= 0),
              (kr >= qr) & (kr < qr + NA_ROWS),
              (kr >= NA_KROWS - NA_ROWS) & (qr >= 0)]
    out = [np.where(r & col_ok, 0.0, NEG_INF).reshape(NA_QROWS * GRID_W, NA_KROWS * GRID_W) for r in row_ok]
    return np.stack(out).astype(np.float32)


def _natten_kernel(q_ref, k_ref, v_ref, pair_ref, mask_ref, o_ref, bias_ref, *, groups, rows):
    gs = pl.program_id(1)
    nk = NA_KROWS * GRID_W
    tq = NA_QROWS * GRID_W

    @pl.when(gs == 0)
    def _():
        for hh in range(NA_HEADS_PER_STEP):
            for v in range(3):
                for qr in range(NA_QROWS):
                    q_rows = pl.ds(qr * GRID_W, GRID_W)
                    for kr2 in range(NA_KROWS // 2):
                        k_cols = pl.ds(kr2 * 128, 128)
                        bias_ref[hh, v, q_rows, k_cols] = (
                            pair_ref[hh, 2 * (kr2 + 2 * (2 - v)) - qr + 3] + mask_ref[v, q_rows, k_cols]) * LOG2E

    for gi in range(NA_GROUPS_PER_STEP):
        g = gs * NA_GROUPS_PER_STEP + gi
        q_rows = pl.ds(gi * tq, tq)
        key_row0 = jnp.clip(NA_QROWS * g - NA_ROWS // 2, 0, rows - NA_KROWS)
        tok0 = pl.multiple_of(key_row0 * GRID_W, GRID_W)
        variant = jnp.where(g == 0, 0, jnp.where(g == groups - 1, 2, 1))
        for hh in range(NA_HEADS_PER_STEP):
            cols = pl.ds(hh * HEAD_DIM, HEAD_DIM)
            s = lax.dot_general(q_ref[q_rows, cols], k_ref[pl.ds(tok0, nk), cols], (((1,), (1,)), ((), ())),
                                preferred_element_type=F32)
            s = s + bias_ref[hh, variant]
            m = jnp.max(s, axis=1, keepdims=True)
            p = jnp.exp2(s - m)
            l = jnp.sum(p, axis=1, keepdims=True)
            o = jnp.dot(p.astype(BF16), v_ref[pl.ds(tok0, nk), cols], preferred_element_type=F32)
            o_ref[q_rows, cols] = (o / l).astype(o_ref.dtype)


def neighbourhood_attention(qkv, bias_pairs, mask_tabs, seq, heads, casts=()):
    rows = seq // GRID_W
    assert rows % (NA_QROWS * NA_GROUPS_PER_STEP) == 0 and rows >= NA_KROWS + NA_QROWS
    assert heads % NA_HEADS_PER_STEP == 0
    groups = rows // NA_QROWS
    tq = NA_QROWS * GRID_W * NA_GROUPS_PER_STEP
    hw = NA_HEADS_PER_STEP * HEAD_DIM
    hsteps = heads // NA_HEADS_PER_STEP
    gsteps = groups // NA_GROUPS_PER_STEP
    cast_in_specs, cast_out_specs, cast_shapes = _cast_plumbing(casts, hsteps * gsteps, lambda h, g: h * gsteps + g)
    out = pl.pallas_call(
        _with_casts(functools.partial(_natten_kernel, groups=groups, rows=rows), 5, 1, len(casts)),
        grid=(hsteps, gsteps),
        in_specs=[
            pl.BlockSpec((tq, hw), lambda h, g: (g, h)),
            pl.BlockSpec((seq, hw), lambda h, g: (0, hsteps + h)),
            pl.BlockSpec((seq, hw), lambda h, g: (0, 2 * hsteps + h)),
            pl.BlockSpec((NA_HEADS_PER_STEP,) + bias_pairs.shape[1:], lambda h, g: (h, 0, 0, 0)),
            pl.BlockSpec(mask_tabs.shape, lambda h, g: (0, 0, 0)),
        ] + cast_in_specs,
        out_specs=[pl.BlockSpec((tq, hw), lambda h, g: (g, h))] + cast_out_specs,
        out_shape=[jax.ShapeDtypeStruct((seq, heads * HEAD_DIM), BF16)] + cast_shapes,
        scratch_shapes=[pltpu.VMEM((NA_HEADS_PER_STEP, 3, NA_QROWS * GRID_W, NA_KROWS * GRID_W), F32)],
        compiler_params=_cparams(("arbitrary", "arbitrary")),
    )(qkv, qkv, qkv, bias_pairs, mask_tabs, *[arr for arr, _ in casts])
    return out[0], list(out[1:])


def kernel(x, t5_bias, ab_w_in, ab_w_out, s5_lam_re, s5_lam_im, s5_log_step, s5_b_re, s5_b_im, s5_c_re, s5_c_im, s5_d, s5_w_glu, c_w_qkv, c_w_out, c_rpb, norm_mix, norm_mlp, mlp_w1, mlp_w2, norm_final):
    batch, seq, d_model = x.shape
    depth = norm_mix.shape[0]
    c_heads = c_rpb.shape[1]
    t5_buckets = _t5_bucket_tables()
    na_mask = jnp.asarray(_na_mask_tables())
    attn_scale = LOG2E / math.sqrt(HEAD_DIM)
    w_glu = s5_w_glu.astype(BF16)

    def mixer_weights(i):
        return ((ab_w_in, ab_w_out) if i % 2 == 0 else (c_w_qkv, c_w_out)), i // 2

    (mix_in0, mix_out0), _ = mixer_weights(0)
    mix_bf = {0: [mix_in0[:1].astype(BF16), mix_out0[:1].astype(BF16)]}
    outs = []
    for bi in range(batch):
        h = x[bi]
        for i in range(depth):
            j = i // 2
            w_mix_in, w_mix_out = mix_bf[i]
            if i % 2 == 0:
                proj = norm_matmul(h, norm_mix[i], w_mix_in, 0, F32)
                o_a, _ = dilated_attention(proj, t5_bias, t5_buckets, seq)
                a_re, a_im, b_mat, c_mat = _s5_discretise(
                    s5_lam_re[j], s5_lam_im[j], s5_log_step[j], s5_b_re[j], s5_b_im[j], s5_c_re[j], s5_c_im[j])
                yf, yb, (w1, w2) = s5_scan(proj, 3 * A_WIDTH, a_re, a_im, b_mat, c_mat, seq,
                                           casts=[(mlp_w1, i), (mlp_w2, i)])
                h = s5_glu_out_projection(o_a, yf, yb, proj, 3 * A_WIDTH, s5_d[j], w_glu, j, w_mix_out, h)
            else:
                qkv = norm_matmul(h, norm_mix[i], w_mix_in, 0, BF16, scaled_cols=c_heads * HEAD_DIM, scale=attn_scale)
                o, (w1, w2) = neighbourhood_attention(qkv, _na_bias_pairs(c_rpb[j]), na_mask, seq, c_heads,
                                                      casts=[(mlp_w1, i), (mlp_w2, i)])
                h = matmul_residual([o], w_mix_out, 0, h)
            casts = []
            if i + 1 < depth and i + 1 not in mix_bf:
                (nxt_in, nxt_out), jn = mixer_weights(i + 1)
                casts = [(nxt_in, jn), (nxt_out, jn)]
            h, cast_out = mlp(h, norm_mlp[i], w1, w2, 0, casts)
            if casts:
                mix_bf[i + 1] = cast_out
        outs.append(rmsnorm(h, norm_final))
    return jnp.stack(outs)
```

```python
import functools
import math

import numpy as np
import jax
import jax.numpy as jnp
from jax import lax
from jax.experimental import pallas as pl
from jax.experimental.pallas import tpu as pltpu

F32 = jnp.float32
BF16 = jnp.bfloat16

HEAD_DIM = 128
A_HEADS = 8
A_WIDTH = A_HEADS * HEAD_DIM
DILATED_BRANCHES = ((128, 1), (512, 4), (2048, 16))
A_QBLOCK = 128
A_HALF = 64
A_KBLOCK = A_QBLOCK + 2 * A_HALF
A_SUPER = 2048
B_GROUP = 16
B_STATE = 64
S5_SLAB = 256
S5_PAIRS = 8
S5_PITCH = 9
GRID_W = 64
NA_ROWS = 8
NA_COLS = 16
NA_QROWS = 4
NA_KROWS = 12
NA_EXT = NA_KROWS + 8
NA_HEADS_PER_STEP = 2
NA_GROUPS_PER_STEP = 4
T5_BUCKETS = 32
T5_MAX_DISTANCE = 1024
RMS_EPS = 1e-6
NEG_INF = -1e30
LOG2E = math.log2(math.e)
VMEM_LIMIT = 56 * 1024 * 1024


def _cparams(sem):
    return pltpu.CompilerParams(dimension_semantics=sem, vmem_limit_bytes=VMEM_LIMIT)


def _rms_rows(x, g):
    y = x * lax.rsqrt(jnp.mean(x * x, axis=-1, keepdims=True) + RMS_EPS)
    return y * g


def _with_casts(body, n_in, n_out, ncast):
    def wrapped(*refs):
        ins = refs[:n_in]
        cast_in = refs[n_in:n_in + ncast]
        outs = refs[n_in + ncast:n_in + ncast + n_out]
        cast_out = refs[n_in + ncast + n_out:n_in + 2 * ncast + n_out]
        for src, dst in zip(cast_in, cast_out):
            dst[...] = src[...].astype(BF16)
        body(*ins, *outs, *refs[n_in + 2 * ncast + n_out:])
    return wrapped


def _cast_plumbing(casts, nblocks, block_id):
    in_specs, out_specs, shapes = [], [], []
    for arr, layer in casts:
        _, rows, cols = arr.shape
        assert rows % (16 * nblocks) == 0
        blk = (None, rows // nblocks, cols)
        in_specs.append(pl.BlockSpec(blk, lambda *g, layer=layer: (layer, block_id(*g), 0)))
        out_specs.append(pl.BlockSpec(blk, lambda *g: (0, block_id(*g), 0)))
        shapes.append(jax.ShapeDtypeStruct((1, rows, cols), BF16))
    return in_specs, out_specs, shapes


def _serpentine(i, j, n):
    return jnp.where(i % 2 == 0, j, n - 1 - j)


def _norm_matmul_kernel(x_ref, g_ref, w_ref, o_ref, xn_ref, *, col_blocks, scaled_blocks, scale):
    j = pl.program_id(1)

    @pl.when(j == 0)
    def _():
        xn_ref[...] = _rms_rows(x_ref[...], g_ref[...]).astype(BF16)

    acc = jnp.dot(xn_ref[...], w_ref[...], preferred_element_type=F32)
    if scaled_blocks:
        acc = acc * jnp.where(_serpentine(pl.program_id(0), j, col_blocks) < scaled_blocks, scale, 1.0)
    o_ref[...] = acc.astype(o_ref.dtype)


def norm_matmul(x, g, w, layer, out_dtype, scaled_cols=0, scale=1.0, tm=1024, tn=1024):
    m, k = x.shape
    n = w.shape[2]
    assert scaled_cols % tn == 0
    nj = n // tn
    return pl.pallas_call(
        functools.partial(_norm_matmul_kernel, col_blocks=nj, scaled_blocks=scaled_cols // tn, scale=scale),
        grid=(m // tm, nj),
        in_specs=[
            pl.BlockSpec((tm, k), lambda i, j: (i, 0)),
            pl.BlockSpec((1, k), lambda i, j: (0, 0)),
            pl.BlockSpec((None, k, tn), lambda i, j: (layer, 0, _serpentine(i, j, nj))),
        ],
        out_specs=pl.BlockSpec((tm, tn), lambda i, j: (i, _serpentine(i, j, nj))),
        out_shape=jax.ShapeDtypeStruct((m, n), out_dtype),
        scratch_shapes=[pltpu.VMEM((tm, k), BF16)],
        compiler_params=_cparams(("arbitrary", "arbitrary")),
    )(x, g.reshape(1, k), w)


def _matmul_residual_kernel(a_ref, w_ref, r_ref, g_ref, o_ref, on_ref):
    acc = r_ref[...] + jnp.dot(a_ref[...], w_ref[...], preferred_element_type=F32)
    o_ref[...] = acc
    on_ref[...] = _rms_rows(acc, g_ref[...]).astype(BF16)


def matmul_residual(a, w, layer, res, g, tm=512):
    m = res.shape[0]
    _, k, n = w.shape
    row = lambda i: (i, 0)
    return pl.pallas_call(
        _matmul_residual_kernel,
        grid=(m // tm,),
        in_specs=[
            pl.BlockSpec((tm, k), row),
            pl.BlockSpec((None, k, n), lambda i: (layer, 0, 0)),
            pl.BlockSpec((tm, n), row),
            pl.BlockSpec((1, n), lambda i: (0, 0)),
        ],
        out_specs=[pl.BlockSpec((tm, n), row), pl.BlockSpec((tm, n), row)],
        out_shape=[jax.ShapeDtypeStruct((m, n), F32), jax.ShapeDtypeStruct((m, n), BF16)],
        compiler_params=_cparams(("arbitrary",)),
    )(a, w, res, g.reshape(1, n))


def _mlp_kernel(xn_ref, xres_ref, w1_ref, w2_ref, o_ref, h_ref, *, up_steps, tf):
    i = pl.program_id(0)
    s = pl.program_id(1)

    @pl.when(s < up_steps)
    def _():
        h = jnp.dot(xn_ref[...], w1_ref[...], preferred_element_type=F32)
        h_ref[_serpentine(i, s, up_steps)] = jnp.square(jnp.maximum(h, 0.0)).astype(BF16)

    @pl.when(s >= up_steps)
    def _():
        acc = xres_ref[...]
        for c in range(up_steps):
            acc = acc + jnp.dot(h_ref[c], w2_ref[pl.ds(c * tf, tf), :], preferred_element_type=F32)
        o_ref[...] = acc


def mlp(x, xn, w1, w2, layer, casts=(), tm=1024, tf=512, tn=256):
    m, d = x.shape
    dff = w1.shape[2]
    up_steps = dff // tf
    down_steps = d // tn
    cast_in_specs, cast_out_specs, cast_shapes = _cast_plumbing(
        casts, (m // tm) * up_steps, lambda i, s: i * up_steps + jnp.minimum(s, up_steps - 1))

    def up_block(i, s):
        return _serpentine(i, jnp.minimum(s, up_steps - 1), up_steps)

    def down_block(i, s):
        return _serpentine(i, jnp.maximum(s - up_steps, 0), down_steps)

    out = pl.pallas_call(
        _with_casts(functools.partial(_mlp_kernel, up_steps=up_steps, tf=tf), 4, 1, len(casts)),
        grid=(m // tm, up_steps + down_steps),
        in_specs=[
            pl.BlockSpec((tm, d), lambda i, s: (i, 0)),
            pl.BlockSpec((tm, tn), lambda i, s: (i, down_block(i, s))),
            pl.BlockSpec((None, d, tf), lambda i, s: (layer, 0, up_block(i, s))),
            pl.BlockSpec((None, dff, tn), lambda i, s: (layer, 0, down_block(i, s))),
        ] + cast_in_specs,
        out_specs=[pl.BlockSpec((tm, tn), lambda i, s: (i, down_block(i, s)))] + cast_out_specs,
        out_shape=[jax.ShapeDtypeStruct((m, d), F32)] + cast_shapes,
        scratch_shapes=[pltpu.VMEM((up_steps, tm, tf), BF16)],
        compiler_params=_cparams(("arbitrary", "arbitrary")),
    )(xn, x, w1, w2, *[arr for arr, _ in casts])
    return out[0], list(out[1:])


def _rmsnorm_kernel(x_ref, g_ref, o_ref):
    o_ref[...] = _rms_rows(x_ref[...], g_ref[...])


def rmsnorm(x, g, tm=512):
    m, d = x.shape
    return pl.pallas_call(
        _rmsnorm_kernel,
        grid=(m // tm,),
        in_specs=[pl.BlockSpec((tm, d), lambda i: (i, 0)), pl.BlockSpec((1, d), lambda i: (0, 0))],
        out_specs=pl.BlockSpec((tm, d), lambda i: (i, 0)),
        out_shape=jax.ShapeDtypeStruct((m, d), F32),
        compiler_params=_cparams(("arbitrary",)),
    )(x, g.reshape(1, d))


def _t5_bucket(rel):
    half = T5_BUCKETS // 2
    max_exact = half // 2
    n = jnp.abs(rel)
    nf = jnp.maximum(n, 1).astype(F32)
    large = max_exact + (jnp.log(nf / max_exact) / math.log(T5_MAX_DISTANCE / max_exact)
                         * (half - max_exact)).astype(jnp.int32)
    large = jnp.minimum(large, half - 1)
    return jnp.where(rel > 0, half, 0) + jnp.where(n < max_exact, n, large)


def _t5_bucket_tables():
    tabs = []
    for _, dil in DILATED_BRANCHES:
        off = jnp.arange(A_KBLOCK)[None, :] - A_HALF - jnp.arange(A_QBLOCK)[:, None]
        tabs.append(jnp.where(jnp.abs(off) <= A_HALF, _t5_bucket(off * dil), T5_BUCKETS))
    return jnp.stack(tabs).astype(jnp.int32)


def _dilated_sections(seq):
    bases, sizes, row = [], [], 0
    for _, dil in DILATED_BRANCHES:
        sec = seq // dil + 2 * A_HALF
        bases.append(row)
        sizes.append(sec)
        row += dil * sec
    return bases, sizes, row


def _dilated_kernel(t5_ref, q_ref, k_ref, v_ref, bucket_ref, o_ref,
                    kd_ref, vd_ref, tmp_ref, bias_ref, m_ref, l_ref, n_ref, *, seq):
    h = pl.program_id(0)
    t = pl.program_id(1)
    scale = LOG2E / math.sqrt(HEAD_DIM)
    bases, sizes, _ = _dilated_sections(seq)

    @pl.when(t == 0)
    def _():
        for b in range(len(DILATED_BRANCHES)):
            bk = bucket_ref[b]
            acc = jnp.full((A_QBLOCK, A_KBLOCK), NEG_INF, F32)
            for kbkt in range(T5_BUCKETS):
                acc = jnp.where(bk == kbkt, t5_ref[kbkt * A_HEADS + h], acc)
            bias_ref[b] = acc * LOG2E
        zeros = jnp.zeros((A_HALF, HEAD_DIM), BF16)
        (_, d0), (_, d1), (_, d2) = DILATED_BRANCHES
        assert d0 == 1 and d2 % d1 == 0
        ratio = d2 // d1

        def put(dst, b, r, rows):
            o = bases[b] + r * sizes[b]
            dst[pl.ds(o, A_HALF), :] = zeros
            dst[pl.ds(o + A_HALF + rows.shape[0], A_HALF), :] = zeros
            dst[pl.ds(o + A_HALF, rows.shape[0]), :] = rows.astype(BF16)

        for src, dst in ((k_ref, kd_ref), (v_ref, vd_ref)):
            put(dst, 0, 0, src[...])
            for r1 in range(d1):
                rows1 = src[pl.ds(r1, seq // d1, stride=d1), :]
                put(dst, 1, r1, rows1)
                tmp_ref[...] = rows1
                for a in range(ratio):
                    put(dst, 2, a * d1 + r1, tmp_ref[pl.ds(a, seq // d2, stride=ratio), :])

    kk = lax.broadcasted_iota(jnp.int32, (1, A_KBLOCK), 1)
    for b, (_, dil) in enumerate(DILATED_BRANCHES):
        sub_len = seq // dil
        blocks_per_residue = A_SUPER // dil // A_QBLOCK

        def body(idx, carry, b=b, dil=dil, sub_len=sub_len, blocks_per_residue=blocks_per_residue):
            r = idx // blocks_per_residue
            n = idx % blocks_per_residue
            qs = r + n * (A_QBLOCK * dil)
            q_idx = pl.ds(qs, A_QBLOCK) if dil == 1 else pl.ds(qs, A_QBLOCK, stride=dil)
            blk = t * blocks_per_residue + n
            k_idx = pl.ds(pl.multiple_of(bases[b] + r * sizes[b] + blk * A_QBLOCK, A_QBLOCK), A_KBLOCK)
            qb = (q_ref[q_idx, :] * scale).astype(BF16)
            s = lax.dot_general(qb, kd_ref[k_idx, :], (((1,), (1,)), ((), ())), preferred_element_type=F32)
            key_l = blk * A_QBLOCK - A_HALF + kk
            edge = jnp.where((key_l >= 0) & (key_l < sub_len), 0.0, NEG_INF)
            s = s + bias_ref[b] + edge
            m = jnp.max(s, axis=1, keepdims=True)
            p = jnp.exp2(s - m)
            l = jnp.sum(p, axis=1, keepdims=True)
            num = jnp.dot(p.astype(BF16), vd_ref[k_idx, :], preferred_element_type=F32)
            m_ref[b, q_idx, :] = jnp.broadcast_to(m, (A_QBLOCK, HEAD_DIM))
            l_ref[b, q_idx, :] = jnp.broadcast_to(l, (A_QBLOCK, HEAD_DIM))
            n_ref[b, q_idx, :] = num
            return carry

        lax.fori_loop(0, A_SUPER // A_QBLOCK, body, 0, unroll=16)

    rows = 256
    for c in range(A_SUPER // rows):
        sl = pl.ds(c * rows, rows)
        m0, m1, m2 = m_ref[0, sl, :], m_ref[1, sl, :], m_ref[2, sl, :]
        mx = jnp.maximum(jnp.maximum(m0, m1), m2)
        w0, w1, w2 = jnp.exp2(m0 - mx), jnp.exp2(m1 - mx), jnp.exp2(m2 - mx)
        num = w0 * n_ref[0, sl, :] + w1 * n_ref[1, sl, :] + w2 * n_ref[2, sl, :]
        den = w0 * l_ref[0, sl, :] + w1 * l_ref[1, sl, :] + w2 * l_ref[2, sl, :]
        o_ref[sl, :] = (num / den).astype(o_ref.dtype)


def dilated_attention(proj, t5_table, bucket_tabs, seq, casts=()):
    assert seq % A_SUPER == 0
    nb = len(DILATED_BRANCHES)
    total_rows = _dilated_sections(seq)[2]
    tiles = seq // A_SUPER
    cast_in_specs, cast_out_specs, cast_shapes = _cast_plumbing(casts, A_HEADS * tiles, lambda h, t: h * tiles + t)
    out = pl.pallas_call(
        _with_casts(functools.partial(_dilated_kernel, seq=seq), 5, 1, len(casts)),
        grid=(A_HEADS, tiles),
        in_specs=[
            pl.BlockSpec(memory_space=pltpu.SMEM),
            pl.BlockSpec((A_SUPER, HEAD_DIM), lambda h, t: (t, h)),
            pl.BlockSpec((seq, HEAD_DIM), lambda h, t: (0, A_HEADS + h)),
            pl.BlockSpec((seq, HEAD_DIM), lambda h, t: (0, 2 * A_HEADS + h)),
            pl.BlockSpec((nb, A_QBLOCK, A_KBLOCK), lambda h, t: (0, 0, 0)),
        ] + cast_in_specs,
        out_specs=[pl.BlockSpec((A_SUPER, HEAD_DIM), lambda h, t: (t, h))] + cast_out_specs,
        out_shape=[jax.ShapeDtypeStruct((seq, A_WIDTH), BF16)] + cast_shapes,
        scratch_shapes=[
            pltpu.VMEM((total_rows, HEAD_DIM), BF16),
            pltpu.VMEM((total_rows, HEAD_DIM), BF16),
            pltpu.VMEM((seq // DILATED_BRANCHES[1][1], HEAD_DIM), F32),
            pltpu.VMEM((nb, A_QBLOCK, A_KBLOCK), F32),
            pltpu.VMEM((nb, A_SUPER, HEAD_DIM), F32),
            pltpu.VMEM((nb, A_SUPER, HEAD_DIM), F32),
            pltpu.VMEM((nb, A_SUPER, HEAD_DIM), F32),
        ],
        compiler_params=_cparams(("arbitrary", "arbitrary")),
    )(t5_table.astype(F32).reshape(-1), proj, proj, proj, bucket_tabs, *[arr for arr, _ in casts])
    return out[0], list(out[1:])


def _s5_discretise(lam_re, lam_im, log_step, b_re, b_im, c_re, c_im):
    g, p = lam_re.shape[1:]
    slabs = g // (2 * S5_PAIRS)
    step = jnp.exp(log_step.astype(F32))[..., None]
    lr = jnp.minimum(lam_re.astype(F32), -1e-4)
    li = lam_im.astype(F32)
    mag = jnp.exp(lr * step)
    ab_re = mag * jnp.cos(li * step)
    ab_im = mag * jnp.sin(li * step)
    den = lr * lr + li * li
    zr = ((ab_re - 1.0) * lr + ab_im * li) / den
    zi = (ab_im * lr - (ab_re - 1.0) * li) / den
    br = b_re.astype(F32)[None]
    bi = b_im.astype(F32)[None]
    bb_re = zr[..., None] * br - zi[..., None] * bi
    bb_im = zr[..., None] * bi + zi[..., None] * br

    a_re = ab_re.reshape(2, slabs, S5_PAIRS, 2 * p)
    a_im = ab_im.reshape(2, slabs, S5_PAIRS, 2 * p)

    eye2 = jnp.eye(2, dtype=F32)

    def pack_b(bb):
        bb = bb.reshape(2, slabs, S5_PAIRS, 2, p, B_GROUP)
        return jnp.einsum('dsjepc,ef->dsjecfp', bb, eye2).reshape(2, slabs, S5_SLAB, 2 * p)

    def pack_c(cc):
        cc = cc.astype(F32).reshape(2, slabs, S5_PAIRS, 2, B_GROUP, p)
        return jnp.einsum('dsjecp,ef->dsfpjec', cc, eye2).reshape(2, slabs, 2 * p, S5_SLAB)

    b_mat = jnp.concatenate([pack_b(bb_re), pack_b(bb_im)], axis=-1)
    c_mat = jnp.concatenate([pack_c(c_re), -pack_c(c_im)], axis=-2)
    return a_re, a_im, b_mat.astype(BF16), c_mat.astype(BF16)


def _s5_kernel(uf_ref, ub_ref, b_ref, c_ref, are_ref, aim_ref, yf_ref, yb_ref,
               xr_ref, xi_ref, bur_ref, bui_ref, *, tm, slabs):
    c = pl.program_id(0)
    nstate = 2 * B_STATE
    u_refs = (uf_ref, ub_ref)
    y_refs = (yf_ref, yb_ref)

    @pl.when(c == 0)
    def _():
        xr_ref[...] = jnp.zeros_like(xr_ref)
        xi_ref[...] = jnp.zeros_like(xi_ref)

    lane_pair = lax.broadcasted_iota(jnp.int32, (1, S5_SLAB), 1) // (2 * B_GROUP)

    def pair_rows(j):
        return pl.ds(j, tm, stride=S5_PITCH)

    for d in range(2):
        for s in range(slabs):
            u16 = u_refs[d][:, s * S5_SLAB:(s + 1) * S5_SLAB].astype(BF16)
            lhs = jnp.concatenate([jnp.where(lane_pair == j, u16, jnp.zeros_like(u16))
                                   for j in range(S5_PAIRS)], axis=0)
            bu = jnp.dot(lhs, b_ref[d, s], preferred_element_type=F32)
            for j in range(S5_PAIRS):
                bur_ref[d, s, pair_rows(j), :] = bu[j * tm:(j + 1) * tm, :nstate]
                bui_ref[d, s, pair_rows(j), :] = bu[j * tm:(j + 1) * tm, nstate:]

    ar = [[are_ref[d, s] for s in range(slabs)] for d in range(2)]
    ai = [[aim_ref[d, s] for s in range(slabs)] for d in range(2)]

    def step(i, carry):
        new = []
        for d in range(2):
            t = i if d == 0 else tm - 1 - i
            start = t * S5_PITCH
            rows = pl.ds(pl.multiple_of(start, 8) if S5_PITCH % 8 == 0 else start, S5_PAIRS)
            for s in range(slabs):
                k = 2 * (d * slabs + s)
                xr, xi = carry[k], carry[k + 1]
                nxr = ar[d][s] * xr - ai[d][s] * xi + bur_ref[d, s, rows, :]
                nxi = ar[d][s] * xi + ai[d][s] * xr + bui_ref[d, s, rows, :]
                bur_ref[d, s, rows, :] = nxr
                bui_ref[d, s, rows, :] = nxi
                new += [nxr, nxi]
        return tuple(new)

    init = []
    for d in range(2):
        for s in range(slabs):
            init += [xr_ref[d, s], xi_ref[d, s]]
    final = tuple(init)
    for i in range(tm):
        final = step(i, final)
    for d in range(2):
        for s in range(slabs):
            k = 2 * (d * slabs + s)
            xr_ref[d, s] = final[k]
            xi_ref[d, s] = final[k + 1]

    for d in range(2):
        for s in range(slabs):
            xcat = jnp.concatenate(
                [jnp.concatenate([bur_ref[d, s, pair_rows(j), :], bui_ref[d, s, pair_rows(j), :]],
                                 axis=1).astype(BF16) for j in range(S5_PAIRS)], axis=0)
            r = jnp.dot(xcat, c_ref[d, s], preferred_element_type=F32)
            y = r[:tm]
            for j in range(1, S5_PAIRS):
                y = jnp.where(lane_pair == j, r[j * tm:(j + 1) * tm], y)
            y_refs[d][:, s * S5_SLAB:(s + 1) * S5_SLAB] = y


def s5_scan(proj, u_col0, a_re, a_im, b_mat, c_mat, seq, casts=(), tm=256):
    slabs = a_re.shape[1]
    width = slabs * S5_SLAB
    nchunks = seq // tm
    assert u_col0 % width == 0
    ucol = u_col0 // width
    wspec = pl.BlockSpec((2, slabs, S5_SLAB, S5_SLAB), lambda c: (0, 0, 0, 0))
    aspec = pl.BlockSpec((2, slabs, S5_PAIRS, 2 * B_STATE), lambda c: (0, 0, 0, 0))
    state = pltpu.VMEM((2, slabs, S5_PAIRS, 2 * B_STATE), F32)
    drive = pltpu.VMEM((2, slabs, tm * S5_PITCH, 2 * B_STATE), F32)
    cast_in_specs, cast_out_specs, cast_shapes = _cast_plumbing(casts, nchunks, lambda c: c)
    out = pl.pallas_call(
        _with_casts(functools.partial(_s5_kernel, tm=tm, slabs=slabs), 6, 2, len(casts)),
        grid=(nchunks,),
        in_specs=[pl.BlockSpec((tm, width), lambda c: (c, ucol)),
                  pl.BlockSpec((tm, width), lambda c: (nchunks - 1 - c, ucol)),
                  wspec, wspec, aspec, aspec] + cast_in_specs,
        out_specs=[pl.BlockSpec((tm, width), lambda c: (c, 0)),
                   pl.BlockSpec((tm, width), lambda c: (nchunks - 1 - c, 0))] + cast_out_specs,
        out_shape=[jax.ShapeDtypeStruct((seq, width), F32), jax.ShapeDtypeStruct((seq, width), F32)] + cast_shapes,
        scratch_shapes=[state, state, drive, drive],
        compiler_params=_cparams(("arbitrary",)),
    )(proj, proj, b_mat, c_mat, a_re, a_im, *[arr for arr, _ in casts])
    return out[0], out[1], list(out[2:])


def _s5_out_kernel(oa_ref, yf_ref, yb_ref, u_ref, dsk_ref, wg_ref, w_ref, r_ref, g_ref, o_ref, on_ref):
    y = jax.nn.gelu(yf_ref[...] + yb_ref[...] + dsk_ref[...] * u_ref[...])
    z = jnp.dot(y.astype(BF16), wg_ref[...], preferred_element_type=F32)
    ob = (y * jax.nn.sigmoid(z)).astype(BF16)
    ka = oa_ref.shape[1]
    acc = r_ref[...] + jnp.dot(oa_ref[...], w_ref[pl.ds(0, ka), :], preferred_element_type=F32)
    acc = acc + jnp.dot(ob, w_ref[pl.ds(ka, ob.shape[1]), :], preferred_element_type=F32)
    o_ref[...] = acc
    on_ref[...] = _rms_rows(acc, g_ref[...]).astype(BF16)


def s5_glu_out_projection(o_a, yf, yb, proj, u_col0, d_skip, w_glu, glu_layer, w_out, res, g, tm=512):
    m, k = yf.shape
    ka = o_a.shape[1]
    n = w_out.shape[2]
    row = lambda i: (i, 0)
    return pl.pallas_call(
        _s5_out_kernel,
        grid=(m // tm,),
        in_specs=[
            pl.BlockSpec((tm, ka), row),
            pl.BlockSpec((tm, k), row),
            pl.BlockSpec((tm, k), row),
            pl.BlockSpec((tm, k), lambda i: (i, u_col0 // k)),
            pl.BlockSpec((1, k), lambda i: (0, 0)),
            pl.BlockSpec((None, k, k), lambda i: (glu_layer, 0, 0)),
            pl.BlockSpec((None, ka + k, n), lambda i: (0, 0, 0)),
            pl.BlockSpec((tm, n), row),
            pl.BlockSpec((1, n), lambda i: (0, 0)),
        ],
        out_specs=[pl.BlockSpec((tm, n), row), pl.BlockSpec((tm, n), row)],
        out_shape=[jax.ShapeDtypeStruct((m, n), F32), jax.ShapeDtypeStruct((m, n), BF16)],
        compiler_params=_cparams(("arbitrary",)),
    )(o_a, yf, yb, proj, d_skip.reshape(1, k), w_glu, w_out, res, g.reshape(1, n))


def _na_bias_pairs(rpb):
    c = np.arange(GRID_W)
    col_idx = np.clip(c[None, :] - c[:, None] + NA_COLS - 1, 0, 2 * NA_COLS - 2)
    onehot = (col_idx[None] == np.arange(2 * NA_COLS - 1)[:, None, None]).astype(np.float32)
    r2 = jnp.einsum('hdk,kqc->hdqc', rpb.astype(F32), jnp.asarray(onehot), precision=lax.Precision.HIGHEST)
    r2 = jnp.pad(r2, ((0, 0), (4, 4), (0, 0), (0, 0)))
    return jnp.concatenate([r2[:, :-1], r2[:, 1:]], axis=-1)


def _na_mask_tables():
    qr = np.arange(NA_QROWS)[:, None, None, None]
    cq = np.arange(GRID_W)[None, :, None, None]
    kr = np.arange(NA_KROWS)[None, None, :, None]
    ck = np.arange(GRID_W)[None, None, None, :]
    cs = np.clip(cq - NA_COLS // 2, 0, GRID_W - NA_COLS)
    col_ok = (ck >= cs) & (ck < cs + NA_COLS)
    row_ok = [(kr < NA_ROWS) & (qr >= 0),
              (kr >= qr) & (kr < qr + NA_ROWS),
              (kr >= NA_KROWS - NA_ROWS) & (qr >= 0)]
    out = [np.where(r & col_ok, 0.0, NEG_INF).reshape(NA_QROWS * GRID_W, NA_KROWS * GRID_W) for r in row_ok]
    return np.stack(out).astype(np.float32)


def _natten_kernel(q_ref, k_ref, v_ref, pair_ref, mask_ref, o_ref, bias_ref, *, groups, rows):
    gs = pl.program_id(1)
    nk = NA_KROWS * GRID_W
    tq = NA_QROWS * GRID_W

    @pl.when(gs == 0)
    def _():
        for hh in range(NA_HEADS_PER_STEP):
            for v in range(3):
                for qr in range(NA_QROWS):
                    q_rows = pl.ds(qr * GRID_W, GRID_W)
                    for kr2 in range(NA_KROWS // 2):
                        k_cols = pl.ds(kr2 * 128, 128)
                        bias_ref[hh, v, q_rows, k_cols] = (
                            pair_ref[hh, 2 * (kr2 + 2 * (2 - v)) - qr + 3] + mask_ref[v, q_rows, k_cols]) * LOG2E

    for gi in range(NA_GROUPS_PER_STEP):
        g = gs * NA_GROUPS_PER_STEP + gi
        q_rows = pl.ds(gi * tq, tq)
        key_row0 = jnp.clip(NA_QROWS * g - NA_ROWS // 2, 0, rows - NA_KROWS)
        tok0 = pl.multiple_of(key_row0 * GRID_W, GRID_W)
        variant = jnp.where(g == 0, 0, jnp.where(g == groups - 1, 2, 1))
        for hh in range(NA_HEADS_PER_STEP):
            cols = pl.ds(hh * HEAD_DIM, HEAD_DIM)
            s = lax.dot_general(q_ref[q_rows, cols], k_ref[pl.ds(tok0, nk), cols], (((1,), (1,)), ((), ())),
                                preferred_element_type=F32)
            s = s + bias_ref[hh, variant]
            m = jnp.max(s, axis=1, keepdims=True)
            p = jnp.exp2(s - m)
            l = jnp.sum(p, axis=1, keepdims=True)
            o = jnp.dot(p.astype(BF16), v_ref[pl.ds(tok0, nk), cols], preferred_element_type=F32)
            o_ref[q_rows, cols] = (o / l).astype(o_ref.dtype)


def neighbourhood_attention(qkv, bias_pairs, mask_tabs, seq, heads, casts=()):
    rows = seq // GRID_W
    assert rows % (NA_QROWS * NA_GROUPS_PER_STEP) == 0 and rows >= NA_KROWS + NA_QROWS
    assert heads % NA_HEADS_PER_STEP == 0
    groups = rows // NA_QROWS
    tq = NA_QROWS * GRID_W * NA_GROUPS_PER_STEP
    hw = NA_HEADS_PER_STEP * HEAD_DIM
    hsteps = heads // NA_HEADS_PER_STEP
    gsteps = groups // NA_GROUPS_PER_STEP
    cast_in_specs, cast_out_specs, cast_shapes = _cast_plumbing(casts, hsteps * gsteps, lambda h, g: h * gsteps + g)
    out = pl.pallas_call(
        _with_casts(functools.partial(_natten_kernel, groups=groups, rows=rows), 5, 1, len(casts)),
        grid=(hsteps, gsteps),
        in_specs=[
            pl.BlockSpec((tq, hw), lambda h, g: (g, h)),
            pl.BlockSpec((seq, hw), lambda h, g: (0, hsteps + h)),
            pl.BlockSpec((seq, hw), lambda h, g: (0, 2 * hsteps + h)),
            pl.BlockSpec((NA_HEADS_PER_STEP,) + bias_pairs.shape[1:], lambda h, g: (h, 0, 0, 0)),
            pl.BlockSpec(mask_tabs.shape, lambda h, g: (0, 0, 0)),
        ] + cast_in_specs,
        out_specs=[pl.BlockSpec((tq, hw), lambda h, g: (g, h))] + cast_out_specs,
        out_shape=[jax.ShapeDtypeStruct((seq, heads * HEAD_DIM), BF16)] + cast_shapes,
        scratch_shapes=[pltpu.VMEM((NA_HEADS_PER_STEP, 3, NA_QROWS * GRID_W, NA_KROWS * GRID_W), F32)],
        compiler_params=_cparams(("arbitrary", "arbitrary")),
    )(qkv, qkv, qkv, bias_pairs, mask_tabs, *[arr for arr, _ in casts])
    return out[0], list(out[1:])


def kernel(x, t5_bias, ab_w_in, ab_w_out, s5_lam_re, s5_lam_im, s5_log_step, s5_b_re, s5_b_im, s5_c_re, s5_c_im, s5_d, s5_w_glu, c_w_qkv, c_w_out, c_rpb, norm_mix, norm_mlp, mlp_w1, mlp_w2, norm_final):
    batch, seq, d_model = x.shape
    depth = norm_mix.shape[0]
    c_heads = c_rpb.shape[1]
    t5_buckets = _t5_bucket_tables()
    na_mask = jnp.asarray(_na_mask_tables())
    attn_scale = LOG2E / math.sqrt(HEAD_DIM)
    w_glu = s5_w_glu.astype(BF16)

    def mixer_weights(i):
        return ((ab_w_in, ab_w_out) if i % 2 == 0 else (c_w_qkv, c_w_out)), i // 2

    (mix_in0, mix_out0), _ = mixer_weights(0)
    mix_bf = {0: [mix_in0[:1].astype(BF16), mix_out0[:1].astype(BF16)]}
    outs = []
    for bi in range(batch):
        h = x[bi]
        for i in range(depth):
            j = i // 2
            w_mix_in, w_mix_out = mix_bf[i]
            if i % 2 == 0:
                proj = norm_matmul(h, norm_mix[i], w_mix_in, 0, F32)
                o_a, _ = dilated_attention(proj, t5_bias, t5_buckets, seq)
                a_re, a_im, b_mat, c_mat = _s5_discretise(
                    s5_lam_re[j], s5_lam_im[j], s5_log_step[j], s5_b_re[j], s5_b_im[j], s5_c_re[j], s5_c_im[j])
                yf, yb, (w1, w2) = s5_scan(proj, 3 * A_WIDTH, a_re, a_im, b_mat, c_mat, seq,
                                           casts=[(mlp_w1, i), (mlp_w2, i)])
                h, hn = s5_glu_out_projection(o_a, yf, yb, proj, 3 * A_WIDTH, s5_d[j], w_glu, j, w_mix_out, h,
                                              norm_mlp[i])
            else:
                qkv = norm_matmul(h, norm_mix[i], w_mix_in, 0, BF16, scaled_cols=c_heads * HEAD_DIM, scale=attn_scale)
                o, (w1, w2) = neighbourhood_attention(qkv, _na_bias_pairs(c_rpb[j]), na_mask, seq, c_heads,
                                                      casts=[(mlp_w1, i), (mlp_w2, i)])
                h, hn = matmul_residual(o, w_mix_out, 0, h, norm_mlp[i])
            casts = []
            if i + 1 < depth and i + 1 not in mix_bf:
                (nxt_in, nxt_out), jn = mixer_weights(i + 1)
                casts = [(nxt_in, jn), (nxt_out, jn)]
            h, cast_out = mlp(h, hn, w1, w2, 0, casts)
            if casts:
                mix_bf[i + 1] = cast_out
        outs.append(rmsnorm(h, norm_final))
    return jnp.stack(outs)
```

```python
import functools
import math

import numpy as np
import jax
import jax.numpy as jnp
from jax import lax
from jax.experimental import pallas as pl
from jax.experimental.pallas import tpu as pltpu

F32 = jnp.float32
BF16 = jnp.bfloat16

HEAD_DIM = 128
A_HEADS = 8
A_WIDTH = A_HEADS * HEAD_DIM
DILATED_BRANCHES = ((128, 1), (512, 4), (2048, 16))
A_QBLOCK = 128
A_HALF = 64
A_KBLOCK = A_QBLOCK + 2 * A_HALF
A_SUPER = 2048
B_GROUP = 16
B_STATE = 64
S5_SLAB = 256
S5_PAIRS = 8
S5_PITCH = 9
GRID_W = 64
NA_ROWS = 8
NA_COLS = 16
NA_QROWS = 4
NA_KROWS = 12
NA_EXT = NA_KROWS + 8
NA_HEADS_PER_STEP = 2
NA_GROUPS_PER_STEP = 4
T5_BUCKETS = 32
T5_MAX_DISTANCE = 1024
RMS_EPS = 1e-6
NEG_INF = -1e30
LOG2E = math.log2(math.e)
VMEM_LIMIT = 56 * 1024 * 1024


def _cparams(sem):
    return pltpu.CompilerParams(dimension_semantics=sem, vmem_limit_bytes=VMEM_LIMIT)


def _rms_rows(x, g):
    y = x * lax.rsqrt(jnp.mean(x * x, axis=-1, keepdims=True) + RMS_EPS)
    return y * g


def _with_casts(body, n_in, n_out, ncast):
    def wrapped(*refs):
        ins = refs[:n_in]
        cast_in = refs[n_in:n_in + ncast]
        outs = refs[n_in + ncast:n_in + ncast + n_out]
        cast_out = refs[n_in + ncast + n_out:n_in + 2 * ncast + n_out]
        for src, dst in zip(cast_in, cast_out):
            dst[...] = src[...].astype(BF16)
        body(*ins, *outs, *refs[n_in + 2 * ncast + n_out:])
    return wrapped


def _cast_plumbing(casts, nblocks, block_id):
    in_specs, out_specs, shapes = [], [], []
    for arr, layer in casts:
        _, rows, cols = arr.shape
        assert rows % (16 * nblocks) == 0
        blk = (None, rows // nblocks, cols)
        in_specs.append(pl.BlockSpec(blk, lambda *g, layer=layer: (layer, block_id(*g), 0)))
        out_specs.append(pl.BlockSpec(blk, lambda *g: (0, block_id(*g), 0)))
        shapes.append(jax.ShapeDtypeStruct((1, rows, cols), BF16))
    return in_specs, out_specs, shapes


def _serpentine(i, j, n):
    return jnp.where(i % 2 == 0, j, n - 1 - j)


def _norm_matmul_kernel(x_ref, g_ref, w_ref, o_ref, xn_ref, *, col_blocks, scaled_blocks, scale):
    j = pl.program_id(1)

    @pl.when(j == 0)
    def _():
        xn_ref[...] = _rms_rows(x_ref[...], g_ref[...]).astype(BF16)

    acc = jnp.dot(xn_ref[...], w_ref[...], preferred_element_type=F32)
    if scaled_blocks:
        acc = acc * jnp.where(_serpentine(pl.program_id(0), j, col_blocks) < scaled_blocks, scale, 1.0)
    o_ref[...] = acc.astype(o_ref.dtype)


def norm_matmul(x, g, w, layer, out_dtype, scaled_cols=0, scale=1.0, tm=1024, tn=1024):
    m, k = x.shape
    n = w.shape[2]
    assert scaled_cols % tn == 0
    nj = n // tn
    return pl.pallas_call(
        functools.partial(_norm_matmul_kernel, col_blocks=nj, scaled_blocks=scaled_cols // tn, scale=scale),
        grid=(m // tm, nj),
        in_specs=[
            pl.BlockSpec((tm, k), lambda i, j: (i, 0)),
            pl.BlockSpec((1, k), lambda i, j: (0, 0)),
            pl.BlockSpec((None, k, tn), lambda i, j: (layer, 0, _serpentine(i, j, nj))),
        ],
        out_specs=pl.BlockSpec((tm, tn), lambda i, j: (i, _serpentine(i, j, nj))),
        out_shape=jax.ShapeDtypeStruct((m, n), out_dtype),
        scratch_shapes=[pltpu.VMEM((tm, k), BF16)],
        compiler_params=_cparams(("arbitrary", "arbitrary")),
    )(x, g.reshape(1, k), w)


def _matmul_residual_kernel(a_ref, w_ref, r_ref, g_ref, o_ref, on_ref):
    acc = r_ref[...] + jnp.dot(a_ref[...], w_ref[...], preferred_element_type=F32)
    o_ref[...] = acc
    on_ref[...] = _rms_rows(acc, g_ref[...]).astype(BF16)


def matmul_residual(a, w, layer, res, g, tm=512):
    m = res.shape[0]
    _, k, n = w.shape
    row = lambda i: (i, 0)
    return pl.pallas_call(
        _matmul_residual_kernel,
        grid=(m // tm,),
        in_specs=[
            pl.BlockSpec((tm, k), row),
            pl.BlockSpec((None, k, n), lambda i: (layer, 0, 0)),
            pl.BlockSpec((tm, n), row),
            pl.BlockSpec((1, n), lambda i: (0, 0)),
        ],
        out_specs=[pl.BlockSpec((tm, n), row), pl.BlockSpec((tm, n), row)],
        out_shape=[jax.ShapeDtypeStruct((m, n), F32), jax.ShapeDtypeStruct((m, n), BF16)],
        compiler_params=_cparams(("arbitrary",)),
    )(a, w, res, g.reshape(1, n))


def _mlp_kernel(xn_ref, xres_ref, w1_ref, w2_ref, o_ref, h_ref, *, up_steps, tf):
    i = pl.program_id(0)
    s = pl.program_id(1)

    @pl.when(s < up_steps)
    def _():
        h = jnp.dot(xn_ref[...], w1_ref[...], preferred_element_type=F32)
        h_ref[_serpentine(i, s, up_steps)] = jnp.square(jnp.maximum(h, 0.0)).astype(BF16)

    @pl.when(s >= up_steps)
    def _():
        acc = xres_ref[...]
        for c in range(up_steps):
            acc = acc + jnp.dot(h_ref[c], w2_ref[pl.ds(c * tf, tf), :], preferred_element_type=F32)
        o_ref[...] = acc


def mlp(x, xn, w1, w2, layer, casts=(), tm=1024, tf=1024, tn=256):
    m, d = x.shape
    dff = w1.shape[2]
    up_steps = dff // tf
    down_steps = d // tn
    cast_in_specs, cast_out_specs, cast_shapes = _cast_plumbing(
        casts, (m // tm) * up_steps, lambda i, s: i * up_steps + jnp.minimum(s, up_steps - 1))

    def up_block(i, s):
        return _serpentine(i, jnp.minimum(s, up_steps - 1), up_steps)

    def down_block(i, s):
        return _serpentine(i, jnp.maximum(s - up_steps, 0), down_steps)

    out = pl.pallas_call(
        _with_casts(functools.partial(_mlp_kernel, up_steps=up_steps, tf=tf), 4, 1, len(casts)),
        grid=(m // tm, up_steps + down_steps),
        in_specs=[
            pl.BlockSpec((tm, d), lambda i, s: (i, 0)),
            pl.BlockSpec((tm, tn), lambda i, s: (i, down_block(i, s))),
            pl.BlockSpec((None, d, tf), lambda i, s: (layer, 0, up_block(i, s))),
            pl.BlockSpec((None, dff, tn), lambda i, s: (layer, 0, down_block(i, s))),
        ] + cast_in_specs,
        out_specs=[pl.BlockSpec((tm, tn), lambda i, s: (i, down_block(i, s)))] + cast_out_specs,
        out_shape=[jax.ShapeDtypeStruct((m, d), F32)] + cast_shapes,
        scratch_shapes=[pltpu.VMEM((up_steps, tm, tf), BF16)],
        compiler_params=_cparams(("arbitrary", "arbitrary")),
    )(xn, x, w1, w2, *[arr for arr, _ in casts])
    return out[0], list(out[1:])


def _rmsnorm_kernel(x_ref, g_ref, o_ref):
    o_ref[...] = _rms_rows(x_ref[...], g_ref[...])


def rmsnorm(x, g, tm=512):
    m, d = x.shape
    return pl.pallas_call(
        _rmsnorm_kernel,
        grid=(m // tm,),
        in_specs=[pl.BlockSpec((tm, d), lambda i: (i, 0)), pl.BlockSpec((1, d), lambda i: (0, 0))],
        out_specs=pl.BlockSpec((tm, d), lambda i: (i, 0)),
        out_shape=jax.ShapeDtypeStruct((m, d), F32),
        compiler_params=_cparams(("arbitrary",)),
    )(x, g.reshape(1, d))


def _t5_bucket(rel):
    half = T5_BUCKETS // 2
    max_exact = half // 2
    n = jnp.abs(rel)
    nf = jnp.maximum(n, 1).astype(F32)
    large = max_exact + (jnp.log(nf / max_exact) / math.log(T5_MAX_DISTANCE / max_exact)
                         * (half - max_exact)).astype(jnp.int32)
    large = jnp.minimum(large, half - 1)
    return jnp.where(rel > 0, half, 0) + jnp.where(n < max_exact, n, large)


def _t5_bucket_tables():
    tabs = []
    for _, dil in DILATED_BRANCHES:
        off = jnp.arange(A_KBLOCK)[None, :] - A_HALF - jnp.arange(A_QBLOCK)[:, None]
        tabs.append(jnp.where(jnp.abs(off) <= A_HALF, _t5_bucket(off * dil), T5_BUCKETS))
    return jnp.stack(tabs).astype(jnp.int32)


def _dilated_sections(seq):
    bases, sizes, row = [], [], 0
    for _, dil in DILATED_BRANCHES:
        sec = seq // dil + 2 * A_HALF
        bases.append(row)
        sizes.append(sec)
        row += dil * sec
    return bases, sizes, row


def _dilated_kernel(t5_ref, q_ref, k_ref, v_ref, bucket_ref, o_ref,
                    kd_ref, vd_ref, tmp_ref, bias_ref, m_ref, l_ref, n_ref, *, seq):
    h = pl.program_id(0)
    t = pl.program_id(1)
    scale = LOG2E / math.sqrt(HEAD_DIM)
    bases, sizes, _ = _dilated_sections(seq)

    @pl.when(t == 0)
    def _():
        for b in range(len(DILATED_BRANCHES)):
            bk = bucket_ref[b]
            acc = jnp.full((A_QBLOCK, A_KBLOCK), NEG_INF, F32)
            for kbkt in range(T5_BUCKETS):
                acc = jnp.where(bk == kbkt, t5_ref[kbkt * A_HEADS + h], acc)
            bias_ref[b] = acc * LOG2E
        zeros = jnp.zeros((A_HALF, HEAD_DIM), BF16)
        (_, d0), (_, d1), (_, d2) = DILATED_BRANCHES
        assert d0 == 1 and d2 % d1 == 0
        ratio = d2 // d1

        def put(dst, b, r, rows):
            o = bases[b] + r * sizes[b]
            dst[pl.ds(o, A_HALF), :] = zeros
            dst[pl.ds(o + A_HALF + rows.shape[0], A_HALF), :] = zeros
            dst[pl.ds(o + A_HALF, rows.shape[0]), :] = rows.astype(BF16)

        for src, dst in ((k_ref, kd_ref), (v_ref, vd_ref)):
            put(dst, 0, 0, src[...])
            for r1 in range(d1):
                rows1 = src[pl.ds(r1, seq // d1, stride=d1), :]
                put(dst, 1, r1, rows1)
                tmp_ref[...] = rows1
                for a in range(ratio):
                    put(dst, 2, a * d1 + r1, tmp_ref[pl.ds(a, seq // d2, stride=ratio), :])

    kk = lax.broadcasted_iota(jnp.int32, (1, A_KBLOCK), 1)
    for b, (_, dil) in enumerate(DILATED_BRANCHES):
        sub_len = seq // dil
        blocks_per_residue = A_SUPER // dil // A_QBLOCK

        def body(idx, carry, b=b, dil=dil, sub_len=sub_len, blocks_per_residue=blocks_per_residue):
            r = idx // blocks_per_residue
            n = idx % blocks_per_residue
            qs = r + n * (A_QBLOCK * dil)
            q_idx = pl.ds(qs, A_QBLOCK) if dil == 1 else pl.ds(qs, A_QBLOCK, stride=dil)
            blk = t * blocks_per_residue + n
            k_idx = pl.ds(pl.multiple_of(bases[b] + r * sizes[b] + blk * A_QBLOCK, A_QBLOCK), A_KBLOCK)
            qb = (q_ref[q_idx, :] * scale).astype(BF16)
            s = lax.dot_general(qb, kd_ref[k_idx, :], (((1,), (1,)), ((), ())), preferred_element_type=F32)
            key_l = blk * A_QBLOCK - A_HALF + kk
            edge = jnp.where((key_l >= 0) & (key_l < sub_len), 0.0, NEG_INF)
            s = s + bias_ref[b] + edge
            m = jnp.max(s, axis=1, keepdims=True)
            p = jnp.exp2(s - m)
            l = jnp.sum(p, axis=1, keepdims=True)
            num = jnp.dot(p.astype(BF16), vd_ref[k_idx, :], preferred_element_type=F32)
            m_ref[b, q_idx, :] = jnp.broadcast_to(m, (A_QBLOCK, HEAD_DIM))
            l_ref[b, q_idx, :] = jnp.broadcast_to(l, (A_QBLOCK, HEAD_DIM))
            n_ref[b, q_idx, :] = num
            return carry

        lax.fori_loop(0, A_SUPER // A_QBLOCK, body, 0, unroll=16)

    rows = 256
    for c in range(A_SUPER // rows):
        sl = pl.ds(c * rows, rows)
        m0, m1, m2 = m_ref[0, sl, :], m_ref[1, sl, :], m_ref[2, sl, :]
        mx = jnp.maximum(jnp.maximum(m0, m1), m2)
        w0, w1, w2 = jnp.exp2(m0 - mx), jnp.exp2(m1 - mx), jnp.exp2(m2 - mx)
        num = w0 * n_ref[0, sl, :] + w1 * n_ref[1, sl, :] + w2 * n_ref[2, sl, :]
        den = w0 * l_ref[0, sl, :] + w1 * l_ref[1, sl, :] + w2 * l_ref[2, sl, :]
        o_ref[sl, :] = (num / den).astype(o_ref.dtype)


def dilated_attention(proj, t5_table, bucket_tabs, seq, casts=()):
    assert seq % A_SUPER == 0
    nb = len(DILATED_BRANCHES)
    total_rows = _dilated_sections(seq)[2]
    tiles = seq // A_SUPER
    cast_in_specs, cast_out_specs, cast_shapes = _cast_plumbing(casts, A_HEADS * tiles, lambda h, t: h * tiles + t)
    out = pl.pallas_call(
        _with_casts(functools.partial(_dilated_kernel, seq=seq), 5, 1, len(casts)),
        grid=(A_HEADS, tiles),
        in_specs=[
            pl.BlockSpec(memory_space=pltpu.SMEM),
            pl.BlockSpec((A_SUPER, HEAD_DIM), lambda h, t: (t, h)),
            pl.BlockSpec((seq, HEAD_DIM), lambda h, t: (0, A_HEADS + h)),
            pl.BlockSpec((seq, HEAD_DIM), lambda h, t: (0, 2 * A_HEADS + h)),
            pl.BlockSpec((nb, A_QBLOCK, A_KBLOCK), lambda h, t: (0, 0, 0)),
        ] + cast_in_specs,
        out_specs=[pl.BlockSpec((A_SUPER, HEAD_DIM), lambda h, t: (t, h))] + cast_out_specs,
        out_shape=[jax.ShapeDtypeStruct((seq, A_WIDTH), BF16)] + cast_shapes,
        scratch_shapes=[
            pltpu.VMEM((total_rows, HEAD_DIM), BF16),
            pltpu.VMEM((total_rows, HEAD_DIM), BF16),
            pltpu.VMEM((seq // DILATED_BRANCHES[1][1], HEAD_DIM), F32),
            pltpu.VMEM((nb, A_QBLOCK, A_KBLOCK), F32),
            pltpu.VMEM((nb, A_SUPER, HEAD_DIM), F32),
            pltpu.VMEM((nb, A_SUPER, HEAD_DIM), F32),
            pltpu.VMEM((nb, A_SUPER, HEAD_DIM), F32),
        ],
        compiler_params=_cparams(("arbitrary", "arbitrary")),
    )(t5_table.astype(F32).reshape(-1), proj, proj, proj, bucket_tabs, *[arr for arr, _ in casts])
    return out[0], list(out[1:])


def _s5_discretise(lam_re, lam_im, log_step, b_re, b_im, c_re, c_im):
    g, p = lam_re.shape[1:]
    slabs = g // (2 * S5_PAIRS)
    step = jnp.exp(log_step.astype(F32))[..., None]
    lr = jnp.minimum(lam_re.astype(F32), -1e-4)
    li = lam_im.astype(F32)
    mag = jnp.exp(lr * step)
    ab_re = mag * jnp.cos(li * step)
    ab_im = mag * jnp.sin(li * step)
    den = lr * lr + li * li
    zr = ((ab_re - 1.0) * lr + ab_im * li) / den
    zi = (ab_im * lr - (ab_re - 1.0) * li) / den
    br = b_re.astype(F32)[None]
    bi = b_im.astype(F32)[None]
    bb_re = zr[..., None] * br - zi[..., None] * bi
    bb_im = zr[..., None] * bi + zi[..., None] * br

    a_re = ab_re.reshape(2, slabs, S5_PAIRS, 2 * p)
    a_im = ab_im.reshape(2, slabs, S5_PAIRS, 2 * p)

    eye2 = jnp.eye(2, dtype=F32)

    def pack_b(bb):
        bb = bb.reshape(2, slabs, S5_PAIRS, 2, p, B_GROUP)
        return jnp.einsum('dsjepc,ef->dsjecfp', bb, eye2).reshape(2, slabs, S5_SLAB, 2 * p)

    def pack_c(cc):
        cc = cc.astype(F32).reshape(2, slabs, S5_PAIRS, 2, B_GROUP, p)
        return jnp.einsum('dsjecp,ef->dsfpjec', cc, eye2).reshape(2, slabs, 2 * p, S5_SLAB)

    b_mat = jnp.concatenate([pack_b(bb_re), pack_b(bb_im)], axis=-1)
    c_mat = jnp.concatenate([pack_c(c_re), -pack_c(c_im)], axis=-2)
    return a_re, a_im, b_mat.astype(BF16), c_mat.astype(BF16)


def _s5_kernel(uf_ref, ub_ref, b_ref, c_ref, are_ref, aim_ref, yf_ref, yb_ref,
               xr_ref, xi_ref, bur_ref, bui_ref, *, tm, slabs):
    c = pl.program_id(0)
    nstate = 2 * B_STATE
    u_refs = (uf_ref, ub_ref)
    y_refs = (yf_ref, yb_ref)

    @pl.when(c == 0)
    def _():
        xr_ref[...] = jnp.zeros_like(xr_ref)
        xi_ref[...] = jnp.zeros_like(xi_ref)

    lane_pair = lax.broadcasted_iota(jnp.int32, (1, S5_SLAB), 1) // (2 * B_GROUP)

    def pair_rows(j):
        return pl.ds(j, tm, stride=S5_PITCH)

    for d in range(2):
        for s in range(slabs):
            u16 = u_refs[d][:, s * S5_SLAB:(s + 1) * S5_SLAB].astype(BF16)
            lhs = jnp.concatenate([jnp.where(lane_pair == j, u16, jnp.zeros_like(u16))
                                   for j in range(S5_PAIRS)], axis=0)
            bu = jnp.dot(lhs, b_ref[d, s], preferred_element_type=F32)
            for j in range(S5_PAIRS):
                bur_ref[d, s, pair_rows(j), :] = bu[j * tm:(j + 1) * tm, :nstate]
                bui_ref[d, s, pair_rows(j), :] = bu[j * tm:(j + 1) * tm, nstate:]

    ar = [[are_ref[d, s] for s in range(slabs)] for d in range(2)]
    ai = [[aim_ref[d, s] for s in range(slabs)] for d in range(2)]

    def step(i, carry):
        new = []
        for d in range(2):
            t = i if d == 0 else tm - 1 - i
            start = t * S5_PITCH
            rows = pl.ds(pl.multiple_of(start, 8) if S5_PITCH % 8 == 0 else start, S5_PAIRS)
            for s in range(slabs):
                k = 2 * (d * slabs + s)
                xr, xi = carry[k], carry[k + 1]
                nxr = ar[d][s] * xr - ai[d][s] * xi + bur_ref[d, s, rows, :]
                nxi = ar[d][s] * xi + ai[d][s] * xr + bui_ref[d, s, rows, :]
                bur_ref[d, s, rows, :] = nxr
                bui_ref[d, s, rows, :] = nxi
                new += [nxr, nxi]
        return tuple(new)

    init = []
    for d in range(2):
        for s in range(slabs):
            init += [xr_ref[d, s], xi_ref[d, s]]
    final = tuple(init)
    for i in range(tm):
        final = step(i, final)
    for d in range(2):
        for s in range(slabs):
            k = 2 * (d * slabs + s)
            xr_ref[d, s] = final[k]
            xi_ref[d, s] = final[k + 1]

    for d in range(2):
        for s in range(slabs):
            xcat = jnp.concatenate(
                [jnp.concatenate([bur_ref[d, s, pair_rows(j), :], bui_ref[d, s, pair_rows(j), :]],
                                 axis=1).astype(BF16) for j in range(S5_PAIRS)], axis=0)
            r = jnp.dot(xcat, c_ref[d, s], preferred_element_type=F32)
            y = r[:tm]
            for j in range(1, S5_PAIRS):
                y = jnp.where(lane_pair == j, r[j * tm:(j + 1) * tm], y)
            y_refs[d][:, s * S5_SLAB:(s + 1) * S5_SLAB] = y


def s5_scan(proj, u_col0, a_re, a_im, b_mat, c_mat, seq, casts=(), tm=256):
    slabs = a_re.shape[1]
    width = slabs * S5_SLAB
    nchunks = seq // tm
    assert u_col0 % width == 0
    ucol = u_col0 // width
    wspec = pl.BlockSpec((2, slabs, S5_SLAB, S5_SLAB), lambda c: (0, 0, 0, 0))
    aspec = pl.BlockSpec((2, slabs, S5_PAIRS, 2 * B_STATE), lambda c: (0, 0, 0, 0))
    state = pltpu.VMEM((2, slabs, S5_PAIRS, 2 * B_STATE), F32)
    drive = pltpu.VMEM((2, slabs, tm * S5_PITCH, 2 * B_STATE), F32)
    cast_in_specs, cast_out_specs, cast_shapes = _cast_plumbing(casts, nchunks, lambda c: c)
    out = pl.pallas_call(
        _with_casts(functools.partial(_s5_kernel, tm=tm, slabs=slabs), 6, 2, len(casts)),
        grid=(nchunks,),
        in_specs=[pl.BlockSpec((tm, width), lambda c: (c, ucol)),
                  pl.BlockSpec((tm, width), lambda c: (nchunks - 1 - c, ucol)),
                  wspec, wspec, aspec, aspec] + cast_in_specs,
        out_specs=[pl.BlockSpec((tm, width), lambda c: (c, 0)),
                   pl.BlockSpec((tm, width), lambda c: (nchunks - 1 - c, 0))] + cast_out_specs,
        out_shape=[jax.ShapeDtypeStruct((seq, width), F32), jax.ShapeDtypeStruct((seq, width), F32)] + cast_shapes,
        scratch_shapes=[state, state, drive, drive],
        compiler_params=_cparams(("arbitrary",)),
    )(proj, proj, b_mat, c_mat, a_re, a_im, *[arr for arr, _ in casts])
    return out[0], out[1], list(out[2:])


def _s5_out_kernel(oa_ref, yf_ref, yb_ref, u_ref, dsk_ref, wg_ref, w_ref, r_ref, g_ref, o_ref, on_ref):
    y = jax.nn.gelu(yf_ref[...] + yb_ref[...] + dsk_ref[...] * u_ref[...])
    z = jnp.dot(y.astype(BF16), wg_ref[...], preferred_element_type=F32)
    ob = (y * jax.nn.sigmoid(z)).astype(BF16)
    ka = oa_ref.shape[1]
    acc = r_ref[...] + jnp.dot(oa_ref[...], w_ref[pl.ds(0, ka), :], preferred_element_type=F32)
    acc = acc + jnp.dot(ob, w_ref[pl.ds(ka, ob.shape[1]), :], preferred_element_type=F32)
    o_ref[...] = acc
    on_ref[...] = _rms_rows(acc, g_ref[...]).astype(BF16)


def s5_glu_out_projection(o_a, yf, yb, proj, u_col0, d_skip, w_glu, glu_layer, w_out, res, g, tm=512):
    m, k = yf.shape
    ka = o_a.shape[1]
    n = w_out.shape[2]
    row = lambda i: (i, 0)
    return pl.pallas_call(
        _s5_out_kernel,
        grid=(m // tm,),
        in_specs=[
            pl.BlockSpec((tm, ka), row),
            pl.BlockSpec((tm, k), row),
            pl.BlockSpec((tm, k), row),
            pl.BlockSpec((tm, k), lambda i: (i, u_col0 // k)),
            pl.BlockSpec((1, k), lambda i: (0, 0)),
            pl.BlockSpec((None, k, k), lambda i: (glu_layer, 0, 0)),
            pl.BlockSpec((None, ka + k, n), lambda i: (0, 0, 0)),
            pl.BlockSpec((tm, n), row),
            pl.BlockSpec((1, n), lambda i: (0, 0)),
        ],
        out_specs=[pl.BlockSpec((tm, n), row), pl.BlockSpec((tm, n), row)],
        out_shape=[jax.ShapeDtypeStruct((m, n), F32), jax.ShapeDtypeStruct((m, n), BF16)],
        compiler_params=_cparams(("arbitrary",)),
    )(o_a, yf, yb, proj, d_skip.reshape(1, k), w_glu, w_out, res, g.reshape(1, n))


def _na_bias_pairs(rpb):
    c = np.arange(GRID_W)
    col_idx = np.clip(c[None, :] - c[:, None] + NA_COLS - 1, 0, 2 * NA_COLS - 2)
    onehot = (col_idx[None] == np.arange(2 * NA_COLS - 1)[:, None, None]).astype(np.float32)
    r2 = jnp.einsum('hdk,kqc->hdqc', rpb.astype(F32), jnp.asarray(onehot), precision=lax.Precision.HIGHEST)
    r2 = jnp.pad(r2, ((0, 0), (4, 4), (0, 0), (0, 0)))
    return jnp.concatenate([r2[:, :-1], r2[:, 1:]], axis=-1)


def _na_mask_tables():
    qr = np.arange(NA_QROWS)[:, None, None, None]
    cq = np.arange(GRID_W)[None, :, None, None]
    kr = np.arange(NA_KROWS)[None, None, :, None]
    ck = np.arange(GRID_W)[None, None, None, :]
    cs = np.clip(cq - NA_COLS // 2, 0, GRID_W - NA_COLS)
    col_ok = (ck >= cs) & (ck < cs + NA_COLS)
    row_ok = [(kr < NA_ROWS) & (qr >= 0),
              (kr >= qr) & (kr < qr + NA_ROWS),
              (kr >= NA_KROWS - NA_ROWS) & (qr >= 0)]
    out = [np.where(r & col_ok, 0.0, NEG_INF).reshape(NA_QROWS * GRID_W, NA_KROWS * GRID_W) for r in row_ok]
    return np.stack(out).astype(np.float32)


def _natten_kernel(q_ref, k_ref, v_ref, pair_ref, mask_ref, o_ref, bias_ref, s_ref, p_ref, l_ref, *, groups, rows):
    gs = pl.program_id(1)
    nk = NA_KROWS * GRID_W
    tq = NA_QROWS * GRID_W

    @pl.when(gs == 0)
    def _():
        for hh in range(NA_HEADS_PER_STEP):
            for v in range(3):
                for qr in range(NA_QROWS):
                    q_rows = pl.ds(qr * GRID_W, GRID_W)
                    for kr2 in range(NA_KROWS // 2):
                        k_cols = pl.ds(kr2 * 128, 128)
                        bias_ref[hh, v, q_rows, k_cols] = (
                            pair_ref[hh, 2 * (kr2 + 2 * (2 - v)) - qr + 3] + mask_ref[v, q_rows, k_cols]) * LOG2E

    blocks = [(gi, hh) for gi in range(NA_GROUPS_PER_STEP) for hh in range(NA_HEADS_PER_STEP)]

    def coords(gi):
        g = gs * NA_GROUPS_PER_STEP + gi
        key_row0 = jnp.clip(NA_QROWS * g - NA_ROWS // 2, 0, rows - NA_KROWS)
        tok0 = pl.multiple_of(key_row0 * GRID_W, GRID_W)
        variant = jnp.where(g == 0, 0, jnp.where(g == groups - 1, 2, 1))
        return tok0, variant

    def stage_scores(b):
        gi, hh = blocks[b]
        tok0, variant = coords(gi)
        cols = pl.ds(hh * HEAD_DIM, HEAD_DIM)
        s = lax.dot_general(q_ref[pl.ds(gi * tq, tq), cols], k_ref[pl.ds(tok0, nk), cols], (((1,), (1,)), ((), ())),
                            preferred_element_type=F32)
        s_ref[b] = s + bias_ref[hh, variant]

    def stage_softmax(b):
        s = s_ref[b]
        m = jnp.max(s, axis=1, keepdims=True)
        p = jnp.exp2(s - m)
        l_ref[b] = jnp.broadcast_to(jnp.sum(p, axis=1, keepdims=True), (tq, HEAD_DIM))
        p_ref[b] = p.astype(BF16)

    def stage_out(b):
        gi, hh = blocks[b]
        tok0, _ = coords(gi)
        cols = pl.ds(hh * HEAD_DIM, HEAD_DIM)
        o = jnp.dot(p_ref[b], v_ref[pl.ds(tok0, nk), cols], preferred_element_type=F32)
        o_ref[pl.ds(gi * tq, tq), cols] = (o / l_ref[b]).astype(o_ref.dtype)

    for step in range(len(blocks) + 2):
        if step < len(blocks):
            stage_scores(step)
        if 0 <= step - 1 < len(blocks):
            stage_softmax(step - 1)
        if 0 <= step - 2 < len(blocks):
            stage_out(step - 2)


def neighbourhood_attention(qkv, bias_pairs, mask_tabs, seq, heads, casts=()):
    rows = seq // GRID_W
    assert rows % (NA_QROWS * NA_GROUPS_PER_STEP) == 0 and rows >= NA_KROWS + NA_QROWS
    assert heads % NA_HEADS_PER_STEP == 0
    groups = rows // NA_QROWS
    tq = NA_QROWS * GRID_W * NA_GROUPS_PER_STEP
    hw = NA_HEADS_PER_STEP * HEAD_DIM
    hsteps = heads // NA_HEADS_PER_STEP
    gsteps = groups // NA_GROUPS_PER_STEP
    cast_in_specs, cast_out_specs, cast_shapes = _cast_plumbing(casts, hsteps * gsteps, lambda h, g: h * gsteps + g)
    out = pl.pallas_call(
        _with_casts(functools.partial(_natten_kernel, groups=groups, rows=rows), 5, 1, len(casts)),
        grid=(hsteps, gsteps),
        in_specs=[
            pl.BlockSpec((tq, hw), lambda h, g: (g, h)),
            pl.BlockSpec((seq, hw), lambda h, g: (0, hsteps + h)),
            pl.BlockSpec((seq, hw), lambda h, g: (0, 2 * hsteps + h)),
            pl.BlockSpec((NA_HEADS_PER_STEP,) + bias_pairs.shape[1:], lambda h, g: (h, 0, 0, 0)),
            pl.BlockSpec(mask_tabs.shape, lambda h, g: (0, 0, 0)),
        ] + cast_in_specs,
        out_specs=[pl.BlockSpec((tq, hw), lambda h, g: (g, h))] + cast_out_specs,
        out_shape=[jax.ShapeDtypeStruct((seq, heads * HEAD_DIM), BF16)] + cast_shapes,
        scratch_shapes=[pltpu.VMEM((NA_HEADS_PER_STEP, 3, NA_QROWS * GRID_W, NA_KROWS * GRID_W), F32),
                        pltpu.VMEM((NA_GROUPS_PER_STEP * NA_HEADS_PER_STEP, NA_QROWS * GRID_W, NA_KROWS * GRID_W), F32),
                        pltpu.VMEM((NA_GROUPS_PER_STEP * NA_HEADS_PER_STEP, NA_QROWS * GRID_W, NA_KROWS * GRID_W), BF16),
                        pltpu.VMEM((NA_GROUPS_PER_STEP * NA_HEADS_PER_STEP, NA_QROWS * GRID_W, HEAD_DIM), F32)],
        compiler_params=_cparams(("arbitrary", "arbitrary")),
    )(qkv, qkv, qkv, bias_pairs, mask_tabs, *[arr for arr, _ in casts])
    return out[0], list(out[1:])


def kernel(x, t5_bias, ab_w_in, ab_w_out, s5_lam_re, s5_lam_im, s5_log_step, s5_b_re, s5_b_im, s5_c_re, s5_c_im, s5_d, s5_w_glu, c_w_qkv, c_w_out, c_rpb, norm_mix, norm_mlp, mlp_w1, mlp_w2, norm_final):
    batch, seq, d_model = x.shape
    depth = norm_mix.shape[0]
    c_heads = c_rpb.shape[1]
    t5_buckets = _t5_bucket_tables()
    na_mask = jnp.asarray(_na_mask_tables())
    attn_scale = LOG2E / math.sqrt(HEAD_DIM)
    w_glu = s5_w_glu.astype(BF16)

    def mixer_weights(i):
        return ((ab_w_in, ab_w_out) if i % 2 == 0 else (c_w_qkv, c_w_out)), i // 2

    (mix_in0, mix_out0), _ = mixer_weights(0)
    mix_bf = {0: [mix_in0[:1].astype(BF16), mix_out0[:1].astype(BF16)]}
    outs = []
    for bi in range(batch):
        h = x[bi]
        for i in range(depth):
            j = i // 2
            w_mix_in, w_mix_out = mix_bf[i]
            if i % 2 == 0:
                proj = norm_matmul(h, norm_mix[i], w_mix_in, 0, F32)
                o_a, _ = dilated_attention(proj, t5_bias, t5_buckets, seq)
                a_re, a_im, b_mat, c_mat = _s5_discretise(
                    s5_lam_re[j], s5_lam_im[j], s5_log_step[j], s5_b_re[j], s5_b_im[j], s5_c_re[j], s5_c_im[j])
                yf, yb, (w1, w2) = s5_scan(proj, 3 * A_WIDTH, a_re, a_im, b_mat, c_mat, seq,
                                           casts=[(mlp_w1, i), (mlp_w2, i)])
                h, hn = s5_glu_out_projection(o_a, yf, yb, proj, 3 * A_WIDTH, s5_d[j], w_glu, j, w_mix_out, h,
                                              norm_mlp[i])
            else:
                qkv = norm_matmul(h, norm_mix[i], w_mix_in, 0, BF16, scaled_cols=c_heads * HEAD_DIM, scale=attn_scale)
                o, (w1, w2) = neighbourhood_attention(qkv, _na_bias_pairs(c_rpb[j]), na_mask, seq, c_heads,
                                                      casts=[(mlp_w1, i), (mlp_w2, i)])
                h, hn = matmul_residual(o, w_mix_out, 0, h, norm_mlp[i])
            casts = []
            if i + 1 < depth and i + 1 not in mix_bf:
                (nxt_in, nxt_out), jn = mixer_weights(i + 1)
                casts = [(nxt_in, jn), (nxt_out, jn)]
            h, cast_out = mlp(h, hn, w1, w2, 0, casts)
            if casts:
                mix_bf[i + 1] = cast_out
        outs.append(rmsnorm(h, norm_final))
    return jnp.stack(outs)
```

```python
import functools
import math

import numpy as np
import jax
import jax.numpy as jnp
from jax import lax
from jax.experimental import pallas as pl
from jax.experimental.pallas import tpu as pltpu

F32 = jnp.float32
BF16 = jnp.bfloat16

HEAD_DIM = 128
A_HEADS = 8
A_WIDTH = A_HEADS * HEAD_DIM
DILATED_BRANCHES = ((128, 1), (512, 4), (2048, 16))
A_QBLOCK = 128
A_HALF = 64
A_KBLOCK = A_QBLOCK + 2 * A_HALF
A_SUPER = 2048
B_GROUP = 16
B_STATE = 64
S5_SLAB = 256
S5_PAIRS = 8
S5_PITCH = 9
S5_PARTS = 2
GRID_W = 64
NA_ROWS = 8
NA_COLS = 16
NA_QROWS = 4
NA_KROWS = 12
NA_EXT = NA_KROWS + 8
NA_HEADS_PER_STEP = 2
NA_GROUPS_PER_STEP = 8
NA_STAGE_SLOTS = 4
T5_BUCKETS = 32
T5_MAX_DISTANCE = 1024
RMS_EPS = 1e-6
NEG_INF = -1e30
LOG2E = math.log2(math.e)
VMEM_LIMIT = 56 * 1024 * 1024


def _cparams(sem):
    return pltpu.CompilerParams(dimension_semantics=sem, vmem_limit_bytes=VMEM_LIMIT)


def _rms_rows(x, g):
    y = x * lax.rsqrt(jnp.mean(x * x, axis=-1, keepdims=True) + RMS_EPS)
    return y * g


def _with_casts(body, n_in, n_out, ncast):
    def wrapped(*refs):
        ins = refs[:n_in]
        cast_in = refs[n_in:n_in + ncast]
        outs = refs[n_in + ncast:n_in + ncast + n_out]
        cast_out = refs[n_in + ncast + n_out:n_in + 2 * ncast + n_out]
        for src, dst in zip(cast_in, cast_out):
            dst[...] = src[...].astype(BF16)
        body(*ins, *outs, *refs[n_in + 2 * ncast + n_out:])
    return wrapped


def _cast_plumbing(casts, nblocks, block_id):
    in_specs, out_specs, shapes = [], [], []
    for arr, layer in casts:
        _, rows, cols = arr.shape
        assert rows % (16 * nblocks) == 0
        blk = (None, rows // nblocks, cols)
        in_specs.append(pl.BlockSpec(blk, lambda *g, layer=layer: (layer, block_id(*g), 0)))
        out_specs.append(pl.BlockSpec(blk, lambda *g: (0, block_id(*g), 0)))
        shapes.append(jax.ShapeDtypeStruct((1, rows, cols), BF16))
    return in_specs, out_specs, shapes


def _serpentine(i, j, n):
    return jnp.where(i % 2 == 0, j, n - 1 - j)


def _norm_matmul_kernel(x_ref, g_ref, w_ref, o_ref, xn_ref, *, col_blocks, scaled_blocks, scale):
    j = pl.program_id(1)

    @pl.when(j == 0)
    def _():
        xn_ref[...] = _rms_rows(x_ref[...], g_ref[...]).astype(BF16)

    acc = jnp.dot(xn_ref[...], w_ref[...], preferred_element_type=F32)
    if scaled_blocks:
        acc = acc * jnp.where(_serpentine(pl.program_id(0), j, col_blocks) < scaled_blocks, scale, 1.0)
    o_ref[...] = acc.astype(o_ref.dtype)


def norm_matmul(x, g, w, layer, out_dtype, scaled_cols=0, scale=1.0, casts=(), tm=1024, tn=1024):
    m, k = x.shape
    n = w.shape[2]
    assert scaled_cols % tn == 0
    nj = n // tn
    cast_in_specs, cast_out_specs, cast_shapes = _cast_plumbing(casts, (m // tm) * nj, lambda i, j: i * nj + j)
    out = pl.pallas_call(
        _with_casts(functools.partial(_norm_matmul_kernel, col_blocks=nj, scaled_blocks=scaled_cols // tn,
                                      scale=scale), 3, 1, len(casts)),
        grid=(m // tm, nj),
        in_specs=[
            pl.BlockSpec((tm, k), lambda i, j: (i, 0)),
            pl.BlockSpec((1, k), lambda i, j: (0, 0)),
            pl.BlockSpec((None, k, tn), lambda i, j: (layer, 0, _serpentine(i, j, nj))),
        ] + cast_in_specs,
        out_specs=[pl.BlockSpec((tm, tn), lambda i, j: (i, _serpentine(i, j, nj)))] + cast_out_specs,
        out_shape=[jax.ShapeDtypeStruct((m, n), out_dtype)] + cast_shapes,
        scratch_shapes=[pltpu.VMEM((tm, k), BF16)],
        compiler_params=_cparams(("arbitrary", "arbitrary")),
    )(x, g.reshape(1, k), w, *[arr for arr, _ in casts])
    return out[0], list(out[1:])


def _matmul_residual_kernel(a_ref, w_ref, r_ref, g_ref, o_ref, on_ref):
    acc = r_ref[...] + jnp.dot(a_ref[...], w_ref[...], preferred_element_type=F32)
    o_ref[...] = acc
    on_ref[...] = _rms_rows(acc, g_ref[...]).astype(BF16)


def matmul_residual(a, w, layer, res, g, tm=512):
    m = res.shape[0]
    _, k, n = w.shape
    row = lambda i: (i, 0)
    return pl.pallas_call(
        _matmul_residual_kernel,
        grid=(m // tm,),
        in_specs=[
            pl.BlockSpec((tm, k), row),
            pl.BlockSpec((None, k, n), lambda i: (layer, 0, 0)),
            pl.BlockSpec((tm, n), row),
            pl.BlockSpec((1, n), lambda i: (0, 0)),
        ],
        out_specs=[pl.BlockSpec((tm, n), row), pl.BlockSpec((tm, n), row)],
        out_shape=[jax.ShapeDtypeStruct((m, n), F32), jax.ShapeDtypeStruct((m, n), BF16)],
        compiler_params=_cparams(("arbitrary",)),
    )(a, w, res, g.reshape(1, n))


def _mlp_kernel(xn_ref, xres_ref, w1_ref, w2_ref, o_ref, h_ref, *, up_steps, tf):
    i = pl.program_id(0)
    s = pl.program_id(1)

    @pl.when(s < up_steps)
    def _():
        h = jnp.dot(xn_ref[...], w1_ref[...], preferred_element_type=F32)
        h_ref[_serpentine(i, s, up_steps)] = jnp.square(jnp.maximum(h, 0.0)).astype(BF16)

    @pl.when(s >= up_steps)
    def _():
        acc = xres_ref[...]
        for c in range(up_steps):
            acc = acc + jnp.dot(h_ref[c], w2_ref[pl.ds(c * tf, tf), :], preferred_element_type=F32)
        o_ref[...] = acc


def mlp(x, xn, w1, w2, layer, casts=(), tm=1024, tf=1024, tn=256):
    m, d = x.shape
    dff = w1.shape[2]
    up_steps = dff // tf
    down_steps = d // tn
    cast_in_specs, cast_out_specs, cast_shapes = _cast_plumbing(
        casts, (m // tm) * up_steps, lambda i, s: i * up_steps + jnp.minimum(s, up_steps - 1))

    def up_block(i, s):
        return _serpentine(i, jnp.minimum(s, up_steps - 1), up_steps)

    def down_block(i, s):
        return _serpentine(i, jnp.maximum(s - up_steps, 0), down_steps)

    out = pl.pallas_call(
        _with_casts(functools.partial(_mlp_kernel, up_steps=up_steps, tf=tf), 4, 1, len(casts)),
        grid=(m // tm, up_steps + down_steps),
        in_specs=[
            pl.BlockSpec((tm, d), lambda i, s: (i, 0)),
            pl.BlockSpec((tm, tn), lambda i, s: (i, down_block(i, s))),
            pl.BlockSpec((None, d, tf), lambda i, s: (layer, 0, up_block(i, s))),
            pl.BlockSpec((None, dff, tn), lambda i, s: (layer, 0, down_block(i, s))),
        ] + cast_in_specs,
        out_specs=[pl.BlockSpec((tm, tn), lambda i, s: (i, down_block(i, s)))] + cast_out_specs,
        out_shape=[jax.ShapeDtypeStruct((m, d), F32)] + cast_shapes,
        scratch_shapes=[pltpu.VMEM((up_steps, tm, tf), BF16)],
        compiler_params=_cparams(("arbitrary", "arbitrary")),
    )(xn, x, w1, w2, *[arr for arr, _ in casts])
    return out[0], list(out[1:])


def _rmsnorm_kernel(x_ref, g_ref, o_ref):
    o_ref[...] = _rms_rows(x_ref[...], g_ref[...])


def rmsnorm(x, g, tm=512):
    m, d = x.shape
    return pl.pallas_call(
        _rmsnorm_kernel,
        grid=(m // tm,),
        in_specs=[pl.BlockSpec((tm, d), lambda i: (i, 0)), pl.BlockSpec((1, d), lambda i: (0, 0))],
        out_specs=pl.BlockSpec((tm, d), lambda i: (i, 0)),
        out_shape=jax.ShapeDtypeStruct((m, d), F32),
        compiler_params=_cparams(("arbitrary",)),
    )(x, g.reshape(1, d))


def _t5_bucket(rel):
    half = T5_BUCKETS // 2
    max_exact = half // 2
    n = jnp.abs(rel)
    nf = jnp.maximum(n, 1).astype(F32)
    large = max_exact + (jnp.log(nf / max_exact) / math.log(T5_MAX_DISTANCE / max_exact)
                         * (half - max_exact)).astype(jnp.int32)
    large = jnp.minimum(large, half - 1)
    return jnp.where(rel > 0, half, 0) + jnp.where(n < max_exact, n, large)


def _t5_bucket_tables():
    tabs = []
    for _, dil in DILATED_BRANCHES:
        off = jnp.arange(A_KBLOCK)[None, :] - A_HALF - jnp.arange(A_QBLOCK)[:, None]
        tabs.append(jnp.where(jnp.abs(off) <= A_HALF, _t5_bucket(off * dil), T5_BUCKETS))
    return jnp.stack(tabs).astype(jnp.int32)


def _dilated_sections(seq):
    bases, sizes, row = [], [], 0
    for _, dil in DILATED_BRANCHES:
        sec = seq // dil + 2 * A_HALF
        bases.append(row)
        sizes.append(sec)
        row += dil * sec
    return bases, sizes, row


def _dilated_kernel(t5_ref, q_ref, k_ref, v_ref, bucket_ref, o_ref,
                    kd_ref, vd_ref, tmp_ref, bias_ref, m_ref, l_ref, n_ref, *, seq):
    h = pl.program_id(0)
    t = pl.program_id(1)
    scale = LOG2E / math.sqrt(HEAD_DIM)
    bases, sizes, _ = _dilated_sections(seq)

    @pl.when(t == 0)
    def _():
        for b in range(len(DILATED_BRANCHES)):
            bk = bucket_ref[b]
            acc = jnp.full((A_QBLOCK, A_KBLOCK), NEG_INF, F32)
            for kbkt in range(T5_BUCKETS):
                acc = jnp.where(bk == kbkt, t5_ref[kbkt * A_HEADS + h], acc)
            bias_ref[b] = acc * LOG2E
        zeros = jnp.zeros((A_HALF, HEAD_DIM), BF16)
        (_, d0), (_, d1), (_, d2) = DILATED_BRANCHES
        assert d0 == 1 and d2 % d1 == 0
        ratio = d2 // d1

        def put(dst, b, r, rows):
            o = bases[b] + r * sizes[b]
            dst[pl.ds(o, A_HALF), :] = zeros
            dst[pl.ds(o + A_HALF + rows.shape[0], A_HALF), :] = zeros
            dst[pl.ds(o + A_HALF, rows.shape[0]), :] = rows.astype(BF16)

        for src, dst in ((k_ref, kd_ref), (v_ref, vd_ref)):
            put(dst, 0, 0, src[...])
            for r1 in range(d1):
                rows1 = src[pl.ds(r1, seq // d1, stride=d1), :]
                put(dst, 1, r1, rows1)
                tmp_ref[...] = rows1
                for a in range(ratio):
                    put(dst, 2, a * d1 + r1, tmp_ref[pl.ds(a, seq // d2, stride=ratio), :])

    kk = lax.broadcasted_iota(jnp.int32, (1, A_KBLOCK), 1)
    for b, (_, dil) in enumerate(DILATED_BRANCHES):
        sub_len = seq // dil
        blocks_per_residue = A_SUPER // dil // A_QBLOCK

        def body(idx, carry, b=b, dil=dil, sub_len=sub_len, blocks_per_residue=blocks_per_residue):
            r = idx // blocks_per_residue
            n = idx % blocks_per_residue
            qs = r + n * (A_QBLOCK * dil)
            q_idx = pl.ds(qs, A_QBLOCK) if dil == 1 else pl.ds(qs, A_QBLOCK, stride=dil)
            blk = t * blocks_per_residue + n
            k_idx = pl.ds(pl.multiple_of(bases[b] + r * sizes[b] + blk * A_QBLOCK, A_QBLOCK), A_KBLOCK)
            qb = (q_ref[q_idx, :] * scale).astype(BF16)
            s = lax.dot_general(qb, kd_ref[k_idx, :], (((1,), (1,)), ((), ())), preferred_element_type=F32)
            key_l = blk * A_QBLOCK - A_HALF + kk
            edge = jnp.where((key_l >= 0) & (key_l < sub_len), 0.0, NEG_INF)
            s = s + bias_ref[b] + edge
            m = jnp.max(s, axis=1, keepdims=True)
            p = jnp.exp2(s - m)
            l = jnp.sum(p, axis=1, keepdims=True)
            num = jnp.dot(p.astype(BF16), vd_ref[k_idx, :], preferred_element_type=F32)
            m_ref[b, q_idx, :] = jnp.broadcast_to(m, (A_QBLOCK, HEAD_DIM))
            l_ref[b, q_idx, :] = jnp.broadcast_to(l, (A_QBLOCK, HEAD_DIM))
            n_ref[b, q_idx, :] = num
            return carry

        lax.fori_loop(0, A_SUPER // A_QBLOCK, body, 0, unroll=16)

    rows = 256
    for c in range(A_SUPER // rows):
        sl = pl.ds(c * rows, rows)
        m0, m1, m2 = m_ref[0, sl, :], m_ref[1, sl, :], m_ref[2, sl, :]
        mx = jnp.maximum(jnp.maximum(m0, m1), m2)
        w0, w1, w2 = jnp.exp2(m0 - mx), jnp.exp2(m1 - mx), jnp.exp2(m2 - mx)
        num = w0 * n_ref[0, sl, :] + w1 * n_ref[1, sl, :] + w2 * n_ref[2, sl, :]
        den = w0 * l_ref[0, sl, :] + w1 * l_ref[1, sl, :] + w2 * l_ref[2, sl, :]
        o_ref[sl, :] = (num / den).astype(o_ref.dtype)


def dilated_attention(proj, t5_table, bucket_tabs, seq, casts=()):
    assert seq % A_SUPER == 0
    nb = len(DILATED_BRANCHES)
    total_rows = _dilated_sections(seq)[2]
    tiles = seq // A_SUPER
    cast_in_specs, cast_out_specs, cast_shapes = _cast_plumbing(casts, A_HEADS * tiles, lambda h, t: h * tiles + t)
    out = pl.pallas_call(
        _with_casts(functools.partial(_dilated_kernel, seq=seq), 5, 1, len(casts)),
        grid=(A_HEADS, tiles),
        in_specs=[
            pl.BlockSpec(memory_space=pltpu.SMEM),
            pl.BlockSpec((A_SUPER, HEAD_DIM), lambda h, t: (t, h)),
            pl.BlockSpec((seq, HEAD_DIM), lambda h, t: (0, A_HEADS + h)),
            pl.BlockSpec((seq, HEAD_DIM), lambda h, t: (0, 2 * A_HEADS + h)),
            pl.BlockSpec((nb, A_QBLOCK, A_KBLOCK), lambda h, t: (0, 0, 0)),
        ] + cast_in_specs,
        out_specs=[pl.BlockSpec((A_SUPER, HEAD_DIM), lambda h, t: (t, h))] + cast_out_specs,
        out_shape=[jax.ShapeDtypeStruct((seq, A_WIDTH), BF16)] + cast_shapes,
        scratch_shapes=[
            pltpu.VMEM((total_rows, HEAD_DIM), BF16),
            pltpu.VMEM((total_rows, HEAD_DIM), BF16),
            pltpu.VMEM((seq // DILATED_BRANCHES[1][1], HEAD_DIM), F32),
            pltpu.VMEM((nb, A_QBLOCK, A_KBLOCK), F32),
            pltpu.VMEM((nb, A_SUPER, HEAD_DIM), F32),
            pltpu.VMEM((nb, A_SUPER, HEAD_DIM), F32),
            pltpu.VMEM((nb, A_SUPER, HEAD_DIM), F32),
        ],
        compiler_params=_cparams(("arbitrary", "arbitrary")),
    )(t5_table.astype(F32).reshape(-1), proj, proj, proj, bucket_tabs, *[arr for arr, _ in casts])
    return out[0], list(out[1:])


def _s5_discretise(lam_re, lam_im, log_step, b_re, b_im, c_re, c_im):
    g, p = lam_re.shape[1:]
    slabs = g // (2 * S5_PAIRS)
    step = jnp.exp(log_step.astype(F32))[..., None]
    lr = jnp.minimum(lam_re.astype(F32), -1e-4)
    li = lam_im.astype(F32)
    mag = jnp.exp(lr * step)
    ab_re = mag * jnp.cos(li * step)
    ab_im = mag * jnp.sin(li * step)
    den = lr * lr + li * li
    zr = ((ab_re - 1.0) * lr + ab_im * li) / den
    zi = (ab_im * lr - (ab_re - 1.0) * li) / den
    br = b_re.astype(F32)[None]
    bi = b_im.astype(F32)[None]
    bb_re = zr[..., None] * br - zi[..., None] * bi
    bb_im = zr[..., None] * bi + zi[..., None] * br

    a_re = ab_re.reshape(2, slabs, S5_PAIRS, 2 * p)
    a_im = ab_im.reshape(2, slabs, S5_PAIRS, 2 * p)

    eye2 = jnp.eye(2, dtype=F32)

    def pack_b(bb):
        bb = bb.reshape(2, slabs, S5_PAIRS, 2, p, B_GROUP)
        return jnp.einsum('dsjepc,ef->dsjecfp', bb, eye2).reshape(2, slabs, S5_SLAB, 2 * p)

    def pack_c(cc):
        cc = cc.astype(F32).reshape(2, slabs, S5_PAIRS, 2, B_GROUP, p)
        return jnp.einsum('dsjecp,ef->dsfpjec', cc, eye2).reshape(2, slabs, 2 * p, S5_SLAB)

    b_mat = jnp.concatenate([pack_b(bb_re), pack_b(bb_im)], axis=-1)
    c_mat = jnp.concatenate([pack_c(c_re), -pack_c(c_im)], axis=-2)
    return a_re, a_im, b_mat.astype(BF16), c_mat.astype(BF16)


def _s5_kernel(uf_ref, ub_ref, b_ref, c_ref, are_ref, aim_ref, yf_ref, yb_ref,
               xr_ref, xi_ref, bur_ref, bui_ref, *, tm, slabs):
    c = pl.program_id(0)
    nstate = 2 * B_STATE
    u_refs = (uf_ref, ub_ref)
    y_refs = (yf_ref, yb_ref)

    @pl.when(c == 0)
    def _():
        xr_ref[...] = jnp.zeros_like(xr_ref)
        xi_ref[...] = jnp.zeros_like(xi_ref)

    lane_pair = lax.broadcasted_iota(jnp.int32, (1, S5_SLAB), 1) // (2 * B_GROUP)

    parts = S5_PARTS
    tp = tm // parts

    def part_t0(d, part):
        return part * tp if d == 0 else (parts - 1 - part) * tp

    def pair_rows(t0, j):
        return pl.ds(t0 * S5_PITCH + j, tp, stride=S5_PITCH)

    def project_in(part):
        for d in range(2):
            t0 = part_t0(d, part)
            for s in range(slabs):
                u16 = u_refs[d][pl.ds(t0, tp), s * S5_SLAB:(s + 1) * S5_SLAB].astype(BF16)
                lhs = jnp.concatenate([jnp.where(lane_pair == j, u16, jnp.zeros_like(u16))
                                       for j in range(S5_PAIRS)], axis=0)
                bu = jnp.dot(lhs, b_ref[d, s], preferred_element_type=F32)
                for j in range(S5_PAIRS):
                    bur_ref[d, s, pair_rows(t0, j), :] = bu[j * tp:(j + 1) * tp, :nstate]
                    bui_ref[d, s, pair_rows(t0, j), :] = bu[j * tp:(j + 1) * tp, nstate:]

    ar = [[are_ref[d, s] for s in range(slabs)] for d in range(2)]
    ai = [[aim_ref[d, s] for s in range(slabs)] for d in range(2)]

    def step(i, carry):
        new = []
        for d in range(2):
            t = i if d == 0 else tm - 1 - i
            start = t * S5_PITCH
            rows = pl.ds(pl.multiple_of(start, 8) if S5_PITCH % 8 == 0 else start, S5_PAIRS)
            for s in range(slabs):
                k = 2 * (d * slabs + s)
                xr, xi = carry[k], carry[k + 1]
                nxr = ar[d][s] * xr - ai[d][s] * xi + bur_ref[d, s, rows, :]
                nxi = ar[d][s] * xi + ai[d][s] * xr + bui_ref[d, s, rows, :]
                bur_ref[d, s, rows, :] = nxr
                bui_ref[d, s, rows, :] = nxi
                new += [nxr, nxi]
        return tuple(new)

    def project_out(part):
        for d in range(2):
            t0 = part_t0(d, part)
            for s in range(slabs):
                xcat = jnp.concatenate(
                    [jnp.concatenate([bur_ref[d, s, pair_rows(t0, j), :], bui_ref[d, s, pair_rows(t0, j), :]],
                                     axis=1).astype(BF16) for j in range(S5_PAIRS)], axis=0)
                r = jnp.dot(xcat, c_ref[d, s], preferred_element_type=F32)
                y = r[:tp]
                for j in range(1, S5_PAIRS):
                    y = jnp.where(lane_pair == j, r[j * tp:(j + 1) * tp], y)
                y_refs[d][pl.ds(t0, tp), s * S5_SLAB:(s + 1) * S5_SLAB] = y

    init = []
    for d in range(2):
        for s in range(slabs):
            init += [xr_ref[d, s], xi_ref[d, s]]
    carry = tuple(init)
    for part in range(parts):
        project_in(part)
    for part in range(parts):
        for i in range(part * tp, (part + 1) * tp):
            carry = step(i, carry)
        project_out(part)
    for d in range(2):
        for s in range(slabs):
            k = 2 * (d * slabs + s)
            xr_ref[d, s] = carry[k]
            xi_ref[d, s] = carry[k + 1]


def s5_scan(proj, u_col0, a_re, a_im, b_mat, c_mat, seq, casts=(), tm=256):
    slabs = a_re.shape[1]
    width = slabs * S5_SLAB
    nchunks = seq // tm
    assert u_col0 % width == 0
    ucol = u_col0 // width
    wspec = pl.BlockSpec((2, slabs, S5_SLAB, S5_SLAB), lambda c: (0, 0, 0, 0))
    aspec = pl.BlockSpec((2, slabs, S5_PAIRS, 2 * B_STATE), lambda c: (0, 0, 0, 0))
    state = pltpu.VMEM((2, slabs, S5_PAIRS, 2 * B_STATE), F32)
    drive = pltpu.VMEM((2, slabs, tm * S5_PITCH, 2 * B_STATE), F32)
    cast_in_specs, cast_out_specs, cast_shapes = _cast_plumbing(casts, nchunks, lambda c: c)
    out = pl.pallas_call(
        _with_casts(functools.partial(_s5_kernel, tm=tm, slabs=slabs), 6, 2, len(casts)),
        grid=(nchunks,),
        in_specs=[pl.BlockSpec((tm, width), lambda c: (c, ucol)),
                  pl.BlockSpec((tm, width), lambda c: (nchunks - 1 - c, ucol)),
                  wspec, wspec, aspec, aspec] + cast_in_specs,
        out_specs=[pl.BlockSpec((tm, width), lambda c: (c, 0)),
                   pl.BlockSpec((tm, width), lambda c: (nchunks - 1 - c, 0))] + cast_out_specs,
        out_shape=[jax.ShapeDtypeStruct((seq, width), F32), jax.ShapeDtypeStruct((seq, width), F32)] + cast_shapes,
        scratch_shapes=[state, state, drive, drive],
        compiler_params=_cparams(("arbitrary",)),
    )(proj, proj, b_mat, c_mat, a_re, a_im, *[arr for arr, _ in casts])
    return out[0], out[1], list(out[2:])


def _s5_out_kernel(oa_ref, yf_ref, yb_ref, u_ref, dsk_ref, wg_ref, w_ref, r_ref, g_ref, o_ref, on_ref):
    y = jax.nn.gelu(yf_ref[...] + yb_ref[...] + dsk_ref[...] * u_ref[...])
    z = jnp.dot(y.astype(BF16), wg_ref[...], preferred_element_type=F32)
    ob = (y * jax.nn.sigmoid(z)).astype(BF16)
    ka = oa_ref.shape[1]
    acc = r_ref[...] + jnp.dot(oa_ref[...], w_ref[pl.ds(0, ka), :], preferred_element_type=F32)
    acc = acc + jnp.dot(ob, w_ref[pl.ds(ka, ob.shape[1]), :], preferred_element_type=F32)
    o_ref[...] = acc
    on_ref[...] = _rms_rows(acc, g_ref[...]).astype(BF16)


def s5_glu_out_projection(o_a, yf, yb, proj, u_col0, d_skip, w_glu, glu_layer, w_out, res, g, tm=512):
    m, k = yf.shape
    ka = o_a.shape[1]
    n = w_out.shape[2]
    row = lambda i: (i, 0)
    return pl.pallas_call(
        _s5_out_kernel,
        grid=(m // tm,),
        in_specs=[
            pl.BlockSpec((tm, ka), row),
            pl.BlockSpec((tm, k), row),
            pl.BlockSpec((tm, k), row),
            pl.BlockSpec((tm, k), lambda i: (i, u_col0 // k)),
            pl.BlockSpec((1, k), lambda i: (0, 0)),
            pl.BlockSpec((None, k, k), lambda i: (glu_layer, 0, 0)),
            pl.BlockSpec((None, ka + k, n), lambda i: (0, 0, 0)),
            pl.BlockSpec((tm, n), row),
            pl.BlockSpec((1, n), lambda i: (0, 0)),
        ],
        out_specs=[pl.BlockSpec((tm, n), row), pl.BlockSpec((tm, n), row)],
        out_shape=[jax.ShapeDtypeStruct((m, n), F32), jax.ShapeDtypeStruct((m, n), BF16)],
        compiler_params=_cparams(("arbitrary",)),
    )(o_a, yf, yb, proj, d_skip.reshape(1, k), w_glu, w_out, res, g.reshape(1, n))


def _na_bias_pairs(rpb):
    c = np.arange(GRID_W)
    col_idx = np.clip(c[None, :] - c[:, None] + NA_COLS - 1, 0, 2 * NA_COLS - 2)
    onehot = (col_idx[None] == np.arange(2 * NA_COLS - 1)[:, None, None]).astype(np.float32)
    r2 = jnp.einsum('hdk,kqc->hdqc', rpb.astype(F32), jnp.asarray(onehot), precision=lax.Precision.HIGHEST)
    r2 = jnp.pad(r2, ((0, 0), (4, 4), (0, 0), (0, 0)))
    return jnp.concatenate([r2[:, :-1], r2[:, 1:]], axis=-1)


def _na_mask_tables():
    qr = np.arange(NA_QROWS)[:, None, None, None]
    cq = np.arange(GRID_W)[None, :, None, None]
    kr = np.arange(NA_KROWS)[None, None, :, None]
    ck = np.arange(GRID_W)[None, None, None, :]
    cs = np.clip(cq - NA_COLS // 2, 0, GRID_W - NA_COLS)
    col_ok = (ck >= cs) & (ck < cs + NA_COLS)
    row_ok = [(kr < NA_ROWS) & (qr >= 0),
              (kr >= qr) & (kr < qr + NA_ROWS),
              (kr >= NA_KROWS - NA_ROWS) & (qr >= 0)]
    out = [np.where(r & col_ok, 0.0, NEG_INF).reshape(NA_QROWS * GRID_W, NA_KROWS * GRID_W) for r in row_ok]
    return np.stack(out).astype(np.float32)


def _natten_kernel(q_ref, k_ref, v_ref, pair_ref, mask_ref, o_ref, bias_ref, s_ref, p_ref, l_ref, *, groups, rows):
    gs = pl.program_id(1)
    nk = NA_KROWS * GRID_W
    tq = NA_QROWS * GRID_W

    @pl.when(gs == 0)
    def _():
        for hh in range(NA_HEADS_PER_STEP):
            for v in range(3):
                for qr in range(NA_QROWS):
                    q_rows = pl.ds(qr * GRID_W, GRID_W)
                    for kr2 in range(NA_KROWS // 2):
                        k_cols = pl.ds(kr2 * 128, 128)
                        bias_ref[hh, v, q_rows, k_cols] = (
                            pair_ref[hh, 2 * (kr2 + 2 * (2 - v)) - qr + 3] + mask_ref[v, q_rows, k_cols]) * LOG2E

    blocks = [(gi, hh) for gi in range(NA_GROUPS_PER_STEP) for hh in range(NA_HEADS_PER_STEP)]

    def coords(gi):
        g = gs * NA_GROUPS_PER_STEP + gi
        key_row0 = jnp.clip(NA_QROWS * g - NA_ROWS // 2, 0, rows - NA_KROWS)
        tok0 = pl.multiple_of(key_row0 * GRID_W, GRID_W)
        variant = jnp.where(g == 0, 0, jnp.where(g == groups - 1, 2, 1))
        return tok0, variant

    def stage_scores(b):
        gi, hh = blocks[b]
        tok0, variant = coords(gi)
        cols = pl.ds(hh * HEAD_DIM, HEAD_DIM)
        s = lax.dot_general(q_ref[pl.ds(gi * tq, tq), cols], k_ref[pl.ds(tok0, nk), cols], (((1,), (1,)), ((), ())),
                            preferred_element_type=F32)
        s_ref[b % NA_STAGE_SLOTS] = s + bias_ref[hh, variant]

    def stage_softmax(b):
        s = s_ref[b % NA_STAGE_SLOTS]
        m = jnp.max(s, axis=1, keepdims=True)
        p = jnp.exp2(s - m)
        l_ref[b % NA_STAGE_SLOTS] = jnp.broadcast_to(jnp.sum(p, axis=1, keepdims=True), (tq, HEAD_DIM))
        p_ref[b % NA_STAGE_SLOTS] = p.astype(BF16)

    def stage_out(b):
        gi, hh = blocks[b]
        tok0, _ = coords(gi)
        cols = pl.ds(hh * HEAD_DIM, HEAD_DIM)
        o = jnp.dot(p_ref[b % NA_STAGE_SLOTS], v_ref[pl.ds(tok0, nk), cols], preferred_element_type=F32)
        o_ref[pl.ds(gi * tq, tq), cols] = (o / l_ref[b % NA_STAGE_SLOTS]).astype(o_ref.dtype)

    for step in range(len(blocks) + 2):
        if step < len(blocks):
            stage_scores(step)
        if 0 <= step - 1 < len(blocks):
            stage_softmax(step - 1)
        if 0 <= step - 2 < len(blocks):
            stage_out(step - 2)


def neighbourhood_attention(qkv, bias_pairs, mask_tabs, seq, heads, casts=()):
    rows = seq // GRID_W
    assert rows % (NA_QROWS * NA_GROUPS_PER_STEP) == 0 and rows >= NA_KROWS + NA_QROWS
    assert heads % NA_HEADS_PER_STEP == 0
    groups = rows // NA_QROWS
    tq = NA_QROWS * GRID_W * NA_GROUPS_PER_STEP
    hw = NA_HEADS_PER_STEP * HEAD_DIM
    hsteps = heads // NA_HEADS_PER_STEP
    gsteps = groups // NA_GROUPS_PER_STEP
    cast_in_specs, cast_out_specs, cast_shapes = _cast_plumbing(casts, hsteps * gsteps, lambda h, g: h * gsteps + g)
    out = pl.pallas_call(
        _with_casts(functools.partial(_natten_kernel, groups=groups, rows=rows), 5, 1, len(casts)),
        grid=(hsteps, gsteps),
        in_specs=[
            pl.BlockSpec((tq, hw), lambda h, g: (g, h)),
            pl.BlockSpec((seq, hw), lambda h, g: (0, hsteps + h)),
            pl.BlockSpec((seq, hw), lambda h, g: (0, 2 * hsteps + h)),
            pl.BlockSpec((NA_HEADS_PER_STEP,) + bias_pairs.shape[1:], lambda h, g: (h, 0, 0, 0)),
            pl.BlockSpec(mask_tabs.shape, lambda h, g: (0, 0, 0)),
        ] + cast_in_specs,
        out_specs=[pl.BlockSpec((tq, hw), lambda h, g: (g, h))] + cast_out_specs,
        out_shape=[jax.ShapeDtypeStruct((seq, heads * HEAD_DIM), BF16)] + cast_shapes,
        scratch_shapes=[pltpu.VMEM((NA_HEADS_PER_STEP, 3, NA_QROWS * GRID_W, NA_KROWS * GRID_W), F32),
                        pltpu.VMEM((NA_STAGE_SLOTS, NA_QROWS * GRID_W, NA_KROWS * GRID_W), F32),
                        pltpu.VMEM((NA_STAGE_SLOTS, NA_QROWS * GRID_W, NA_KROWS * GRID_W), BF16),
                        pltpu.VMEM((NA_STAGE_SLOTS, NA_QROWS * GRID_W, HEAD_DIM), F32)],
        compiler_params=_cparams(("arbitrary", "arbitrary")),
    )(qkv, qkv, qkv, bias_pairs, mask_tabs, *[arr for arr, _ in casts])
    return out[0], list(out[1:])


def kernel(x, t5_bias, ab_w_in, ab_w_out, s5_lam_re, s5_lam_im, s5_log_step, s5_b_re, s5_b_im, s5_c_re, s5_c_im, s5_d, s5_w_glu, c_w_qkv, c_w_out, c_rpb, norm_mix, norm_mlp, mlp_w1, mlp_w2, norm_final):
    batch, seq, d_model = x.shape
    depth = norm_mix.shape[0]
    c_heads = c_rpb.shape[1]
    t5_buckets = _t5_bucket_tables()
    na_mask = jnp.asarray(_na_mask_tables())
    attn_scale = LOG2E / math.sqrt(HEAD_DIM)
    w_glu = s5_w_glu.astype(BF16)
    s5_params = jax.vmap(_s5_discretise)(s5_lam_re, s5_lam_im, s5_log_step, s5_b_re, s5_b_im, s5_c_re, s5_c_im)
    na_pairs = jax.vmap(_na_bias_pairs)(c_rpb)

    def mixer_weights(i):
        return ((ab_w_in, ab_w_out) if i % 2 == 0 else (c_w_qkv, c_w_out)), i // 2

    (mix_in0, mix_out0), _ = mixer_weights(0)
    mix_bf = {0: [mix_in0[:1].astype(BF16), None]}
    outs = []
    for bi in range(batch):
        h = x[bi]
        for i in range(depth):
            j = i // 2
            w_mix_in, w_mix_out = mix_bf[i]
            first_casts = [(mix_out0, 0)] if w_mix_out is None else []
            if i % 2 == 0:
                proj, first_out = norm_matmul(h, norm_mix[i], w_mix_in, 0, F32, casts=first_casts)
                if first_out:
                    mix_bf[i][1] = w_mix_out = first_out[0]
                o_a, _ = dilated_attention(proj, t5_bias, t5_buckets, seq)
                a_re, a_im, b_mat, c_mat = [p[j] for p in s5_params]
                yf, yb, (w1, w2) = s5_scan(proj, 3 * A_WIDTH, a_re, a_im, b_mat, c_mat, seq,
                                           casts=[(mlp_w1, i), (mlp_w2, i)])
                h, hn = s5_glu_out_projection(o_a, yf, yb, proj, 3 * A_WIDTH, s5_d[j], w_glu, j, w_mix_out, h,
                                              norm_mlp[i])
            else:
                qkv, first_out = norm_matmul(h, norm_mix[i], w_mix_in, 0, BF16, scaled_cols=c_heads * HEAD_DIM,
                                             scale=attn_scale, casts=first_casts)
                if first_out:
                    mix_bf[i][1] = w_mix_out = first_out[0]
                o, (w1, w2) = neighbourhood_attention(qkv, na_pairs[j], na_mask, seq, c_heads,
                                                      casts=[(mlp_w1, i), (mlp_w2, i)])
                h, hn = matmul_residual(o, w_mix_out, 0, h, norm_mlp[i])
            casts = []
            if i + 1 < depth and i + 1 not in mix_bf:
                (nxt_in, nxt_out), jn = mixer_weights(i + 1)
                casts = [(nxt_in, jn), (nxt_out, jn)]
            h, cast_out = mlp(h, hn, w1, w2, 0, casts)
            if casts:
                mix_bf[i + 1] = cast_out
        outs.append(rmsnorm(h, norm_final))
    return jnp.stack(outs)
```

```python
import functools
import math

import numpy as np
import jax
import jax.numpy as jnp
from jax import lax
from jax.experimental import pallas as pl
from jax.experimental.pallas import tpu as pltpu

F32 = jnp.float32
BF16 = jnp.bfloat16

HEAD_DIM = 128
A_HEADS = 8
A_WIDTH = A_HEADS * HEAD_DIM
DILATED_BRANCHES = ((128, 1), (512, 4), (2048, 16))
A_QBLOCK = 128
A_HALF = 64
A_KBLOCK = A_QBLOCK + 2 * A_HALF
A_SUPER = 2048
A_MERGE_ROWS = 256
B_GROUP = 16
B_STATE = 64
S5_SLAB = 256
S5_PAIRS = 8
S5_PITCH = 9
S5_PARTS = 2
GRID_W = 64
NA_ROWS = 8
NA_COLS = 16
NA_QROWS = 4
NA_KROWS = 12
NA_EXT = NA_KROWS + 8
NA_HEADS_PER_STEP = 2
NA_GROUPS_PER_STEP = 8
NA_STAGE_SLOTS = 4
T5_BUCKETS = 32
T5_MAX_DISTANCE = 1024
RMS_EPS = 1e-6
NEG_INF = -1e30
LOG2E = math.log2(math.e)
V7X_VMEM_BYTES = 64 * 1024 * 1024
VMEM_LIMIT = V7X_VMEM_BYTES - 8 * 1024 * 1024


def _cparams(sem):
    return pltpu.CompilerParams(dimension_semantics=sem, vmem_limit_bytes=VMEM_LIMIT)


def _rms_rows(x, g):
    y = x * lax.rsqrt(jnp.mean(x * x, axis=-1, keepdims=True) + RMS_EPS)
    return y * g


def _with_casts(body, n_in, n_out, ncast):
    def wrapped(*refs):
        ins = refs[:n_in]
        cast_in = refs[n_in:n_in + ncast]
        outs = refs[n_in + ncast:n_in + ncast + n_out]
        cast_out = refs[n_in + ncast + n_out:n_in + 2 * ncast + n_out]
        for src, dst in zip(cast_in, cast_out):
            dst[...] = src[...].astype(BF16)
        body(*ins, *outs, *refs[n_in + 2 * ncast + n_out:])
    return wrapped


def _cast_plumbing(casts, nblocks, block_id):
    in_specs, out_specs, shapes = [], [], []
    for arr, layer in casts:
        _, rows, cols = arr.shape
        assert rows % (16 * nblocks) == 0
        blk = (None, rows // nblocks, cols)
        in_specs.append(pl.BlockSpec(blk, lambda *g, layer=layer: (layer, block_id(*g), 0)))
        out_specs.append(pl.BlockSpec(blk, lambda *g: (0, block_id(*g), 0)))
        shapes.append(jax.ShapeDtypeStruct((1, rows, cols), BF16))
    return in_specs, out_specs, shapes


def _serpentine(i, j, n):
    return jnp.where(i % 2 == 0, j, n - 1 - j)


def _norm_matmul_kernel(x_ref, g_ref, w_ref, o_ref, xn_ref, *, col_blocks, scaled_blocks, scale):
    j = pl.program_id(1)

    @pl.when(j == 0)
    def _():
        xn_ref[...] = _rms_rows(x_ref[...], g_ref[...]).astype(BF16)

    acc = jnp.dot(xn_ref[...], w_ref[...], preferred_element_type=F32)
    if scaled_blocks:
        acc = acc * jnp.where(_serpentine(pl.program_id(0), j, col_blocks) < scaled_blocks, scale, 1.0)
    o_ref[...] = acc.astype(o_ref.dtype)


def norm_matmul(x, g, w, layer, out_dtype, scaled_cols=0, scale=1.0, tm=1024, tn=1024):
    m, k = x.shape
    n = w.shape[2]
    assert scaled_cols % tn == 0
    nj = n // tn
    return pl.pallas_call(
        functools.partial(_norm_matmul_kernel, col_blocks=nj, scaled_blocks=scaled_cols // tn, scale=scale),
        grid=(m // tm, nj),
        in_specs=[
            pl.BlockSpec((tm, k), lambda i, j: (i, 0)),
            pl.BlockSpec((1, k), lambda i, j: (0, 0)),
            pl.BlockSpec((None, k, tn), lambda i, j: (layer, 0, _serpentine(i, j, nj))),
        ],
        out_specs=pl.BlockSpec((tm, tn), lambda i, j: (i, _serpentine(i, j, nj))),
        out_shape=jax.ShapeDtypeStruct((m, n), out_dtype),
        scratch_shapes=[pltpu.VMEM((tm, k), BF16)],
        compiler_params=_cparams(("arbitrary", "arbitrary")),
    )(x, g.reshape(1, k), w)


def _matmul_residual_kernel(a_ref, w_ref, r_ref, g_ref, o_ref, on_ref):
    acc = r_ref[...] + jnp.dot(a_ref[...], w_ref[...], preferred_element_type=F32)
    o_ref[...] = acc
    on_ref[...] = _rms_rows(acc, g_ref[...]).astype(BF16)


def matmul_residual(a, w, layer, res, g, tm=512):
    m = res.shape[0]
    _, k, n = w.shape
    row = lambda i: (i, 0)
    return pl.pallas_call(
        _matmul_residual_kernel,
        grid=(m // tm,),
        in_specs=[
            pl.BlockSpec((tm, k), row),
            pl.BlockSpec((None, k, n), lambda i: (layer, 0, 0)),
            pl.BlockSpec((tm, n), row),
            pl.BlockSpec((1, n), lambda i: (0, 0)),
        ],
        out_specs=[pl.BlockSpec((tm, n), row), pl.BlockSpec((tm, n), row)],
        out_shape=[jax.ShapeDtypeStruct((m, n), F32), jax.ShapeDtypeStruct((m, n), BF16)],
        compiler_params=_cparams(("arbitrary",)),
    )(a, w, res, g.reshape(1, n))


def _mlp_kernel(xn_ref, xres_ref, w1_ref, w2_ref, o_ref, h_ref, *, up_steps, tf):
    i = pl.program_id(0)
    s = pl.program_id(1)

    @pl.when(s < up_steps)
    def _():
        h = jnp.dot(xn_ref[...], w1_ref[...], preferred_element_type=F32)
        h_ref[_serpentine(i, s, up_steps)] = jnp.square(jnp.maximum(h, 0.0)).astype(BF16)

    @pl.when(s >= up_steps)
    def _():
        acc = xres_ref[...]
        for c in range(up_steps):
            acc = acc + jnp.dot(h_ref[c], w2_ref[pl.ds(c * tf, tf), :], preferred_element_type=F32)
        o_ref[...] = acc


def mlp(x, xn, w1, w2, layer, casts=(), tm=1024, tf=1024, tn=256):
    m, d = x.shape
    dff = w1.shape[2]
    up_steps = dff // tf
    down_steps = d // tn
    cast_in_specs, cast_out_specs, cast_shapes = _cast_plumbing(
        casts, (m // tm) * up_steps, lambda i, s: i * up_steps + jnp.minimum(s, up_steps - 1))

    def up_block(i, s):
        return _serpentine(i, jnp.minimum(s, up_steps - 1), up_steps)

    def down_block(i, s):
        return _serpentine(i, jnp.maximum(s - up_steps, 0), down_steps)

    out = pl.pallas_call(
        _with_casts(functools.partial(_mlp_kernel, up_steps=up_steps, tf=tf), 4, 1, len(casts)),
        grid=(m // tm, up_steps + down_steps),
        in_specs=[
            pl.BlockSpec((tm, d), lambda i, s: (i, 0)),
            pl.BlockSpec((tm, tn), lambda i, s: (i, down_block(i, s))),
            pl.BlockSpec((None, d, tf), lambda i, s: (layer, 0, up_block(i, s))),
            pl.BlockSpec((None, dff, tn), lambda i, s: (layer, 0, down_block(i, s))),
        ] + cast_in_specs,
        out_specs=[pl.BlockSpec((tm, tn), lambda i, s: (i, down_block(i, s)))] + cast_out_specs,
        out_shape=[jax.ShapeDtypeStruct((m, d), F32)] + cast_shapes,
        scratch_shapes=[pltpu.VMEM((up_steps, tm, tf), BF16)],
        compiler_params=_cparams(("arbitrary", "arbitrary")),
    )(xn, x, w1, w2, *[arr for arr, _ in casts])
    return out[0], list(out[1:])


def _rmsnorm_kernel(x_ref, g_ref, o_ref):
    o_ref[...] = _rms_rows(x_ref[...], g_ref[...])


def rmsnorm(x, g, tm=512):
    m, d = x.shape
    return pl.pallas_call(
        _rmsnorm_kernel,
        grid=(m // tm,),
        in_specs=[pl.BlockSpec((tm, d), lambda i: (i, 0)), pl.BlockSpec((1, d), lambda i: (0, 0))],
        out_specs=pl.BlockSpec((tm, d), lambda i: (i, 0)),
        out_shape=jax.ShapeDtypeStruct((m, d), F32),
        compiler_params=_cparams(("arbitrary",)),
    )(x, g.reshape(1, d))


def _t5_bucket(rel):
    half = T5_BUCKETS // 2
    max_exact = half // 2
    n = jnp.abs(rel)
    nf = jnp.maximum(n, 1).astype(F32)
    large = max_exact + (jnp.log(nf / max_exact) / math.log(T5_MAX_DISTANCE / max_exact)
                         * (half - max_exact)).astype(jnp.int32)
    large = jnp.minimum(large, half - 1)
    return jnp.where(rel > 0, half, 0) + jnp.where(n < max_exact, n, large)


def _t5_bucket_tables():
    tabs = []
    for _, dil in DILATED_BRANCHES:
        off = jnp.arange(A_KBLOCK)[None, :] - A_HALF - jnp.arange(A_QBLOCK)[:, None]
        tabs.append(jnp.where(jnp.abs(off) <= A_HALF, _t5_bucket(off * dil), T5_BUCKETS))
    return jnp.stack(tabs).astype(jnp.int32)


def _dilated_sections(seq):
    bases, sizes, row = [], [], 0
    for _, dil in DILATED_BRANCHES:
        sec = seq // dil + 2 * A_HALF
        bases.append(row)
        sizes.append(sec)
        row += dil * sec
    return bases, sizes, row


def _dilated_kernel(t5_ref, q_ref, k_ref, v_ref, bucket_ref, o_ref,
                    kd_ref, vd_ref, tmp_ref, bias_ref, m_ref, l_ref, n_ref, *, seq):
    h = pl.program_id(0)
    t = pl.program_id(1)
    scale = LOG2E / math.sqrt(HEAD_DIM)
    bases, sizes, _ = _dilated_sections(seq)

    @pl.when(t == 0)
    def _():
        for b in range(len(DILATED_BRANCHES)):
            bk = bucket_ref[b]
            acc = jnp.full((A_QBLOCK, A_KBLOCK), NEG_INF, F32)
            for kbkt in range(T5_BUCKETS):
                acc = jnp.where(bk == kbkt, t5_ref[kbkt * A_HEADS + h], acc)
            bias_ref[b] = acc * LOG2E
        zeros = jnp.zeros((A_HALF, HEAD_DIM), BF16)
        (_, d0), (_, d1), (_, d2) = DILATED_BRANCHES
        assert d0 == 1 and d2 % d1 == 0
        ratio = d2 // d1

        def put(dst, b, r, rows):
            o = bases[b] + r * sizes[b]
            dst[pl.ds(o, A_HALF), :] = zeros
            dst[pl.ds(o + A_HALF + rows.shape[0], A_HALF), :] = zeros
            dst[pl.ds(o + A_HALF, rows.shape[0]), :] = rows.astype(BF16)

        for src, dst in ((k_ref, kd_ref), (v_ref, vd_ref)):
            put(dst, 0, 0, src[...])
            for r1 in range(d1):
                rows1 = src[pl.ds(r1, seq // d1, stride=d1), :]
                put(dst, 1, r1, rows1)
                tmp_ref[...] = rows1
                for a in range(ratio):
                    put(dst, 2, a * d1 + r1, tmp_ref[pl.ds(a, seq // d2, stride=ratio), :])

    kk = lax.broadcasted_iota(jnp.int32, (1, A_KBLOCK), 1)
    for b, (_, dil) in enumerate(DILATED_BRANCHES):
        sub_len = seq // dil
        blocks_per_residue = A_SUPER // dil // A_QBLOCK

        def body(idx, carry, b=b, dil=dil, sub_len=sub_len, blocks_per_residue=blocks_per_residue):
            r = idx // blocks_per_residue
            n = idx % blocks_per_residue
            qs = r + n * (A_QBLOCK * dil)
            q_idx = pl.ds(qs, A_QBLOCK) if dil == 1 else pl.ds(qs, A_QBLOCK, stride=dil)
            blk = t * blocks_per_residue + n
            k_idx = pl.ds(pl.multiple_of(bases[b] + r * sizes[b] + blk * A_QBLOCK, A_QBLOCK), A_KBLOCK)
            qb = (q_ref[q_idx, :] * scale).astype(BF16)
            s = lax.dot_general(qb, kd_ref[k_idx, :], (((1,), (1,)), ((), ())), preferred_element_type=F32)
            key_l = blk * A_QBLOCK - A_HALF + kk
            edge = jnp.where((key_l >= 0) & (key_l < sub_len), 0.0, NEG_INF)
            s = s + bias_ref[b] + edge
            m = jnp.max(s, axis=1, keepdims=True)
            p = jnp.exp2(s - m)
            l = jnp.sum(p, axis=1, keepdims=True)
            num = jnp.dot(p.astype(BF16), vd_ref[k_idx, :], preferred_element_type=F32)
            m_ref[b, q_idx, :] = jnp.broadcast_to(m, (A_QBLOCK, HEAD_DIM))
            l_ref[b, q_idx, :] = jnp.broadcast_to(l, (A_QBLOCK, HEAD_DIM))
            n_ref[b, q_idx, :] = num
            return carry

        lax.fori_loop(0, A_SUPER // A_QBLOCK, body, 0, unroll=A_SUPER // A_QBLOCK)

    rows = A_MERGE_ROWS
    for c in range(A_SUPER // rows):
        sl = pl.ds(c * rows, rows)
        m0, m1, m2 = m_ref[0, sl, :], m_ref[1, sl, :], m_ref[2, sl, :]
        mx = jnp.maximum(jnp.maximum(m0, m1), m2)
        w0, w1, w2 = jnp.exp2(m0 - mx), jnp.exp2(m1 - mx), jnp.exp2(m2 - mx)
        num = w0 * n_ref[0, sl, :] + w1 * n_ref[1, sl, :] + w2 * n_ref[2, sl, :]
        den = w0 * l_ref[0, sl, :] + w1 * l_ref[1, sl, :] + w2 * l_ref[2, sl, :]
        o_ref[sl, :] = (num / den).astype(o_ref.dtype)


def dilated_attention(proj, t5_table, bucket_tabs, seq, casts=()):
    assert seq % A_SUPER == 0
    nb = len(DILATED_BRANCHES)
    total_rows = _dilated_sections(seq)[2]
    tiles = seq // A_SUPER
    cast_in_specs, cast_out_specs, cast_shapes = _cast_plumbing(casts, A_HEADS * tiles, lambda h, t: h * tiles + t)
    out = pl.pallas_call(
        _with_casts(functools.partial(_dilated_kernel, seq=seq), 5, 1, len(casts)),
        grid=(A_HEADS, tiles),
        in_specs=[
            pl.BlockSpec(memory_space=pltpu.SMEM),
            pl.BlockSpec((A_SUPER, HEAD_DIM), lambda h, t: (t, h)),
            pl.BlockSpec((seq, HEAD_DIM), lambda h, t: (0, A_HEADS + h)),
            pl.BlockSpec((seq, HEAD_DIM), lambda h, t: (0, 2 * A_HEADS + h)),
            pl.BlockSpec((nb, A_QBLOCK, A_KBLOCK), lambda h, t: (0, 0, 0)),
        ] + cast_in_specs,
        out_specs=[pl.BlockSpec((A_SUPER, HEAD_DIM), lambda h, t: (t, h))] + cast_out_specs,
        out_shape=[jax.ShapeDtypeStruct((seq, A_WIDTH), BF16)] + cast_shapes,
        scratch_shapes=[
            pltpu.VMEM((total_rows, HEAD_DIM), BF16),
            pltpu.VMEM((total_rows, HEAD_DIM), BF16),
            pltpu.VMEM((seq // DILATED_BRANCHES[1][1], HEAD_DIM), F32),
            pltpu.VMEM((nb, A_QBLOCK, A_KBLOCK), F32),
            pltpu.VMEM((nb, A_SUPER, HEAD_DIM), F32),
            pltpu.VMEM((nb, A_SUPER, HEAD_DIM), F32),
            pltpu.VMEM((nb, A_SUPER, HEAD_DIM), F32),
        ],
        compiler_params=_cparams(("arbitrary", "arbitrary")),
    )(t5_table.astype(F32).reshape(-1), proj, proj, proj, bucket_tabs, *[arr for arr, _ in casts])
    return out[0], list(out[1:])


def _s5_discretise(lam_re, lam_im, log_step, b_re, b_im, c_re, c_im):
    g, p = lam_re.shape[1:]
    slabs = g // (2 * S5_PAIRS)
    step = jnp.exp(log_step.astype(F32))[..., None]
    lr = jnp.minimum(lam_re.astype(F32), -1e-4)
    li = lam_im.astype(F32)
    mag = jnp.exp(lr * step)
    ab_re = mag * jnp.cos(li * step)
    ab_im = mag * jnp.sin(li * step)
    den = lr * lr + li * li
    zr = ((ab_re - 1.0) * lr + ab_im * li) / den
    zi = (ab_im * lr - (ab_re - 1.0) * li) / den
    br = b_re.astype(F32)[None]
    bi = b_im.astype(F32)[None]
    bb_re = zr[..., None] * br - zi[..., None] * bi
    bb_im = zr[..., None] * bi + zi[..., None] * br

    a_re = ab_re.reshape(2, slabs, S5_PAIRS, 2 * p)
    a_im = ab_im.reshape(2, slabs, S5_PAIRS, 2 * p)

    eye2 = jnp.eye(2, dtype=F32)

    def pack_b(bb):
        bb = bb.reshape(2, slabs, S5_PAIRS, 2, p, B_GROUP)
        return jnp.einsum('dsjepc,ef->dsjecfp', bb, eye2).reshape(2, slabs, S5_SLAB, 2 * p)

    def pack_c(cc):
        cc = cc.astype(F32).reshape(2, slabs, S5_PAIRS, 2, B_GROUP, p)
        return jnp.einsum('dsjecp,ef->dsfpjec', cc, eye2).reshape(2, slabs, 2 * p, S5_SLAB)

    b_mat = jnp.concatenate([pack_b(bb_re), pack_b(bb_im)], axis=-1)
    c_mat = jnp.concatenate([pack_c(c_re), -pack_c(c_im)], axis=-2)
    return a_re, a_im, b_mat.astype(BF16), c_mat.astype(BF16)


def _s5_kernel(uf_ref, ub_ref, b_ref, c_ref, are_ref, aim_ref, yf_ref, yb_ref,
               xr_ref, xi_ref, bur_ref, bui_ref, *, tm, slabs):
    c = pl.program_id(0)
    nstate = 2 * B_STATE
    u_refs = (uf_ref, ub_ref)
    y_refs = (yf_ref, yb_ref)

    @pl.when(c == 0)
    def _():
        xr_ref[...] = jnp.zeros_like(xr_ref)
        xi_ref[...] = jnp.zeros_like(xi_ref)

    lane_pair = lax.broadcasted_iota(jnp.int32, (1, S5_SLAB), 1) // (2 * B_GROUP)

    parts = S5_PARTS
    tp = tm // parts

    def part_t0(d, part):
        return part * tp if d == 0 else (parts - 1 - part) * tp

    def pair_rows(t0, j):
        return pl.ds(t0 * S5_PITCH + j, tp, stride=S5_PITCH)

    def project_in(part):
        for d in range(2):
            t0 = part_t0(d, part)
            for s in range(slabs):
                u16 = u_refs[d][pl.ds(t0, tp), s * S5_SLAB:(s + 1) * S5_SLAB].astype(BF16)
                lhs = jnp.concatenate([jnp.where(lane_pair == j, u16, jnp.zeros_like(u16))
                                       for j in range(S5_PAIRS)], axis=0)
                bu = jnp.dot(lhs, b_ref[d, s], preferred_element_type=F32)
                for j in range(S5_PAIRS):
                    bur_ref[d, s, pair_rows(t0, j), :] = bu[j * tp:(j + 1) * tp, :nstate]
                    bui_ref[d, s, pair_rows(t0, j), :] = bu[j * tp:(j + 1) * tp, nstate:]

    ar = [[are_ref[d, s] for s in range(slabs)] for d in range(2)]
    ai = [[aim_ref[d, s] for s in range(slabs)] for d in range(2)]

    def step(i, carry):
        new = []
        for d in range(2):
            t = i if d == 0 else tm - 1 - i
            start = t * S5_PITCH
            rows = pl.ds(pl.multiple_of(start, 8) if S5_PITCH % 8 == 0 else start, S5_PAIRS)
            for s in range(slabs):
                k = 2 * (d * slabs + s)
                xr, xi = carry[k], carry[k + 1]
                nxr = ar[d][s] * xr - ai[d][s] * xi + bur_ref[d, s, rows, :]
                nxi = ar[d][s] * xi + ai[d][s] * xr + bui_ref[d, s, rows, :]
                bur_ref[d, s, rows, :] = nxr
                bui_ref[d, s, rows, :] = nxi
                new += [nxr, nxi]
        return tuple(new)

    def project_out(part):
        for d in range(2):
            t0 = part_t0(d, part)
            for s in range(slabs):
                xcat = jnp.concatenate(
                    [jnp.concatenate([bur_ref[d, s, pair_rows(t0, j), :], bui_ref[d, s, pair_rows(t0, j), :]],
                                     axis=1).astype(BF16) for j in range(S5_PAIRS)], axis=0)
                r = jnp.dot(xcat, c_ref[d, s], preferred_element_type=F32)
                y = r[:tp]
                for j in range(1, S5_PAIRS):
                    y = jnp.where(lane_pair == j, r[j * tp:(j + 1) * tp], y)
                y_refs[d][pl.ds(t0, tp), s * S5_SLAB:(s + 1) * S5_SLAB] = y

    init = []
    for d in range(2):
        for s in range(slabs):
            init += [xr_ref[d, s], xi_ref[d, s]]
    carry = tuple(init)
    for part in range(parts):
        project_in(part)
    for part in range(parts):
        for i in range(part * tp, (part + 1) * tp):
            carry = step(i, carry)
        project_out(part)
    for d in range(2):
        for s in range(slabs):
            k = 2 * (d * slabs + s)
            xr_ref[d, s] = carry[k]
            xi_ref[d, s] = carry[k + 1]


def s5_scan(proj, u_col0, a_re, a_im, b_mat, c_mat, seq, casts=(), tm=256):
    slabs = a_re.shape[1]
    width = slabs * S5_SLAB
    nchunks = seq // tm
    assert u_col0 % width == 0
    ucol = u_col0 // width
    wspec = pl.BlockSpec((2, slabs, S5_SLAB, S5_SLAB), lambda c: (0, 0, 0, 0))
    aspec = pl.BlockSpec((2, slabs, S5_PAIRS, 2 * B_STATE), lambda c: (0, 0, 0, 0))
    state = pltpu.VMEM((2, slabs, S5_PAIRS, 2 * B_STATE), F32)
    drive = pltpu.VMEM((2, slabs, tm * S5_PITCH, 2 * B_STATE), F32)
    cast_in_specs, cast_out_specs, cast_shapes = _cast_plumbing(casts, nchunks, lambda c: c)
    out = pl.pallas_call(
        _with_casts(functools.partial(_s5_kernel, tm=tm, slabs=slabs), 6, 2, len(casts)),
        grid=(nchunks,),
        in_specs=[pl.BlockSpec((tm, width), lambda c: (c, ucol)),
                  pl.BlockSpec((tm, width), lambda c: (nchunks - 1 - c, ucol)),
                  wspec, wspec, aspec, aspec] + cast_in_specs,
        out_specs=[pl.BlockSpec((tm, width), lambda c: (c, 0)),
                   pl.BlockSpec((tm, width), lambda c: (nchunks - 1 - c, 0))] + cast_out_specs,
        out_shape=[jax.ShapeDtypeStruct((seq, width), F32), jax.ShapeDtypeStruct((seq, width), F32)] + cast_shapes,
        scratch_shapes=[state, state, drive, drive],
        compiler_params=_cparams(("arbitrary",)),
    )(proj, proj, b_mat, c_mat, a_re, a_im, *[arr for arr, _ in casts])
    return out[0], out[1], list(out[2:])


def _s5_out_kernel(oa_ref, yf_ref, yb_ref, u_ref, dsk_ref, wg_ref, w_ref, r_ref, g_ref, o_ref, on_ref):
    y = jax.nn.gelu(yf_ref[...] + yb_ref[...] + dsk_ref[...] * u_ref[...])
    z = jnp.dot(y.astype(BF16), wg_ref[...], preferred_element_type=F32)
    ob = (y * jax.nn.sigmoid(z)).astype(BF16)
    ka = oa_ref.shape[1]
    acc = r_ref[...] + jnp.dot(oa_ref[...], w_ref[pl.ds(0, ka), :], preferred_element_type=F32)
    acc = acc + jnp.dot(ob, w_ref[pl.ds(ka, ob.shape[1]), :], preferred_element_type=F32)
    o_ref[...] = acc
    on_ref[...] = _rms_rows(acc, g_ref[...]).astype(BF16)


def s5_glu_out_projection(o_a, yf, yb, proj, u_col0, d_skip, w_glu, glu_layer, w_out, res, g, tm=512):
    m, k = yf.shape
    ka = o_a.shape[1]
    n = w_out.shape[2]
    row = lambda i: (i, 0)
    return pl.pallas_call(
        _s5_out_kernel,
        grid=(m // tm,),
        in_specs=[
            pl.BlockSpec((tm, ka), row),
            pl.BlockSpec((tm, k), row),
            pl.BlockSpec((tm, k), row),
            pl.BlockSpec((tm, k), lambda i: (i, u_col0 // k)),
            pl.BlockSpec((1, k), lambda i: (0, 0)),
            pl.BlockSpec((None, k, k), lambda i: (glu_layer, 0, 0)),
            pl.BlockSpec((None, ka + k, n), lambda i: (0, 0, 0)),
            pl.BlockSpec((tm, n), row),
            pl.BlockSpec((1, n), lambda i: (0, 0)),
        ],
        out_specs=[pl.BlockSpec((tm, n), row), pl.BlockSpec((tm, n), row)],
        out_shape=[jax.ShapeDtypeStruct((m, n), F32), jax.ShapeDtypeStruct((m, n), BF16)],
        compiler_params=_cparams(("arbitrary",)),
    )(o_a, yf, yb, proj, d_skip.reshape(1, k), w_glu, w_out, res, g.reshape(1, n))


def _na_bias_pairs(rpb):
    c = np.arange(GRID_W)
    col_idx = np.clip(c[None, :] - c[:, None] + NA_COLS - 1, 0, 2 * NA_COLS - 2)
    onehot = (col_idx[None] == np.arange(2 * NA_COLS - 1)[:, None, None]).astype(np.float32)
    r2 = jnp.einsum('hdk,kqc->hdqc', rpb.astype(F32), jnp.asarray(onehot), precision=lax.Precision.HIGHEST)
    r2 = jnp.pad(r2, ((0, 0), (4, 4), (0, 0), (0, 0)))
    return jnp.concatenate([r2[:, :-1], r2[:, 1:]], axis=-1)


def _na_mask_tables():
    qr = np.arange(NA_QROWS)[:, None, None, None]
    cq = np.arange(GRID_W)[None, :, None, None]
    kr = np.arange(NA_KROWS)[None, None, :, None]
    ck = np.arange(GRID_W)[None, None, None, :]
    cs = np.clip(cq - NA_COLS // 2, 0, GRID_W - NA_COLS)
    col_ok = (ck >= cs) & (ck < cs + NA_COLS)
    row_ok = [(kr < NA_ROWS) & (qr >= 0),
              (kr >= qr) & (kr < qr + NA_ROWS),
              (kr >= NA_KROWS - NA_ROWS) & (qr >= 0)]
    out = [np.where(r & col_ok, 0.0, NEG_INF).reshape(NA_QROWS * GRID_W, NA_KROWS * GRID_W) for r in row_ok]
    return np.stack(out).astype(np.float32)


def _natten_kernel(q_ref, k_ref, v_ref, pair_ref, mask_ref, o_ref, bias_ref, s_ref, p_ref, l_ref, *, groups, rows):
    gs = pl.program_id(1)
    nk = NA_KROWS * GRID_W
    tq = NA_QROWS * GRID_W

    @pl.when(gs == 0)
    def _():
        for hh in range(NA_HEADS_PER_STEP):
            for v in range(3):
                for qr in range(NA_QROWS):
                    q_rows = pl.ds(qr * GRID_W, GRID_W)
                    for kr2 in range(NA_KROWS // 2):
                        k_cols = pl.ds(kr2 * 128, 128)
                        bias_ref[hh, v, q_rows, k_cols] = (
                            pair_ref[hh, 2 * (kr2 + 2 * (2 - v)) - qr + 3] + mask_ref[v, q_rows, k_cols]) * LOG2E

    blocks = [(gi, hh) for gi in range(NA_GROUPS_PER_STEP) for hh in range(NA_HEADS_PER_STEP)]

    def coords(gi):
        g = gs * NA_GROUPS_PER_STEP + gi
        key_row0 = jnp.clip(NA_QROWS * g - NA_ROWS // 2, 0, rows - NA_KROWS)
        tok0 = pl.multiple_of(key_row0 * GRID_W, GRID_W)
        variant = jnp.where(g == 0, 0, jnp.where(g == groups - 1, 2, 1))
        return tok0, variant

    def stage_scores(b):
        gi, hh = blocks[b]
        tok0, variant = coords(gi)
        cols = pl.ds(hh * HEAD_DIM, HEAD_DIM)
        s = lax.dot_general(q_ref[pl.ds(gi * tq, tq), cols], k_ref[pl.ds(tok0, nk), cols], (((1,), (1,)), ((), ())),
                            preferred_element_type=F32)
        s_ref[b % NA_STAGE_SLOTS] = s + bias_ref[hh, variant]

    def stage_softmax(b):
        s = s_ref[b % NA_STAGE_SLOTS]
        m = jnp.max(s, axis=1, keepdims=True)
        p = jnp.exp2(s - m)
        l_ref[b % NA_STAGE_SLOTS] = jnp.broadcast_to(jnp.sum(p, axis=1, keepdims=True), (tq, HEAD_DIM))
        p_ref[b % NA_STAGE_SLOTS] = p.astype(BF16)

    def stage_out(b):
        gi, hh = blocks[b]
        tok0, _ = coords(gi)
        cols = pl.ds(hh * HEAD_DIM, HEAD_DIM)
        o = jnp.dot(p_ref[b % NA_STAGE_SLOTS], v_ref[pl.ds(tok0, nk), cols], preferred_element_type=F32)
        o_ref[pl.ds(gi * tq, tq), cols] = (o / l_ref[b % NA_STAGE_SLOTS]).astype(o_ref.dtype)

    for step in range(len(blocks) + 2):
        if step < len(blocks):
            stage_scores(step)
        if 0 <= step - 1 < len(blocks):
            stage_softmax(step - 1)
        if 0 <= step - 2 < len(blocks):
            stage_out(step - 2)


def neighbourhood_attention(qkv, bias_pairs, mask_tabs, seq, heads, casts=()):
    rows = seq // GRID_W
    assert rows % (NA_QROWS * NA_GROUPS_PER_STEP) == 0 and rows >= NA_KROWS + NA_QROWS
    assert heads % NA_HEADS_PER_STEP == 0
    groups = rows // NA_QROWS
    tq = NA_QROWS * GRID_W * NA_GROUPS_PER_STEP
    hw = NA_HEADS_PER_STEP * HEAD_DIM
    hsteps = heads // NA_HEADS_PER_STEP
    gsteps = groups // NA_GROUPS_PER_STEP
    cast_in_specs, cast_out_specs, cast_shapes = _cast_plumbing(casts, hsteps * gsteps, lambda h, g: h * gsteps + g)
    out = pl.pallas_call(
        _with_casts(functools.partial(_natten_kernel, groups=groups, rows=rows), 5, 1, len(casts)),
        grid=(hsteps, gsteps),
        in_specs=[
            pl.BlockSpec((tq, hw), lambda h, g: (g, h)),
            pl.BlockSpec((seq, hw), lambda h, g: (0, hsteps + h)),
            pl.BlockSpec((seq, hw), lambda h, g: (0, 2 * hsteps + h)),
            pl.BlockSpec((NA_HEADS_PER_STEP,) + bias_pairs.shape[1:], lambda h, g: (h, 0, 0, 0)),
            pl.BlockSpec(mask_tabs.shape, lambda h, g: (0, 0, 0)),
        ] + cast_in_specs,
        out_specs=[pl.BlockSpec((tq, hw), lambda h, g: (g, h))] + cast_out_specs,
        out_shape=[jax.ShapeDtypeStruct((seq, heads * HEAD_DIM), BF16)] + cast_shapes,
        scratch_shapes=[pltpu.VMEM((NA_HEADS_PER_STEP, 3, NA_QROWS * GRID_W, NA_KROWS * GRID_W), F32),
                        pltpu.VMEM((NA_STAGE_SLOTS, NA_QROWS * GRID_W, NA_KROWS * GRID_W), F32),
                        pltpu.VMEM((NA_STAGE_SLOTS, NA_QROWS * GRID_W, NA_KROWS * GRID_W), BF16),
                        pltpu.VMEM((NA_STAGE_SLOTS, NA_QROWS * GRID_W, HEAD_DIM), F32)],
        compiler_params=_cparams(("arbitrary", "arbitrary")),
    )(qkv, qkv, qkv, bias_pairs, mask_tabs, *[arr for arr, _ in casts])
    return out[0], list(out[1:])


def kernel(x, t5_bias, ab_w_in, ab_w_out, s5_lam_re, s5_lam_im, s5_log_step, s5_b_re, s5_b_im, s5_c_re, s5_c_im, s5_d, s5_w_glu, c_w_qkv, c_w_out, c_rpb, norm_mix, norm_mlp, mlp_w1, mlp_w2, norm_final):
    batch, seq, d_model = x.shape
    depth = norm_mix.shape[0]
    c_heads = c_rpb.shape[1]
    t5_buckets = _t5_bucket_tables()
    na_mask = jnp.asarray(_na_mask_tables())
    attn_scale = LOG2E / math.sqrt(HEAD_DIM)
    w_glu = s5_w_glu.astype(BF16)

    def mixer_weights(i):
        return ((ab_w_in, ab_w_out) if i % 2 == 0 else (c_w_qkv, c_w_out)), i // 2

    (mix_in0, mix_out0), _ = mixer_weights(0)
    mix_bf = {0: [mix_in0[:1].astype(BF16), mix_out0[:1].astype(BF16)]}
    outs = []
    for bi in range(batch):
        h = x[bi]
        for i in range(depth):
            j = i // 2
            w_mix_in, w_mix_out = mix_bf[i]
            if i % 2 == 0:
                proj = norm_matmul(h, norm_mix[i], w_mix_in, 0, F32)
                o_a, _ = dilated_attention(proj, t5_bias, t5_buckets, seq)
                a_re, a_im, b_mat, c_mat = _s5_discretise(
                    s5_lam_re[j], s5_lam_im[j], s5_log_step[j], s5_b_re[j], s5_b_im[j], s5_c_re[j], s5_c_im[j])
                yf, yb, (w1, w2) = s5_scan(proj, 3 * A_WIDTH, a_re, a_im, b_mat, c_mat, seq,
                                           casts=[(mlp_w1, i), (mlp_w2, i)])
                h, hn = s5_glu_out_projection(o_a, yf, yb, proj, 3 * A_WIDTH, s5_d[j], w_glu, j, w_mix_out, h,
                                              norm_mlp[i])
            else:
                qkv = norm_matmul(h, norm_mix[i], w_mix_in, 0, BF16, scaled_cols=c_heads * HEAD_DIM, scale=attn_scale,
                                  tn=2048)
                o, (w1, w2) = neighbourhood_attention(qkv, _na_bias_pairs(c_rpb[j]), na_mask, seq, c_heads,
                                                      casts=[(mlp_w1, i), (mlp_w2, i)])
                h, hn = matmul_residual(o, w_mix_out, 0, h, norm_mlp[i])
            casts = []
            if i + 1 < depth and i + 1 not in mix_bf:
                (nxt_in, nxt_out), jn = mixer_weights(i + 1)
                casts = [(nxt_in, jn), (nxt_out, jn)]
            h, cast_out = mlp(h, hn, w1, w2, 0, casts)
            if casts:
                mix_bf[i + 1] = cast_out
        outs.append(rmsnorm(h, norm_final))
    return jnp.stack(outs)
```
